```python
import math
import jax
import jax.numpy as jnp
from jax import lax
import numpy as np

D_MODEL = 1024
BATCH = 4
SEQ = 4096
DEPTH = 4

GRID_W = 64
CTX_LEN = 256
N_MOD = 6
HEAD_DIM = 64
N_DIR = 2
RMS_EPS = 1e-6

DN_HEADS = (3 * D_MODEL) // (8 * HEAD_DIM)
DN_HEAD_DIM = HEAD_DIM
DN_WIDTH = DN_HEADS * DN_HEAD_DIM
DN_CONV = 3
DN_CHUNK = 64

SW_Q_HEADS = (3 * D_MODEL) // (8 * HEAD_DIM)
SW_KV_HEADS = SW_Q_HEADS // 3
SW_HEAD_DIM = HEAD_DIM
SW_WIDTH = SW_Q_HEADS * SW_HEAD_DIM
SW_KV_WIDTH = SW_KV_HEADS * SW_HEAD_DIM
SW_WINDOW = 128
SW_BLOCK = 128
ROPE_THETA = 10000.0
NEG_INF = -1e30

HY_CH = D_MODEL - DN_WIDTH - SW_WIDTH
HY_ORDER = 2
HY_CONV = 3
HY_BANDS = 16
HY_EMB = 1 + 2 * HY_BANDS
HY_FFN = 64
HY_FAST_DECAY = 0.3
HY_SLOW_DECAY = 1.5
HY_TARGET = 1e-2

IN_SIZES = (3 * DN_WIDTH, DN_WIDTH, N_DIR * DN_HEADS, N_DIR * DN_HEADS, (HY_ORDER + 1) * HY_CH, SW_WIDTH, SW_KV_WIDTH, SW_KV_WIDTH)
N_IN = sum(IN_SIZES)

DENSE_FF = 256 * ((8 * D_MODEL // 3 + 255) // 256)
N_EXPERTS = 8
TOP_K = 2
EXPERT_FF = 7 * D_MODEL // 2
N_DENSE = (DEPTH + 1) // 2
N_MOE = DEPTH // 2

kernel_name = "hybrid_deltanet_hyena_swa_moe_dit"


def rms_norm(x, g):
    xf = x.astype(jnp.float32)
    y = xf * lax.rsqrt(jnp.mean(xf * xf, axis=-1, keepdims=True) + RMS_EPS)
    return (y * g.astype(jnp.float32)).astype(x.dtype)


def modulate(h, shift, scale):
    return h * (1 + scale) + shift


def split_columns(p):
    bounds = [int(b) for b in np.cumsum(IN_SIZES)[:-1]]
    return jnp.split(p, bounds, axis=-1)


def centred_depthwise_conv(x, w):
    k = w.shape[0]
    return lax.conv_general_dilated(x, w[:, None, :].astype(x.dtype), window_strides=(1,), padding=[(k // 2, k // 2)], dimension_numbers=('NWC', 'WIO', 'NWC'), feature_group_count=x.shape[-1])


def l2_normalize(x):
    return x * lax.rsqrt(jnp.sum(x * x, axis=-1, keepdims=True) + RMS_EPS)


def delta_inputs(qkv, a, b, conv_w, a_log, dt_bias):
    B, L, _ = qkv.shape
    qkv = jax.nn.silu(centred_depthwise_conv(qkv, conv_w).astype(jnp.float32))
    q, k, v = jnp.split(qkv, 3, axis=-1)
    shp = (B, L, DN_HEADS, DN_HEAD_DIM)
    q = l2_normalize(q.reshape(shp)) * (DN_HEAD_DIM ** -0.5)
    k = l2_normalize(k.reshape(shp))
    v = v.reshape(shp)
    a = a.astype(jnp.float32).reshape(B, L, N_DIR, DN_HEADS)
    b = b.astype(jnp.float32).reshape(B, L, N_DIR, DN_HEADS)
    g = -jnp.exp(a_log.astype(jnp.float32)) * jax.nn.softplus(a + dt_bias.astype(jnp.float32))
    beta = jax.nn.sigmoid(b)
    return q, k, v, g, beta


def gated_delta_chunked(q, k, v, g, beta, s0):
    B, L, H, D = q.shape
    n = L // DN_CHUNK

    def chunks(t):
        t = t.reshape((B, n, DN_CHUNK, H) + t.shape[3:])
        return jnp.moveaxis(t, 3, 1)

    q, k, v, g, beta = (chunks(t) for t in (q, k, v, g, beta))
    gcum = jnp.cumsum(g, axis=-1)
    incl = jnp.tril(jnp.ones((DN_CHUNK, DN_CHUNK), dtype=bool))
    strict = jnp.tril(jnp.ones((DN_CHUNK, DN_CHUNK), dtype=bool), -1)
    diff = gcum[..., :, None] - gcum[..., None, :]
    decay = jnp.where(incl, jnp.exp(jnp.where(incl, diff, 0.0)), 0.0)
    kb = k * beta[..., None]
    a_mat = jnp.where(strict, jnp.einsum('bhncd,bhnsd->bhncs', kb, k) * decay, 0.0)
    rhs = jnp.concatenate([v * beta[..., None], kb * jnp.exp(gcum)[..., None]], axis=-1)
    sol = lax.linalg.triangular_solve(a_mat + jnp.eye(DN_CHUNK, dtype=jnp.float32), rhs, left_side=True, lower=True, unit_diagonal=True)
    u, w = sol[..., :D], sol[..., D:]
    qk = jnp.where(incl, jnp.einsum('bhncd,bhnsd->bhncs', q, k) * decay, 0.0)
    q_dec = q * jnp.exp(gcum)[..., None]
    k_dec = k * jnp.exp(gcum[..., -1:] - gcum)[..., None]
    g_tot = jnp.exp(gcum[..., -1])
    xs = tuple(jnp.moveaxis(t, 2, 0) for t in (qk, q_dec, k_dec, u, w, g_tot))

    def step(s, inp):
        qk_i, qd_i, kd_i, u_i, w_i, gt_i = inp
        v_new = u_i - jnp.einsum('bhck,bhkv->bhcv', w_i, s)
        o_i = jnp.einsum('bhck,bhkv->bhcv', qd_i, s) + jnp.einsum('bhcs,bhsv->bhcv', qk_i, v_new)
        s = s * gt_i[..., None, None] + jnp.einsum('bhck,bhcv->bhkv', kd_i, v_new)
        return s, o_i

    s_fin, o = lax.scan(step, s0, xs)
    o = jnp.transpose(o, (1, 0, 3, 2, 4)).reshape(B, L, H, D)
    return o, s_fin


def delta_bidirectional(ctx_in, lat_in):
    qc, kc, vc, gc, bc = ctx_in
    qx, kx, vx, gx, bx = lat_in
    flip = lambda t: jnp.flip(t, axis=1)
    s0 = jnp.zeros((qc.shape[0], DN_HEADS, DN_HEAD_DIM, DN_HEAD_DIM), jnp.float32)
    oc_f, s_f = gated_delta_chunked(qc, kc, vc, gc[:, :, 0], bc[:, :, 0], s0)
    oc_b, s_b = gated_delta_chunked(flip(qc), flip(kc), flip(vc), flip(gc[:, :, 1]), flip(bc[:, :, 1]), s0)
    ox_f, _ = gated_delta_chunked(qx, kx, vx, gx[:, :, 0], bx[:, :, 0], s_f)
    ox_b, _ = gated_delta_chunked(flip(qx), flip(kx), flip(vx), flip(gx[:, :, 1]), flip(bx[:, :, 1]), s_b)
    return oc_f + flip(oc_b), ox_f + flip(ox_b)


def gated_head_norm(o, z, g):
    B, L, H, D = o.shape
    on = o * lax.rsqrt(jnp.mean(o * o, axis=-1, keepdims=True) + RMS_EPS) * g.astype(jnp.float32)
    return (on * jax.nn.silu(z.astype(jnp.float32).reshape(B, L, H, D))).reshape(B, L, H * D)


def hyena_filter_spectrum(length, w1, b1, f1, w2, b2, f2, w3, b3):
    w1, b1, f1, w2, b2, f2, w3, b3 = (p.astype(jnp.float32) for p in (w1, b1, f1, w2, b2, f2, w3, b3))
    t = jnp.linspace(0.0, 1.0, length, dtype=jnp.float32)[:, None]
    omega = 2.0 * math.pi * jnp.arange(length, dtype=jnp.float32) / length
    bands = jnp.linspace(1e-4, HY_BANDS - 1, HY_BANDS, dtype=jnp.float32)
    ang = omega[:, None] * bands[None, :]
    feats = jnp.concatenate([t, jnp.cos(ang), -jnp.sin(ang)], axis=-1)
    h = jnp.sin(f1 * (feats @ w1 + b1))
    h = jnp.sin(f2 * (h @ w2 + b2))
    h = (h @ w3 + b3).reshape(length, N_DIR, HY_ORDER, HY_CH)
    max_decay = math.log(HY_TARGET) / HY_FAST_DECAY
    min_decay = math.log(HY_TARGET) / HY_SLOW_DECAY
    deltas = jnp.abs(jnp.linspace(min_decay, max_decay, HY_CH, dtype=jnp.float32))
    h = h * jnp.exp(-t * deltas[None, :])[:, None, None, :]
    fwd, bwd = h[:, 0], h[:, 1]
    kern = jnp.concatenate([fwd, jnp.zeros_like(fwd[:1]), bwd[:0:-1]], axis=0)
    kern = kern * lax.rsqrt(jnp.sum(kern * kern, axis=0, keepdims=True) + RMS_EPS)
    return jnp.fft.rfft(kern, axis=0)


def bidir_long_conv(u, spec, skip):
    L = u.shape[1]
    U = jnp.fft.rfft(u, n=2 * L, axis=1)
    y = jnp.fft.irfft(U * spec[None], n=2 * L, axis=1)[:, :L]
    return y + skip.astype(jnp.float32) * u


def hyena_stream(p, conv_w, spec, skip):
    z = centred_depthwise_conv(p, conv_w).astype(jnp.float32)
    v, x1, x2 = jnp.split(z, 3, axis=-1)
    y = x1 * bidir_long_conv(v, spec[:, 0], skip[0])
    y = x2 * bidir_long_conv(y, spec[:, 1], skip[1])
    return y.astype(p.dtype)


def rope_1d(x, pos):
    n = x.shape[-1] // 2
    inv = jnp.power(ROPE_THETA, -jnp.arange(n, dtype=jnp.float32) / n)
    ang = pos.astype(jnp.float32)[:, None] * inv[None, :]
    cos = jnp.cos(ang)[None, :, None, :]
    sin = jnp.sin(ang)[None, :, None, :]
    xf = x.astype(jnp.float32)
    a, b = xf[..., :n], xf[..., n:]
    return jnp.concatenate([a * cos - b * sin, b * cos + a * sin], axis=-1).astype(x.dtype)


def axial_rope(x, row, col):
    half = x.shape[-1] // 2
    return jnp.concatenate([rope_1d(x[..., :half], row), rope_1d(x[..., half:], col)], axis=-1)


def window_attention_latent(q, k, v, kc, vc, sink):
    B, L, _, D = q.shape
    nb = L // SW_BLOCK
    G = SW_Q_HEADS // SW_KV_HEADS
    qb = q.astype(jnp.float32).reshape(B, nb, SW_BLOCK, SW_KV_HEADS, G, D) * (D ** -0.5)

    def band(t):
        tb = t.astype(jnp.float32).reshape(B, nb, SW_BLOCK, SW_KV_HEADS, D)
        tp = jnp.pad(tb, ((0, 0), (1, 1), (0, 0), (0, 0), (0, 0)))
        return jnp.concatenate([tp[:, :-2], tp[:, 1:-1], tp[:, 2:]], axis=2)

    kw, vw = band(k), band(v)
    s_loc = jnp.einsum('bnqhgd,bnkhd->bnhgqk', qb, kw)
    blk = jnp.arange(nb)[:, None, None]
    qpos = blk * SW_BLOCK + jnp.arange(SW_BLOCK)[None, :, None]
    kpos = (blk - 1) * SW_BLOCK + jnp.arange(3 * SW_BLOCK)[None, None, :]
    valid = (jnp.abs(kpos - qpos) <= SW_WINDOW) & (kpos >= 0) & (kpos < L)
    s_loc = jnp.where(valid[None, :, None, None], s_loc, NEG_INF)
    s_ctx = jnp.einsum('bnqhgd,bkhd->bnhgqk', qb, kc.astype(jnp.float32))
    s_sink = jnp.broadcast_to(sink.astype(jnp.float32).reshape(1, 1, SW_KV_HEADS, G, 1, 1), s_loc.shape[:-1] + (1,))
    p = jax.nn.softmax(jnp.concatenate([s_loc, s_ctx, s_sink], axis=-1), axis=-1)
    nk = 3 * SW_BLOCK
    o = jnp.einsum('bnhgqk,bnkhd->bnqhgd', p[..., :nk], vw) + jnp.einsum('bnhgqk,bkhd->bnqhgd', p[..., nk:nk + kc.shape[1]], vc.astype(jnp.float32))
    return o.reshape(B, L, SW_Q_HEADS * D)


def context_attention(q, k, v, sink):
    B, Lc, _, D = q.shape
    G = SW_Q_HEADS // SW_KV_HEADS
    qg = q.astype(jnp.float32).reshape(B, Lc, SW_KV_HEADS, G, D) * (D ** -0.5)
    s = jnp.einsum('bqhgd,bkhd->bhgqk', qg, k.astype(jnp.float32))
    s_sink = jnp.broadcast_to(sink.astype(jnp.float32).reshape(1, SW_KV_HEADS, G, 1, 1), s.shape[:-1] + (1,))
    p = jax.nn.softmax(jnp.concatenate([s, s_sink], axis=-1), axis=-1)
    o = jnp.einsum('bhgqk,bkhd->bqhgd', p[..., :-1], v.astype(jnp.float32))
    return o.reshape(B, Lc, SW_Q_HEADS * D)


def swiglu(h, w_gate, w_up, w_down):
    return (jax.nn.silu(h @ w_gate) * (h @ w_up)) @ w_down


def moe_swiglu(h, router, w_gate, w_up, w_down):
    logits = (h @ router).astype(jnp.float32)
    top_logit, top_idx = lax.top_k(logits, TOP_K)
    top_w = jax.nn.softmax(top_logit, axis=-1)
    gate = jnp.sum(jax.nn.one_hot(top_idx, N_EXPERTS, dtype=jnp.float32) * top_w[..., None], axis=-2).astype(h.dtype)
    out = jnp.zeros_like(h)
    for e in range(N_EXPERTS):
        out = out + gate[..., e:e + 1] * swiglu(h, w_gate[e], w_up[e], w_down[e])
    return out


def setup_inputs(seed: int = 0) -> dict:
    key = jax.random.key(seed)
    ks = iter(jax.random.split(key, 48))
    nrm = lambda shape, s: jax.random.normal(next(ks), shape, jnp.float32) * s
    D = D_MODEL
    a_init = jax.random.uniform(next(ks), (DEPTH, N_DIR, DN_HEADS), jnp.float32, 1.0, 16.0)
    dt = jnp.exp(jax.random.uniform(next(ks), (DEPTH, N_DIR, DN_HEADS), jnp.float32, math.log(1e-3), math.log(1e-1)))
    hy_out = N_DIR * HY_ORDER * HY_CH
    return {
        "x": nrm((BATCH, SEQ, D), 1.0),
        "c": nrm((BATCH, D), 1.0),
        "ctx": nrm((BATCH, CTX_LEN, D), 1.0),
        "c_ctx": nrm((D,), 1.0),
        "w_mod": nrm((DEPTH, D, N_MOD * D), 0.5 * D ** -0.5),
        "b_mod": nrm((DEPTH, N_MOD * D), 0.02),
        "norm_mix_g": 1.0 + nrm((DEPTH, D), 0.1),
        "norm_ffn_g": 1.0 + nrm((DEPTH, D), 0.1),
        "w_in": nrm((DEPTH, D, N_IN), D ** -0.5),
        "dn_conv_w": nrm((DEPTH, DN_CONV, 3 * DN_WIDTH), DN_CONV ** -0.5),
        "dn_a_log": jnp.log(a_init),
        "dn_dt_bias": dt + jnp.log(-jnp.expm1(-dt)),
        "dn_norm_g": 1.0 + nrm((DEPTH, DN_HEAD_DIM), 0.1),
        "hy_conv_w": nrm((DEPTH, HY_CONV, (HY_ORDER + 1) * HY_CH), HY_CONV ** -0.5),
        "hy_w1": nrm((DEPTH, HY_EMB, HY_FFN), HY_EMB ** -0.5),
        "hy_b1": nrm((DEPTH, HY_FFN), 0.1),
        "hy_freq1": 1.0 + nrm((DEPTH, HY_FFN), 0.1),
        "hy_w2": nrm((DEPTH, HY_FFN, HY_FFN), HY_FFN ** -0.5),
        "hy_b2": nrm((DEPTH, HY_FFN), 0.1),
        "hy_freq2": 1.0 + nrm((DEPTH, HY_FFN), 0.1),
        "hy_w3": nrm((DEPTH, HY_FFN, hy_out), HY_FFN ** -0.5),
        "hy_b3": nrm((DEPTH, hy_out), 0.1),
        "hy_skip": nrm((DEPTH, HY_ORDER, HY_CH), 1.0),
        "sw_sink": nrm((DEPTH, SW_Q_HEADS), 0.5),
        "w_out": nrm((DEPTH, D, D), D ** -0.5),
        "ffn_w_gate": nrm((N_DENSE, D, DENSE_FF), D ** -0.5),
        "ffn_w_up": nrm((N_DENSE, D, DENSE_FF), D ** -0.5),
        "ffn_w_down": nrm((N_DENSE, DENSE_FF, D), DENSE_FF ** -0.5),
        "moe_router": nrm((N_MOE, D, N_EXPERTS), D ** -0.5),
        "moe_w_gate": nrm((N_MOE, N_EXPERTS, D, EXPERT_FF), D ** -0.5),
        "moe_w_up": nrm((N_MOE, N_EXPERTS, D, EXPERT_FF), D ** -0.5),
        "moe_w_down": nrm((N_MOE, N_EXPERTS, EXPERT_FF, D), EXPERT_FF ** -0.5),
        "final_norm_g": 1.0 + nrm((D,), 0.1),
    }


def reference(x, c, ctx, c_ctx, w_mod, b_mod, norm_mix_g, norm_ffn_g, w_in, dn_conv_w, dn_a_log, dn_dt_bias, dn_norm_g, hy_conv_w, hy_w1, hy_b1, hy_freq1, hy_w2, hy_b2, hy_freq2, hy_w3, hy_b3, hy_skip, sw_sink, w_out, ffn_w_gate, ffn_w_up, ffn_w_down, moe_router, moe_w_gate, moe_w_up, moe_w_down, final_norm_g):
    B, L, _ = x.shape
    Lc = ctx.shape[1]
    rows = L // GRID_W
    row = jnp.repeat(jnp.arange(rows), GRID_W)
    col = jnp.tile(jnp.arange(GRID_W), rows)
    silu_c = jax.nn.silu(c)
    silu_cc = jax.nn.silu(c_ctx)

    for layer in range(DEPTH):
        last = layer == DEPTH - 1
        mod_x = jnp.split((silu_c @ w_mod[layer] + b_mod[layer])[:, None, :], N_MOD, axis=-1)
        mod_c = jnp.split(silu_cc @ w_mod[layer] + b_mod[layer], N_MOD, axis=-1)

        hx = modulate(rms_norm(x, norm_mix_g[layer]), mod_x[0], mod_x[1])
        hc = modulate(rms_norm(ctx, norm_mix_g[layer]), mod_c[0], mod_c[1])
        px = split_columns(hx @ w_in[layer])
        pc = split_columns(hc @ w_in[layer])

        dn_c_in = delta_inputs(pc[0], pc[2], pc[3], dn_conv_w[layer], dn_a_log[layer], dn_dt_bias[layer])
        dn_x_in = delta_inputs(px[0], px[2], px[3], dn_conv_w[layer], dn_a_log[layer], dn_dt_bias[layer])
        dn_o_c, dn_o_x = delta_bidirectional(dn_c_in, dn_x_in)
        dn_x = gated_head_norm(dn_o_x, px[1], dn_norm_g[layer]).astype(x.dtype)

        hy_p = (hy_w1[layer], hy_b1[layer], hy_freq1[layer], hy_w2[layer], hy_b2[layer], hy_freq2[layer], hy_w3[layer], hy_b3[layer])
        hy_x = hyena_stream(px[4], hy_conv_w[layer], hyena_filter_spectrum(L, *hy_p), hy_skip[layer])

        sw_q_x = axial_rope(px[5].reshape(B, L, SW_Q_HEADS, SW_HEAD_DIM), row, col)
        sw_k_x = axial_rope(px[6].reshape(B, L, SW_KV_HEADS, SW_HEAD_DIM), row, col)
        sw_v_x = px[7].reshape(B, L, SW_KV_HEADS, SW_HEAD_DIM)
        sw_k_c = pc[6].reshape(B, Lc, SW_KV_HEADS, SW_HEAD_DIM)
        sw_v_c = pc[7].reshape(B, Lc, SW_KV_HEADS, SW_HEAD_DIM)
        sw_x = window_attention_latent(sw_q_x, sw_k_x, sw_v_x, sw_k_c, sw_v_c, sw_sink[layer]).astype(x.dtype)

        mix_x = jnp.concatenate([dn_x, hy_x, sw_x], axis=-1) @ w_out[layer]
        x = x + mod_x[2] * mix_x

        if not last:
            dn_c = gated_head_norm(dn_o_c, pc[1], dn_norm_g[layer]).astype(ctx.dtype)
            hy_c = hyena_stream(pc[4], hy_conv_w[layer], hyena_filter_spectrum(Lc, *hy_p), hy_skip[layer])
            sw_c = context_attention(pc[5].reshape(B, Lc, SW_Q_HEADS, SW_HEAD_DIM), sw_k_c, sw_v_c, sw_sink[layer]).astype(ctx.dtype)
            mix_c = jnp.concatenate([dn_c, hy_c, sw_c], axis=-1) @ w_out[layer]
            ctx = ctx + mod_c[2] * mix_c

        i = layer // 2
        hx2 = modulate(rms_norm(x, norm_ffn_g[layer]), mod_x[3], mod_x[4])
        if layer % 2 == 0:
            x = x + mod_x[5] * swiglu(hx2, ffn_w_gate[i], ffn_w_up[i], ffn_w_down[i])
        else:
            x = x + mod_x[5] * moe_swiglu(hx2, moe_router[i], moe_w_gate[i], moe_w_up[i], moe_w_down[i])
        if not last:
            hc2 = modulate(rms_norm(ctx, norm_ffn_g[layer]), mod_c[3], mod_c[4])
            if layer % 2 == 0:
                ctx = ctx + mod_c[5] * swiglu(hc2, ffn_w_gate[i], ffn_w_up[i], ffn_w_down[i])
            else:
                ctx = ctx + mod_c[5] * moe_swiglu(hc2, moe_router[i], moe_w_gate[i], moe_w_up[i], moe_w_down[i])

    return rms_norm(x, final_norm_g)
```

```python
import functools
import math

import jax
import jax.numpy as jnp
import numpy as np
from jax import lax
from jax.experimental import pallas as pl
from jax.experimental.pallas import tpu as pltpu

F32 = jnp.float32
BF16 = jnp.bfloat16

D_MODEL = 1024
DEPTH = 4
GRID_W = 64
N_MOD = 6
HEAD_DIM = 64
N_DIR = 2
RMS_EPS = 1e-6

DN_HEADS = 6
DN_WIDTH = DN_HEADS * HEAD_DIM
DN_CHUNK = 64
DN_LEVELS = 6

SW_Q_HEADS = 6
SW_KV_HEADS = 2
SW_GROUP = SW_Q_HEADS // SW_KV_HEADS
SW_WIDTH = SW_Q_HEADS * HEAD_DIM
SW_KV_WIDTH = SW_KV_HEADS * HEAD_DIM
SW_BLOCK = 128
ROPE_THETA = 10000.0
NEG_INF = -1e30

HY_CH = D_MODEL - DN_WIDTH - SW_WIDTH
HY_ORDER = 2
HY_BANDS = 16
HY_EMB = 1 + 2 * HY_BANDS
HY_FFN = 64
HY_FAST_DECAY = 0.3
HY_SLOW_DECAY = 1.5
HY_TARGET = 1e-2

N_EXPERTS = 8

LANES = 128
SUBLANES = 8
VMEM_LIMIT = 56 * 1024 * 1024

_IN_SIZES = (3 * DN_WIDTH, DN_WIDTH, N_DIR * DN_HEADS, N_DIR * DN_HEADS, 3 * HY_CH, SW_WIDTH, SW_KV_WIDTH, SW_KV_WIDTH)
_IN_OFF = tuple(int(v) for v in np.cumsum((0,) + _IN_SIZES))
_SEGS = (3 * DN_WIDTH, DN_WIDTH, 3 * HY_CH, SW_WIDTH, SW_KV_WIDTH, SW_KV_WIDTH, LANES)
_SEG_Q, _SEG_K = 3, 4


def _cparams(*sem):
    return pltpu.CompilerParams(dimension_semantics=sem, vmem_limit_bytes=VMEM_LIMIT)


def _dot(a, b):
    return jnp.dot(a, b, preferred_element_type=F32)


def _dot_nt(a, b):
    return lax.dot_general(a, b, (((1,), (1,)), ((), ())), preferred_element_type=F32)


def _split(x):
    hi = x.astype(BF16)
    lo = (x - hi.astype(F32)).astype(BF16)
    return hi, lo


def _dot3(a, b):
    ah, al = _split(a)
    bh, bl = _split(b)
    return _dot(ah, bh) + (_dot(al, bh) + _dot(ah, bl))


def _dot2_exact_rhs(a, b_bf16):
    ah, al = _split(a)
    return _dot(ah, b_bf16) + _dot(al, b_bf16)


def _sigmoid(x):
    return 1.0 / (1.0 + jnp.exp(-x))


def _silu(x):
    return x * _sigmoid(x)


def _softplus(x):
    return jnp.maximum(x, 0.0) + jnp.log(1.0 + jnp.exp(-jnp.abs(x)))


def _rms_mod(x, g, shift, scale):
    y = x * lax.rsqrt(jnp.mean(x * x, axis=-1, keepdims=True) + RMS_EPS)
    return (y * g) * (1.0 + scale) + shift


def _mod_kernel(c_ref, w_ref, b_ref, o_ref):
    o_ref[0] = _dot3(_silu(c_ref[...]), w_ref[0]) + b_ref[0]


def _modulation(c_rows, w_mod, b_mod):
    depth, d, n = w_mod.shape
    rows = c_rows.shape[0]
    tn = 1024
    return pl.pallas_call(
        _mod_kernel,
        grid=(depth, n // tn),
        in_specs=[
            pl.BlockSpec((rows, d), lambda l, j: (0, 0)),
            pl.BlockSpec((1, d, tn), lambda l, j: (l, 0, j)),
            pl.BlockSpec((1, 1, tn), lambda l, j: (l, 0, j)),
        ],
        out_specs=pl.BlockSpec((1, rows, tn), lambda l, j: (l, 0, j)),
        out_shape=jax.ShapeDtypeStruct((depth, rows, n), F32),
        compiler_params=_cparams("parallel", "parallel"),
        name="adaln_mod",
    )(c_rows, w_mod, b_mod.reshape(depth, 1, n))


def _rope_apply(x, cos, sin_signed):
    lane = lax.broadcasted_iota(jnp.int32, x.shape, 1)
    first = (lane & 31) < 16
    partner = jnp.where(first, pltpu.roll(x, LANES - 16, 1), pltpu.roll(x, 16, 1))
    return x * cos + partner * sin_signed


def _in_proj_kernel(x_ref, g_ref, sh_ref, sc_ref, w_ref, *rest, rope):
    if rope:
        cos_ref, sin_ref = rest[:2]
        outs = rest[2:]
    else:
        outs = rest
    h = _rms_mod(x_ref[0], g_ref[...], sh_ref[0], sc_ref[0]).astype(BF16)
    off = 0
    for idx, (o_ref, n) in enumerate(zip(outs, _SEGS)):
        r = _dot(h, w_ref[:, off:off + n])
        if idx == _SEG_Q:
            r = r * (HEAD_DIM ** -0.5)
        if rope and idx in (_SEG_Q, _SEG_K):
            cos, sin = cos_ref[...], sin_ref[...]
            r = jnp.concatenate(
                [_rope_apply(r[:, c:c + LANES], cos, sin) for c in range(0, n, LANES)], axis=1)
        o_ref[0] = r
        off += n


def _in_proj(x, g, shift, scale, w, rope_tabs):
    bn, t, d = x.shape
    tm = min(512, t)
    rope = rope_tabs is not None
    in_specs = [
        pl.BlockSpec((1, tm, d), lambda b, i: (b, i, 0)),
        pl.BlockSpec((1, d), lambda b, i: (0, 0)),
        pl.BlockSpec((1, 1, d), lambda b, i: (b, 0, 0)),
        pl.BlockSpec((1, 1, d), lambda b, i: (b, 0, 0)),
        pl.BlockSpec(w.shape, lambda b, i: (0, 0)),
    ]
    args = [x, g, shift, scale, w]
    if rope:
        in_specs += [pl.BlockSpec((tm, LANES), lambda b, i: (i, 0))] * 2
        args += list(rope_tabs)
    return pl.pallas_call(
        functools.partial(_in_proj_kernel, rope=rope),
        grid=(bn, t // tm),
        in_specs=in_specs,
        out_specs=[pl.BlockSpec((1, tm, n), lambda b, i: (b, i, 0)) for n in _SEGS],
        out_shape=[jax.ShapeDtypeStruct((bn, t, n), F32) for n in _SEGS],
        compiler_params=_cparams("parallel", "parallel"),
        name="in_proj_rope" if rope else "in_proj",
    )(*args)


def _relayout_w_in(w_in):
    o = _IN_OFF
    d = w_in.shape[0]
    pad = jnp.zeros((d, LANES - 2 * N_DIR * DN_HEADS), w_in.dtype)
    cols = [w_in[:, o[0]:o[2]], w_in[:, o[4]:o[8]], w_in[:, o[2]:o[4]], pad]
    return jnp.concatenate(cols, axis=1).astype(BF16)


def _rope_tables(length):
    n = HEAD_DIM // 4
    inv = jnp.power(ROPE_THETA, -jnp.arange(n, dtype=F32) / n)
    t = jnp.arange(length)
    row = (t // GRID_W).astype(F32)[:, None] * inv[None, :]
    col = (t % GRID_W).astype(F32)[:, None] * inv[None, :]
    cos = jnp.concatenate([jnp.cos(row), jnp.cos(row), jnp.cos(col), jnp.cos(col)], axis=1)
    sin = jnp.concatenate([-jnp.sin(row), jnp.sin(row), -jnp.sin(col), jnp.sin(col)], axis=1)
    return jnp.tile(cos, (1, 2)), jnp.tile(sin, (1, 2))


def _conv3_rows(x, prev8, next8, w, first, last):
    tm = x.shape[0]
    row = lax.broadcasted_iota(jnp.int32, x.shape, 0)
    before = jnp.where(first, 0.0, prev8[SUBLANES - 1:SUBLANES, :])
    after = jnp.where(last, 0.0, next8[0:1, :])
    xm = jnp.where(row == 0, before, pltpu.roll(x, 1, 0))
    xp = jnp.where(row == tm - 1, after, pltpu.roll(x, tm - 1, 0))
    return xm * w[0:1, :] + x * w[1:2, :] + xp * w[2:3, :]


def _halo_specs(tm, t, width):
    nb8 = t // SUBLANES
    step = tm // SUBLANES
    return [
        pl.BlockSpec((1, tm, width), lambda b, i: (b, i, 0)),
        pl.BlockSpec((1, SUBLANES, width), lambda b, i: (b, jnp.maximum(i * step - 1, 0), 0)),
        pl.BlockSpec((1, SUBLANES, width), lambda b, i: (b, jnp.minimum((i + 1) * step, nb8 - 1), 0)),
    ]


def _dn_prep_kernel(x_ref, xp_ref, xn_ref, w_ref, ab_ref, al_ref, dt_ref, gsum_ref,
                    q_ref, k_ref, v_ref, g_ref, beta_ref):
    i = pl.program_id(1)
    y = _silu(_conv3_rows(x_ref[0], xp_ref[0], xn_ref[0], w_ref[...], i == 0, i == pl.num_programs(1) - 1))
    q, k, v = y[:, :DN_WIDTH], y[:, DN_WIDTH:2 * DN_WIDTH], y[:, 2 * DN_WIDTH:]
    gs = gsum_ref[...]
    q = q * lax.rsqrt(_dot2_exact_rhs(q * q, gs) + RMS_EPS) * (HEAD_DIM ** -0.5)
    k = k * lax.rsqrt(_dot2_exact_rhs(k * k, gs) + RMS_EPS)
    for h in range(DN_HEADS):
        sl = slice(h * HEAD_DIM, (h + 1) * HEAD_DIM)
        q_ref[0, h] = q[:, sl]
        k_ref[0, h] = k[:, sl]
        v_ref[0, h] = v[:, sl]
    ab = ab_ref[0]
    lane = lax.broadcasted_iota(jnp.int32, ab.shape, 1)
    nh = N_DIR * DN_HEADS
    g_ref[0] = jnp.where(lane < nh, -jnp.exp(al_ref[...]) * _softplus(ab + dt_ref[...]), 0.0)
    beta_ref[0] = jnp.where(lane < nh, _sigmoid(pltpu.roll(ab, LANES - nh, 1)), 0.0)


def _dn_prep(qkv, ab, conv_w, a_log, dt_bias, gsum):
    bn, t, width = qkv.shape
    tm = min(512, t)
    pad = LANES - N_DIR * DN_HEADS
    al = jnp.pad(a_log.reshape(1, -1), ((0, 0), (0, pad)))
    dt = jnp.pad(dt_bias.reshape(1, -1), ((0, 0), (0, pad)))
    head = pl.BlockSpec((1, DN_HEADS, tm, HEAD_DIM), lambda b, i: (b, 0, i, 0))
    row = pl.BlockSpec((1, tm, LANES), lambda b, i: (b, i, 0))
    const = lambda shape: pl.BlockSpec(shape, lambda b, i: (0,) * len(shape))
    head_shape = jax.ShapeDtypeStruct((bn, DN_HEADS, t, HEAD_DIM), F32)
    row_shape = jax.ShapeDtypeStruct((bn, t, LANES), F32)
    return pl.pallas_call(
        _dn_prep_kernel,
        grid=(bn, t // tm),
        in_specs=_halo_specs(tm, t, width) + [const(conv_w.shape), row, const(al.shape), const(dt.shape),
                                              const(gsum.shape)],
        out_specs=[head, head, head, row, row],
        out_shape=[head_shape, head_shape, head_shape, row_shape, row_shape],
        compiler_params=_cparams("parallel", "parallel"),
        name="dn_prep",
    )(qkv, qkv, qkv, conv_w, ab, al, dt, gsum)


def _cumsum_rows(x, reverse):
    c = x.shape[0]
    row = lax.broadcasted_iota(jnp.int32, x.shape, 0)
    s = 1
    while s < c:
        if reverse:
            x = x + jnp.where(row < c - s, pltpu.roll(x, c - s, 0), 0.0)
        else:
            x = x + jnp.where(row >= s, pltpu.roll(x, s, 0), 0.0)
        s *= 2
    return x


def _dn_chunk(q, k, v, kt, kk, qk, cum_col, cum_row, tot, beta_col, s_prev, lower):
    c = q.shape[0]
    ii = lax.broadcasted_iota(jnp.int32, (c, c), 0)
    jj = lax.broadcasted_iota(jnp.int32, (c, c), 1)
    incl = (ii >= jj) if lower else (ii <= jj)
    strict = (ii > jj) if lower else (ii < jj)
    decay = jnp.where(incl, jnp.exp(jnp.where(incl, cum_col - cum_row, 0.0)), 0.0)
    e_cum = jnp.exp(cum_col)
    kb = k * beta_col
    x = jnp.concatenate([v * beta_col, kb * e_cum], axis=1)
    p = jnp.where(strict, -(kk * beta_col * decay), 0.0)
    for lvl in range(DN_LEVELS):
        if lvl + 1 < DN_LEVELS:
            r = _dot3(p, jnp.concatenate([p, x], axis=1))
            p = r[:, :c]
            x = x + r[:, c:]
        else:
            x = x + _dot3(p, x)
    qkm = qk * decay
    kdt = kt * jnp.exp(tot - cum_row)
    r = _dot(jnp.concatenate([qkm, kdt], axis=0).astype(BF16), x.astype(BF16))
    d = HEAD_DIM
    o_loc, q_eff = r[:c, :d], q * e_cum - r[:c, d:]
    n_mat = r[c:, :d]
    di = lax.broadcasted_iota(jnp.int32, (d, d), 0)
    dj = lax.broadcasted_iota(jnp.int32, (d, d), 1)
    m_mat = jnp.where(di == dj, jnp.exp(tot), 0.0) - r[c:, d:]
    r = _dot(jnp.concatenate([m_mat, q_eff], axis=0).astype(BF16), s_prev.astype(BF16))
    return r[:d] + n_mat, r[d:] + o_loc


def _dn_scan_kernel(qf_ref, kf_ref, vf_ref, gf_ref, bf_ref, qb_ref, kb_ref, vb_ref, gb_ref, bb_ref,
                    s0_ref, of_ref, ob_ref, sfin_ref, s_ref):
    t = pl.program_id(1)

    @pl.when(t == 0)
    def _():
        s_ref[...] = s0_ref[0]

    for d_idx, (q_ref, k_ref, v_ref, g_ref, b_ref, o_ref) in enumerate(
            ((qf_ref, kf_ref, vf_ref, gf_ref, bf_ref, of_ref), (qb_ref, kb_ref, vb_ref, gb_ref, bb_ref, ob_ref))):
        lower = d_idx == 0
        cum = _cumsum_rows(g_ref[0], reverse=not lower)
        cum_t = cum.T
        beta = b_ref[0]
        outs = []
        for h in range(DN_HEADS):
            col = d_idx * DN_HEADS + h
            q, k, v = q_ref[0, h], k_ref[0, h], v_ref[0, h]
            kb16 = k.astype(BF16)
            kk = _dot_nt(kb16, kb16)
            qk = _dot_nt(q.astype(BF16), kb16)
            cum_col = cum[:, col:col + 1]
            cum_row = cum_t[col:col + 1, :]
            last = DN_CHUNK - 1 if lower else 0
            tot = cum_row[:, last:last + 1]
            s_new, o = _dn_chunk(q, k, v, k.T, kk, qk, cum_col, cum_row, tot, beta[:, col:col + 1],
                                 s_ref[d_idx, h], lower)
            s_ref[d_idx, h] = s_new
            outs.append(o)
        o_ref[0] = jnp.concatenate(outs, axis=1)

    @pl.when(t == pl.num_programs(1) - 1)
    def _():
        sfin_ref[0] = s_ref[...]


def _dn_scan(q, k, v, g, beta, s0):
    bn, nh, t, d = q.shape
    n = t // DN_CHUNK
    head_f = pl.BlockSpec((1, nh, DN_CHUNK, d), lambda b, i: (b, 0, i, 0))
    head_b = pl.BlockSpec((1, nh, DN_CHUNK, d), lambda b, i: (b, 0, n - 1 - i, 0))
    row_f = pl.BlockSpec((1, DN_CHUNK, LANES), lambda b, i: (b, i, 0))
    row_b = pl.BlockSpec((1, DN_CHUNK, LANES), lambda b, i: (b, n - 1 - i, 0))
    state = pl.BlockSpec((1, N_DIR, nh, d, d), lambda b, i: (b, 0, 0, 0, 0))
    out_f = pl.BlockSpec((1, DN_CHUNK, nh * d), lambda b, i: (b, i, 0))
    out_b = pl.BlockSpec((1, DN_CHUNK, nh * d), lambda b, i: (b, n - 1 - i, 0))
    o_shape = jax.ShapeDtypeStruct((bn, t, nh * d), F32)
    return pl.pallas_call(
        _dn_scan_kernel,
        grid=(bn, n),
        in_specs=[head_f, head_f, head_f, row_f, row_f, head_b, head_b, head_b, row_b, row_b, state],
        out_specs=[out_f, out_b, state],
        out_shape=[o_shape, o_shape, jax.ShapeDtypeStruct(s0.shape, F32)],
        scratch_shapes=[pltpu.VMEM((N_DIR, nh, d, d), F32)],
        compiler_params=_cparams("parallel", "arbitrary"),
        name="dn_scan",
    )(q, k, v, g, beta, q, k, v, g, beta, s0)


def _hy_prep_kernel(x_ref, xp_ref, xn_ref, w_ref, v_ref, v16_ref, x1_ref, x2_ref):
    i = pl.program_id(1)
    y = _conv3_rows(x_ref[0], xp_ref[0], xn_ref[0], w_ref[...], i == 0, i == pl.num_programs(1) - 1)
    v = y[:, :HY_CH]
    v_ref[...] = v
    v16_ref[...] = v.astype(BF16)
    x1_ref[...] = y[:, HY_CH:2 * HY_CH]
    x2_ref[...] = y[:, 2 * HY_CH:]


def _hy_prep(z, conv_w):
    bn, t, width = z.shape
    tm = min(512, t)
    out = pl.BlockSpec((tm, HY_CH), lambda b, i: (i, b))
    f32 = jax.ShapeDtypeStruct((t, bn * HY_CH), F32)
    return pl.pallas_call(
        _hy_prep_kernel,
        grid=(bn, t // tm),
        in_specs=_halo_specs(tm, t, width) + [pl.BlockSpec(conv_w.shape, lambda b, i: (0, 0))],
        out_specs=[out, out, out, out],
        out_shape=[f32, jax.ShapeDtypeStruct((t, bn * HY_CH), BF16), f32, f32],
        compiler_params=_cparams("parallel", "parallel"),
        name="hy_prep",
    )(z, z, z, conv_w)


def _dft_matrices(length):
    n = 2 * length
    k = jnp.arange(length, dtype=jnp.int32)
    ang = ((k[:, None] * k[None, :]) % n).astype(F32) * (2.0 * math.pi / n)
    alt = (1 - 2 * (k % 2)).astype(F32)
    sin = jnp.where(k[:, None] == 0, alt[None, :], jnp.sin(ang))
    return jnp.cos(ang).astype(BF16), sin.astype(BF16)


def _hy_filter_kernel(f_ref, w1_ref, b1_ref, f1_ref, w2_ref, b2_ref, f2_ref, w3_ref, b3_ref, dl_ref,
                      p_ref, q_ref, ssq_ref, nyq_ref):
    i = pl.program_id(1)
    feats = f_ref[...]
    h = jnp.sin(f1_ref[0] * (_dot3(feats, w1_ref[0]) + b1_ref[0]))
    h = jnp.sin(f2_ref[0] * (_dot3(h, w2_ref[0]) + b2_ref[0]))
    h = _dot3(h, w3_ref[0]) + b3_ref[0]
    win = jnp.exp(-feats[:, 0:1] * dl_ref[...])
    half = HY_ORDER * HY_CH
    fwd = h[:, :half] * jnp.concatenate([win] * HY_ORDER, axis=1)
    bwd = h[:, half:] * jnp.concatenate([win] * HY_ORDER, axis=1)
    row = lax.broadcasted_iota(jnp.int32, bwd.shape, 0)
    bwd = jnp.where((row == 0) & (i == 0), 0.0, bwd)
    p = fwd + bwd
    p_ref[0] = p.astype(BF16)
    q_ref[0] = (bwd - fwd).astype(BF16)
    alt = (1 - 2 * (row & 1)).astype(F32)

    @pl.when(i == 0)
    def _():
        ssq_ref[0] = jnp.zeros_like(ssq_ref[0])
        nyq_ref[0] = jnp.zeros_like(nyq_ref[0])

    ssq_ref[0] += jnp.sum(fwd * fwd + bwd * bwd, axis=0, keepdims=True)
    nyq_ref[0] += jnp.sum(p * alt, axis=0, keepdims=True)


def _hy_spec_kernel(c_ref, s_ref, p_ref, q_ref, ssq_ref, nyq_ref, hre_ref, him_ref):
    i = pl.program_id(1)
    scale = lax.rsqrt(ssq_ref[0] + RMS_EPS)
    hre = _dot(c_ref[...], p_ref[0]) * scale
    him = _dot(s_ref[...], q_ref[0]) * scale
    row = lax.broadcasted_iota(jnp.int32, him.shape, 0)
    hre_ref[0] = hre
    him_ref[0] = jnp.where((row == 0) & (i == 0), nyq_ref[0] * scale, him)


def _hy_filters(length, cmat, smat, w1, b1, f1, w2, b2, f2, w3, b3):
    depth = w1.shape[0]
    t = jnp.linspace(0.0, 1.0, length, dtype=F32)[:, None]
    omega = 2.0 * math.pi * jnp.arange(length, dtype=F32) / length
    bands = jnp.linspace(1e-4, HY_BANDS - 1, HY_BANDS, dtype=F32)
    ang = omega[:, None] * bands[None, :]
    feats = jnp.concatenate([t, jnp.cos(ang), -jnp.sin(ang), jnp.zeros((length, LANES - HY_EMB), F32)], axis=-1)
    w1p = jnp.pad(w1, ((0, 0), (0, LANES - HY_EMB), (0, 0)))
    max_decay = math.log(HY_TARGET) / HY_FAST_DECAY
    min_decay = math.log(HY_TARGET) / HY_SLOW_DECAY
    deltas = jnp.abs(jnp.linspace(min_decay, max_decay, HY_CH, dtype=F32))[None, :]
    half = HY_ORDER * HY_CH
    tm = min(512, length)
    vec = lambda a: a.reshape(depth, 1, -1)
    lay = lambda shape: pl.BlockSpec((1,) + shape, lambda l, i: (l,) + (0,) * len(shape))
    p, q, ssq, nyq = pl.pallas_call(
        _hy_filter_kernel,
        grid=(depth, length // tm),
        in_specs=[pl.BlockSpec((tm, LANES), lambda l, i: (i, 0)),
                  lay((LANES, HY_FFN)), lay((1, HY_FFN)), lay((1, HY_FFN)),
                  lay((HY_FFN, HY_FFN)), lay((1, HY_FFN)), lay((1, HY_FFN)),
                  lay((HY_FFN, 2 * half)), lay((1, 2 * half)),
                  pl.BlockSpec((1, HY_CH), lambda l, i: (0, 0))],
        out_specs=[pl.BlockSpec((1, tm, half), lambda l, i: (l, i, 0)),
                   pl.BlockSpec((1, tm, half), lambda l, i: (l, i, 0)),
                   lay((1, half)), lay((1, half))],
        out_shape=[jax.ShapeDtypeStruct((depth, length, half), BF16),
                   jax.ShapeDtypeStruct((depth, length, half), BF16),
                   jax.ShapeDtypeStruct((depth, 1, half), F32),
                   jax.ShapeDtypeStruct((depth, 1, half), F32)],
        compiler_params=_cparams("parallel", "arbitrary"),
        name="hy_filter",
    )(feats, w1p, vec(b1), vec(f1), w2, vec(b2), vec(f2), w3, vec(b3), deltas)
    tk = min(512, length)
    return pl.pallas_call(
        _hy_spec_kernel,
        grid=(depth, length // tk),
        in_specs=[pl.BlockSpec((tk, length), lambda l, i: (i, 0)),
                  pl.BlockSpec((tk, length), lambda l, i: (i, 0)),
                  lay((length, half)), lay((length, half)), lay((1, half)), lay((1, half))],
        out_specs=[pl.BlockSpec((1, tk, half), lambda l, i: (l, i, 0))] * 2,
        out_shape=[jax.ShapeDtypeStruct((depth, length, half), F32)] * 2,
        compiler_params=_cparams("parallel", "parallel"),
        name="hy_spec",
    )(cmat, smat, p, q, ssq, nyq)


def _hy_fwd_kernel(c_ref, s_ref, u_ref, hre_ref, him_ref, yre_ref, yim_ref, *, n_points, n_batch):
    i = pl.program_id(0)
    ure = _dot(c_ref[...], u_ref[...])
    usn = _dot(s_ref[...], u_ref[...])
    hre = jnp.concatenate([hre_ref[...]] * n_batch, axis=1)
    him = jnp.concatenate([him_ref[...]] * n_batch, axis=1)
    row0 = (lax.broadcasted_iota(jnp.int32, ure.shape, 0) == 0) & (i == 0)
    yre = jnp.where(row0, ure * hre * (1.0 / n_points), (ure * hre + usn * him) * (2.0 / n_points))
    yim = jnp.where(row0, usn * him * (1.0 / n_points), (usn * hre - ure * him) * (2.0 / n_points))
    yre_ref[...] = yre.astype(BF16)
    yim_ref[...] = yim.astype(BF16)


def _hy_inv_kernel(c_ref, s_ref, yre_ref, yim_ref, u_ref, gate_ref, skip_ref, o_ref, *o16_ref, n_batch):
    i = pl.program_id(0)
    tm = c_ref.shape[0]
    conv_c = _dot(c_ref[...], yre_ref[...])
    conv_s = _dot(s_ref[...], yim_ref[...])
    trow = lax.broadcasted_iota(jnp.int32, conv_c.shape, 0) + i * tm
    alt = (1 - 2 * (trow & 1)).astype(F32)
    nyq = yim_ref[0:1, :].astype(F32)
    conv = conv_c + jnp.where(trow == 0, 0.0, conv_s) + alt * nyq
    skip = jnp.concatenate([skip_ref[...]] * n_batch, axis=1)
    y = gate_ref[...] * (conv + skip * u_ref[...])
    o_ref[...] = y
    if o16_ref:
        o16_ref[0][...] = y.astype(BF16)


def _hy_long_conv(cmat, smat, u16, u, gate, hre, him, skip, order, emit_bf16):
    length, cols = u.shape
    n_batch = cols // HY_CH
    tk = min(512, length)
    full = pl.BlockSpec((length, cols), lambda i: (0, 0))
    mat = pl.BlockSpec((tk, length), lambda i: (i, 0))
    tile = pl.BlockSpec((tk, cols), lambda i: (i, 0))
    spec = pl.BlockSpec((tk, HY_CH), lambda i: (i, order))
    yre, yim = pl.pallas_call(
        functools.partial(_hy_fwd_kernel, n_points=2 * length, n_batch=n_batch),
        grid=(length // tk,),
        in_specs=[mat, mat, full, spec, spec],
        out_specs=[tile, tile],
        out_shape=[jax.ShapeDtypeStruct((length, cols), BF16)] * 2,
        compiler_params=_cparams("parallel"),
        name="hy_fwd",
    )(cmat, smat, u16, hre, him)
    tm = min(256, length)
    mat = pl.BlockSpec((tm, length), lambda i: (i, 0))
    tile = pl.BlockSpec((tm, cols), lambda i: (i, 0))
    out_shape = [jax.ShapeDtypeStruct((length, cols), F32)]
    if emit_bf16:
        out_shape.append(jax.ShapeDtypeStruct((length, cols), BF16))
    return pl.pallas_call(
        functools.partial(_hy_inv_kernel, n_batch=n_batch),
        grid=(length // tm,),
        in_specs=[mat, mat, full, full, tile, tile, pl.BlockSpec((1, HY_CH), lambda i: (0, 0))],
        out_specs=[tile] * len(out_shape),
        out_shape=out_shape,
        compiler_params=_cparams("parallel"),
        name="hy_inv",
    )(cmat, smat, yre, yim, u, gate, skip[order][None, :])


def _hyena(z, conv_w, cmat, smat, hre, him, skip):
    v, v16, x1, x2 = _hy_prep(z, conv_w)
    y1, y16 = _hy_long_conv(cmat, smat, v16, v, x1, hre, him, skip, 0, True)
    (y,) = _hy_long_conv(cmat, smat, y16, y1, x2, hre, him, skip, 1, False)
    return y


def _attn_kernel(sink_ref, q_ref, *rest, local, n_blocks):
    if local:
        kp_ref, kc_ref, kn_ref, vp_ref, vc_ref, vn_ref, kx_ref, vx_ref, o_ref = rest
        k_all = jnp.concatenate([kp_ref[0], kc_ref[0], kn_ref[0], kx_ref[0]], axis=0)
        v_all = jnp.concatenate([vp_ref[0], vc_ref[0], vn_ref[0], vx_ref[0]], axis=0)
    else:
        kx_ref, vx_ref, o_ref = rest
        k_all, v_all = kx_ref[0], vx_ref[0]
    blk = pl.program_id(1)
    q = q_ref[0]
    tq = q.shape[0]
    nk = k_all.shape[0]
    k16, v16 = k_all.astype(BF16), v_all.astype(BF16)
    lane = lax.broadcasted_iota(jnp.int32, (tq, LANES), 1)
    rows = SW_GROUP * tq
    r_idx = lax.broadcasted_iota(jnp.int32, (rows, nk), 0)
    if local:
        r = r_idx & (tq - 1)
        c = lax.broadcasted_iota(jnp.int32, (rows, nk), 1)
        rel = c - r - SW_BLOCK
        valid = (rel >= -SW_BLOCK) & (rel <= SW_BLOCK)
        valid = valid & ((c >= SW_BLOCK) | (blk > 0)) & ((c < 2 * SW_BLOCK) | (blk < n_blocks - 1))
        valid = valid | (c >= 3 * SW_BLOCK)
    out_heads = [None] * SW_Q_HEADS
    for j in range(SW_KV_HEADS):
        keep = (lane >= j * HEAD_DIM) & (lane < (j + 1) * HEAD_DIM)
        parts = []
        for g in range(SW_GROUP):
            hq = j * SW_GROUP + g
            chunk = q[:, (hq // 2) * LANES:(hq // 2 + 1) * LANES]
            if hq % 2 != j:
                chunk = pltpu.roll(chunk, HEAD_DIM, 1)
            parts.append(jnp.where(keep, chunk, 0.0))
        qz = jnp.concatenate(parts, axis=0).astype(BF16)
        s = _dot_nt(qz, k16)
        if local:
            s = jnp.where(valid, s, NEG_INF)
        sink = jnp.where(r_idx[:, 0:1] < tq, sink_ref[j * SW_GROUP],
                         jnp.where(r_idx[:, 0:1] < 2 * tq, sink_ref[j * SW_GROUP + 1], sink_ref[j * SW_GROUP + 2]))
        m = jnp.maximum(jnp.max(s, axis=-1, keepdims=True), sink)
        p = jnp.exp(s - m)
        denom = jnp.sum(p, axis=-1, keepdims=True) + jnp.exp(sink - m)
        o = _dot(p.astype(BF16), v16) / denom
        for g in range(SW_GROUP):
            hq = j * SW_GROUP + g
            og = o[g * tq:(g + 1) * tq]
            if hq % 2 != j:
                og = pltpu.roll(og, HEAD_DIM, 1)
            out_heads[hq] = og
    first_half = lane < HEAD_DIM
    o_ref[0] = jnp.concatenate(
        [jnp.where(first_half, out_heads[2 * c], out_heads[2 * c + 1]) for c in range(SW_Q_HEADS // 2)], axis=1)


def _attention(q, k, v, kx, vx, sink, local):
    bn, t, _ = q.shape
    nb = t // SW_BLOCK
    tx = kx.shape[1]
    qspec = pl.BlockSpec((1, SW_BLOCK, SW_WIDTH), lambda b, i: (b, i, 0))
    xspec = pl.BlockSpec((1, tx, SW_KV_WIDTH), lambda b, i: (b, 0, 0))
    in_specs = [pl.BlockSpec(memory_space=pltpu.SMEM), qspec]
    args = [sink, q]
    if local:
        prev = pl.BlockSpec((1, SW_BLOCK, SW_KV_WIDTH), lambda b, i: (b, jnp.maximum(i - 1, 0), 0))
        cur = pl.BlockSpec((1, SW_BLOCK, SW_KV_WIDTH), lambda b, i: (b, i, 0))
        nxt = pl.BlockSpec((1, SW_BLOCK, SW_KV_WIDTH), lambda b, i: (b, jnp.minimum(i + 1, nb - 1), 0))
        in_specs += [prev, cur, nxt, prev, cur, nxt]
        args += [k, k, k, v, v, v]
    in_specs += [xspec, xspec]
    args += [kx, vx]
    return pl.pallas_call(
        functools.partial(_attn_kernel, local=local, n_blocks=nb),
        grid=(bn, nb),
        in_specs=in_specs,
        out_specs=qspec,
        out_shape=jax.ShapeDtypeStruct((bn, t, SW_WIDTH), F32),
        compiler_params=_cparams("parallel", "parallel"),
        name="attn_local" if local else "attn_ctx",
    )(*args)


def _out_proj_kernel(of_ref, ob_ref, z_ref, gn_ref, gsum_ref, hy_ref, sw_ref, x_ref, gate_ref, w_ref, o_ref):
    o = of_ref[0] + ob_ref[0]
    mean = _dot2_exact_rhs(o * o, gsum_ref[...]) * (1.0 / HEAD_DIM)
    dn = o * lax.rsqrt(mean + RMS_EPS) * gn_ref[...] * _silu(z_ref[0])
    a, b = DN_WIDTH, DN_WIDTH + HY_CH
    mix = _dot(dn.astype(BF16), w_ref[0:a, :])
    mix += _dot(hy_ref[...].astype(BF16), w_ref[a:b, :])
    mix += _dot(sw_ref[0].astype(BF16), w_ref[b:, :])
    o_ref[0] = x_ref[0] + gate_ref[0] * mix


def _out_proj(o_f, o_b, z, gn, gsum, hy, sw, x, gate, w):
    bn, t, d = x.shape
    tm = min(512, t)
    seq = lambda n: pl.BlockSpec((1, tm, n), lambda b, i: (b, i, 0))
    const = lambda shape: pl.BlockSpec(shape, lambda b, i: (0,) * len(shape))
    return pl.pallas_call(
        _out_proj_kernel,
        grid=(bn, t // tm),
        in_specs=[seq(DN_WIDTH), seq(DN_WIDTH), seq(DN_WIDTH), const(gn.shape), const(gsum.shape),
                  pl.BlockSpec((tm, HY_CH), lambda b, i: (i, b)), seq(SW_WIDTH), seq(d),
                  pl.BlockSpec((1, 1, d), lambda b, i: (b, 0, 0)), const(w.shape)],
        out_specs=seq(d),
        out_shape=jax.ShapeDtypeStruct(x.shape, F32),
        compiler_params=_cparams("parallel", "parallel"),
        name="out_proj",
    )(o_f, o_b, z, gn, gsum, hy, sw, x, gate, w)


def _ffn_kernel(x_ref, g_ref, sh_ref, sc_ref, gate_ref, wg_ref, wu_ref, wd_ref, o_ref, h_ref, acc_ref):
    j = pl.program_id(2)

    @pl.when(j == 0)
    def _():
        h_ref[...] = _rms_mod(x_ref[0], g_ref[...], sh_ref[0], sc_ref[0]).astype(BF16)
        acc_ref[...] = jnp.zeros_like(acc_ref)

    h = h_ref[...]
    act = _silu(_dot(h, wg_ref[...])) * _dot(h, wu_ref[...])
    acc_ref[...] += _dot(act.astype(BF16), wd_ref[...])

    @pl.when(j == pl.num_programs(2) - 1)
    def _():
        o_ref[0] = x_ref[0] + gate_ref[0] * acc_ref[...]


def _ffn(x, g, shift, scale, gate, wg, wu, wd):
    bn, t, d = x.shape
    ff = wg.shape[1]
    tm = min(512, t)
    tf = ff // 2
    seq = pl.BlockSpec((1, tm, d), lambda b, i, j: (b, i, 0))
    mod = pl.BlockSpec((1, 1, d), lambda b, i, j: (b, 0, 0))
    return pl.pallas_call(
        _ffn_kernel,
        grid=(bn, t // tm, ff // tf),
        in_specs=[seq, pl.BlockSpec((1, d), lambda b, i, j: (0, 0)), mod, mod, mod,
                  pl.BlockSpec((d, tf), lambda b, i, j: (0, j)),
                  pl.BlockSpec((d, tf), lambda b, i, j: (0, j)),
                  pl.BlockSpec((tf, d), lambda b, i, j: (j, 0))],
        out_specs=seq,
        out_shape=jax.ShapeDtypeStruct(x.shape, F32),
        scratch_shapes=[pltpu.VMEM((tm, d), BF16), pltpu.VMEM((tm, d), F32)],
        compiler_params=_cparams("parallel", "parallel", "arbitrary"),
        name="ffn_dense",
    )(x, g, shift, scale, gate, wg, wu, wd)


def _top2_gate(logits):
    lane = lax.broadcasted_iota(jnp.int32, logits.shape, 1)
    lg = jnp.where(lane < N_EXPERTS, logits, NEG_INF)
    m1 = jnp.max(lg, axis=-1, keepdims=True)
    i1 = jnp.min(jnp.where(lg == m1, lane, LANES), axis=-1, keepdims=True)
    lg2 = jnp.where(lane == i1, NEG_INF, lg)
    m2 = jnp.max(lg2, axis=-1, keepdims=True)
    i2 = jnp.min(jnp.where(lg2 == m2, lane, LANES), axis=-1, keepdims=True)
    e2 = jnp.exp(m2 - m1)
    w1 = 1.0 / (1.0 + e2)
    return jnp.where(lane == i1, w1, jnp.where(lane == i2, e2 * w1, 0.0))


def _moe_kernel(x_ref, g_ref, sh_ref, sc_ref, gate_ref, r_ref, wg_ref, wu_ref, wd_ref, o_ref,
                h_ref, acc_ref, rw_ref):
    e = pl.program_id(2)
    j = pl.program_id(3)

    @pl.when((e == 0) & (j == 0))
    def _():
        h = _rms_mod(x_ref[0], g_ref[...], sh_ref[0], sc_ref[0])
        h_ref[...] = h.astype(BF16)
        rw_ref[...] = _top2_gate(_dot3(h, r_ref[...]))
        acc_ref[...] = jnp.zeros_like(acc_ref)

    h = h_ref[...]
    lane = lax.broadcasted_iota(jnp.int32, rw_ref.shape, 1)
    w_e = jnp.sum(jnp.where(lane == e, rw_ref[...], 0.0), axis=-1, keepdims=True)
    act = _silu(_dot(h, wg_ref[0])) * _dot(h, wu_ref[0]) * w_e
    acc_ref[...] += _dot(act.astype(BF16), wd_ref[0])

    @pl.when((e == pl.num_programs(2) - 1) & (j == pl.num_programs(3) - 1))
    def _():
        o_ref[0] = x_ref[0] + gate_ref[0] * acc_ref[...]


def _moe(x, g, shift, scale, gate, router, wg, wu, wd):
    bn, t, d = x.shape
    ne, _, ff = wg.shape
    tm = min(512, t)
    tf = ff // 2
    seq = pl.BlockSpec((1, tm, d), lambda b, i, e, j: (b, i, 0))
    mod = pl.BlockSpec((1, 1, d), lambda b, i, e, j: (b, 0, 0))
    return pl.pallas_call(
        _moe_kernel,
        grid=(bn, t // tm, ne, ff // tf),
        in_specs=[seq, pl.BlockSpec((1, d), lambda b, i, e, j: (0, 0)), mod, mod, mod,
                  pl.BlockSpec(router.shape, lambda b, i, e, j: (0, 0)),
                  pl.BlockSpec((1, d, tf), lambda b, i, e, j: (e, 0, j)),
                  pl.BlockSpec((1, d, tf), lambda b, i, e, j: (e, 0, j)),
                  pl.BlockSpec((1, tf, d), lambda b, i, e, j: (e, j, 0))],
        out_specs=seq,
        out_shape=jax.ShapeDtypeStruct(x.shape, F32),
        scratch_shapes=[pltpu.VMEM((tm, d), BF16), pltpu.VMEM((tm, d), F32), pltpu.VMEM((tm, LANES), F32)],
        compiler_params=_cparams("parallel", "parallel", "arbitrary", "arbitrary"),
        name="moe_dense",
    )(x, g, shift, scale, gate, router, wg, wu, wd)


def _final_norm_kernel(x_ref, g_ref, o_ref):
    x = x_ref[0]
    o_ref[0] = x * lax.rsqrt(jnp.mean(x * x, axis=-1, keepdims=True) + RMS_EPS) * g_ref[...]


def _final_norm(x, g):
    bn, t, d = x.shape
    tm = min(1024, t)
    seq = pl.BlockSpec((1, tm, d), lambda b, i: (b, i, 0))
    return pl.pallas_call(
        _final_norm_kernel,
        grid=(bn, t // tm),
        in_specs=[seq, pl.BlockSpec((1, d), lambda b, i: (0, 0))],
        out_specs=seq,
        out_shape=jax.ShapeDtypeStruct(x.shape, F32),
        compiler_params=_cparams("parallel", "parallel"),
        name="final_norm",
    )(x, g)


def _head_sum_matrix():
    idx = np.arange(DN_WIDTH) // HEAD_DIM
    return jnp.asarray(idx[:, None] == idx[None, :], dtype=BF16)


def kernel(x, c, ctx, c_ctx, w_mod, b_mod, norm_mix_g, norm_ffn_g, w_in, dn_conv_w, dn_a_log, dn_dt_bias, dn_norm_g, hy_conv_w, hy_w1, hy_b1, hy_freq1, hy_w2, hy_b2, hy_freq2, hy_w3, hy_b3, hy_skip, sw_sink, w_out, ffn_w_gate, ffn_w_up, ffn_w_down, moe_router, moe_w_gate, moe_w_up, moe_w_down, final_norm_g):
    bn, seq_len, d = x.shape
    ctx_len = ctx.shape[1]
    depth = w_mod.shape[0]

    c_rows = jnp.concatenate([c, c_ctx[None, :], jnp.zeros((SUBLANES - bn - 1, d), F32)], axis=0)
    mods = _modulation(c_rows, w_mod, b_mod).reshape(depth, SUBLANES, N_MOD, d)

    rope_tabs = _rope_tables(seq_len)
    gsum = _head_sum_matrix()
    cm_x, sm_x = _dft_matrices(seq_len)
    cm_c, sm_c = _dft_matrices(ctx_len)
    hy_params = (hy_w1, hy_b1, hy_freq1, hy_w2, hy_b2, hy_freq2, hy_w3, hy_b3)
    hre_x, him_x = _hy_filters(seq_len, cm_x, sm_x, *hy_params)
    hre_c, him_c = _hy_filters(ctx_len, cm_c, sm_c, *hy_params)
    router = jnp.pad(moe_router, ((0, 0), (0, 0), (0, LANES - N_EXPERTS)))
    zero_state = jnp.zeros((bn, N_DIR, DN_HEADS, HEAD_DIM, HEAD_DIM), F32)

    for layer in range(depth):
        last = layer == depth - 1
        mod_x = [mods[layer, :bn, m][:, None, :] for m in range(N_MOD)]
        mod_c = [jnp.broadcast_to(mods[layer, bn, m][None, None, :], (bn, 1, d)) for m in range(N_MOD)]
        g_mix = norm_mix_g[layer][None, :]
        g_ffn = norm_ffn_g[layer][None, :]
        w_in_l = _relayout_w_in(w_in[layer])
        w_out_l = w_out[layer].astype(BF16)
        gn = jnp.tile(dn_norm_g[layer], DN_HEADS)[None, :]

        qkv_x, z_x, hyp_x, swq_x, swk_x, swv_x, ab_x = _in_proj(x, g_mix, mod_x[0], mod_x[1], w_in_l, rope_tabs)
        qkv_c, z_c, hyp_c, swq_c, swk_c, swv_c, ab_c = _in_proj(ctx, g_mix, mod_c[0], mod_c[1], w_in_l, None)

        dn_in_c = _dn_prep(qkv_c, ab_c, dn_conv_w[layer], dn_a_log[layer], dn_dt_bias[layer], gsum)
        dn_in_x = _dn_prep(qkv_x, ab_x, dn_conv_w[layer], dn_a_log[layer], dn_dt_bias[layer], gsum)
        of_c, ob_c, state_c = _dn_scan(*dn_in_c, zero_state)
        of_x, ob_x, _ = _dn_scan(*dn_in_x, state_c)

        hy_x = _hyena(hyp_x, hy_conv_w[layer], cm_x, sm_x, hre_x[layer], him_x[layer], hy_skip[layer])
        sw_x = _attention(swq_x, swk_x, swv_x, swk_c, swv_c, sw_sink[layer], True)
        x = _out_proj(of_x, ob_x, z_x, gn, gsum, hy_x, sw_x, x, mod_x[2], w_out_l)

        if not last:
            hy_c = _hyena(hyp_c, hy_conv_w[layer], cm_c, sm_c, hre_c[layer], him_c[layer], hy_skip[layer])
            sw_c = _attention(swq_c, None, None, swk_c, swv_c, sw_sink[layer], False)
            ctx = _out_proj(of_c, ob_c, z_c, gn, gsum, hy_c, sw_c, ctx, mod_c[2], w_out_l)

        i = layer // 2
        streams = [(x, mod_x)] if last else [(x, mod_x), (ctx, mod_c)]
        outs = []
        for s, mod in streams:
            if layer % 2 == 0:
                outs.append(_ffn(s, g_ffn, mod[3], mod[4], mod[5], ffn_w_gate[i].astype(BF16),
                                 ffn_w_up[i].astype(BF16), ffn_w_down[i].astype(BF16)))
            else:
                outs.append(_moe(s, g_ffn, mod[3], mod[4], mod[5], router[i], moe_w_gate[i].astype(BF16),
                                 moe_w_up[i].astype(BF16), moe_w_down[i].astype(BF16)))
        x = outs[0]
        if not last:
            ctx = outs[1]

    return _final_norm(x, final_norm_g[None, :])
```

```python
import functools
import math

import jax
import jax.numpy as jnp
import numpy as np
from jax import lax
from jax.experimental import pallas as pl
from jax.experimental.pallas import tpu as pltpu

F32 = jnp.float32
BF16 = jnp.bfloat16

D_MODEL = 1024
DEPTH = 4
GRID_W = 64
N_MOD = 6
HEAD_DIM = 64
N_DIR = 2
RMS_EPS = 1e-6

DN_HEADS = 6
DN_WIDTH = DN_HEADS * HEAD_DIM
DN_CHUNK = 64
DN_LEVELS = 6

SW_Q_HEADS = 6
SW_KV_HEADS = 2
SW_GROUP = SW_Q_HEADS // SW_KV_HEADS
SW_WIDTH = SW_Q_HEADS * HEAD_DIM
SW_KV_WIDTH = SW_KV_HEADS * HEAD_DIM
SW_BLOCK = 128
ROPE_THETA = 10000.0
NEG_INF = -1e30

HY_CH = D_MODEL - DN_WIDTH - SW_WIDTH
HY_ORDER = 2
HY_BANDS = 16
HY_EMB = 1 + 2 * HY_BANDS
HY_FFN = 64
HY_FAST_DECAY = 0.3
HY_SLOW_DECAY = 1.5
HY_TARGET = 1e-2

N_EXPERTS = 8

LANES = 128
SUBLANES = 8
VMEM_LIMIT = 56 * 1024 * 1024

_IN_SIZES = (3 * DN_WIDTH, DN_WIDTH, N_DIR * DN_HEADS, N_DIR * DN_HEADS, 3 * HY_CH, SW_WIDTH, SW_KV_WIDTH, SW_KV_WIDTH)
_IN_OFF = tuple(int(v) for v in np.cumsum((0,) + _IN_SIZES))
_SEGS = (3 * DN_WIDTH, DN_WIDTH, 3 * HY_CH, SW_WIDTH, SW_KV_WIDTH, SW_KV_WIDTH, LANES)
_SEG_Q, _SEG_K = 3, 4


def _cparams(*sem):
    return pltpu.CompilerParams(dimension_semantics=sem, vmem_limit_bytes=VMEM_LIMIT)


def _dot(a, b):
    return jnp.dot(a, b, preferred_element_type=F32)


def _dot_nt(a, b):
    return lax.dot_general(a, b, (((1,), (1,)), ((), ())), preferred_element_type=F32)


def _split(x):
    hi = x.astype(BF16)
    lo = (x - hi.astype(F32)).astype(BF16)
    return hi, lo


def _dot3(a, b):
    ah, al = _split(a)
    bh, bl = _split(b)
    return _dot(ah, bh) + (_dot(al, bh) + _dot(ah, bl))


def _dot2_exact_rhs(a, b_bf16):
    ah, al = _split(a)
    return _dot(ah, b_bf16) + _dot(al, b_bf16)


def _sigmoid(x):
    return 1.0 / (1.0 + jnp.exp(-x))


def _silu(x):
    return x * _sigmoid(x)


def _softplus(x):
    return jnp.maximum(x, 0.0) + jnp.log(1.0 + jnp.exp(-jnp.abs(x)))


def _rms_mod(x, g, shift, scale):
    y = x * lax.rsqrt(jnp.mean(x * x, axis=-1, keepdims=True) + RMS_EPS)
    return (y * g) * (1.0 + scale) + shift


def _mod_kernel(c_ref, w_ref, b_ref, o_ref):
    o_ref[0] = _dot3(_silu(c_ref[...]), w_ref[0]) + b_ref[0]


def _modulation(c_rows, w_mod, b_mod):
    depth, d, n = w_mod.shape
    rows = c_rows.shape[0]
    tn = 1024
    return pl.pallas_call(
        _mod_kernel,
        grid=(depth, n // tn),
        in_specs=[
            pl.BlockSpec((rows, d), lambda l, j: (0, 0)),
            pl.BlockSpec((1, d, tn), lambda l, j: (l, 0, j)),
            pl.BlockSpec((1, 1, tn), lambda l, j: (l, 0, j)),
        ],
        out_specs=pl.BlockSpec((1, rows, tn), lambda l, j: (l, 0, j)),
        out_shape=jax.ShapeDtypeStruct((depth, rows, n), F32),
        compiler_params=_cparams("parallel", "parallel"),
        name="adaln_mod",
    )(c_rows, w_mod, b_mod.reshape(depth, 1, n))


def _rope_apply(x, cos, sin_signed):
    lane = lax.broadcasted_iota(jnp.int32, x.shape, 1)
    first = (lane & 31) < 16
    partner = jnp.where(first, pltpu.roll(x, LANES - 16, 1), pltpu.roll(x, 16, 1))
    return x * cos + partner * sin_signed


def _in_proj_kernel(x_ref, g_ref, sh_ref, sc_ref, w_ref, *rest, rope):
    if rope:
        cos_ref, sin_ref = rest[:2]
        outs = rest[2:]
    else:
        outs = rest
    h = _rms_mod(x_ref[0], g_ref[...], sh_ref[0], sc_ref[0]).astype(BF16)
    off = 0
    for idx, (o_ref, n) in enumerate(zip(outs, _SEGS)):
        r = _dot(h, w_ref[:, off:off + n])
        if idx == _SEG_Q:
            r = r * (HEAD_DIM ** -0.5)
        if rope and idx in (_SEG_Q, _SEG_K):
            cos, sin = cos_ref[...], sin_ref[...]
            r = jnp.concatenate(
                [_rope_apply(r[:, c:c + LANES], cos, sin) for c in range(0, n, LANES)], axis=1)
        o_ref[0] = r
        off += n


def _in_proj(x, g, shift, scale, w, rope_tabs):
    bn, t, d = x.shape
    tm = min(512, t)
    rope = rope_tabs is not None
    in_specs = [
        pl.BlockSpec((1, tm, d), lambda b, i: (b, i, 0)),
        pl.BlockSpec((1, d), lambda b, i: (0, 0)),
        pl.BlockSpec((1, 1, d), lambda b, i: (b, 0, 0)),
        pl.BlockSpec((1, 1, d), lambda b, i: (b, 0, 0)),
        pl.BlockSpec(w.shape, lambda b, i: (0, 0)),
    ]
    args = [x, g, shift, scale, w]
    if rope:
        in_specs += [pl.BlockSpec((tm, LANES), lambda b, i: (i, 0))] * 2
        args += list(rope_tabs)
    return pl.pallas_call(
        functools.partial(_in_proj_kernel, rope=rope),
        grid=(bn, t // tm),
        in_specs=in_specs,
        out_specs=[pl.BlockSpec((1, tm, n), lambda b, i: (b, i, 0)) for n in _SEGS],
        out_shape=[jax.ShapeDtypeStruct((bn, t, n), F32) for n in _SEGS],
        compiler_params=_cparams("parallel", "parallel"),
        name="in_proj_rope" if rope else "in_proj",
    )(*args)


def _relayout_w_in(w_in):
    o = _IN_OFF
    d = w_in.shape[0]
    pad = jnp.zeros((d, LANES - 2 * N_DIR * DN_HEADS), w_in.dtype)
    cols = [w_in[:, o[0]:o[2]], w_in[:, o[4]:o[8]], w_in[:, o[2]:o[4]], pad]
    return jnp.concatenate(cols, axis=1).astype(BF16)


def _rope_tables(length):
    n = HEAD_DIM // 4
    inv = jnp.power(ROPE_THETA, -jnp.arange(n, dtype=F32) / n)
    t = jnp.arange(length)
    row = (t // GRID_W).astype(F32)[:, None] * inv[None, :]
    col = (t % GRID_W).astype(F32)[:, None] * inv[None, :]
    cos = jnp.concatenate([jnp.cos(row), jnp.cos(row), jnp.cos(col), jnp.cos(col)], axis=1)
    sin = jnp.concatenate([-jnp.sin(row), jnp.sin(row), -jnp.sin(col), jnp.sin(col)], axis=1)
    return jnp.tile(cos, (1, 2)), jnp.tile(sin, (1, 2))


def _conv3_rows(x, prev8, next8, w, first, last):
    tm = x.shape[0]
    row = lax.broadcasted_iota(jnp.int32, x.shape, 0)
    before = jnp.where(first, 0.0, prev8[SUBLANES - 1:SUBLANES, :])
    after = jnp.where(last, 0.0, next8[0:1, :])
    xm = jnp.where(row == 0, before, pltpu.roll(x, 1, 0))
    xp = jnp.where(row == tm - 1, after, pltpu.roll(x, tm - 1, 0))
    return xm * w[0:1, :] + x * w[1:2, :] + xp * w[2:3, :]


def _halo_specs(tm, t, width):
    nb8 = t // SUBLANES
    step = tm // SUBLANES
    return [
        pl.BlockSpec((1, tm, width), lambda b, i: (b, i, 0)),
        pl.BlockSpec((1, SUBLANES, width), lambda b, i: (b, jnp.maximum(i * step - 1, 0), 0)),
        pl.BlockSpec((1, SUBLANES, width), lambda b, i: (b, jnp.minimum((i + 1) * step, nb8 - 1), 0)),
    ]


def _dn_prep_kernel(x_ref, xp_ref, xn_ref, w_ref, ab_ref, al_ref, dt_ref, gsum_ref,
                    q_ref, k_ref, v_ref, g_ref, beta_ref):
    i = pl.program_id(1)
    y = _silu(_conv3_rows(x_ref[0], xp_ref[0], xn_ref[0], w_ref[...], i == 0, i == pl.num_programs(1) - 1))
    q, k, v = y[:, :DN_WIDTH], y[:, DN_WIDTH:2 * DN_WIDTH], y[:, 2 * DN_WIDTH:]
    gs = gsum_ref[...]
    q = q * lax.rsqrt(_dot2_exact_rhs(q * q, gs) + RMS_EPS) * (HEAD_DIM ** -0.5)
    k = k * lax.rsqrt(_dot2_exact_rhs(k * k, gs) + RMS_EPS)
    for h in range(DN_HEADS):
        sl = slice(h * HEAD_DIM, (h + 1) * HEAD_DIM)
        q_ref[0, h] = q[:, sl]
        k_ref[0, h] = k[:, sl]
        v_ref[0, h] = v[:, sl]
    ab = ab_ref[0]
    lane = lax.broadcasted_iota(jnp.int32, ab.shape, 1)
    nh = N_DIR * DN_HEADS
    g_ref[0] = jnp.where(lane < nh, -jnp.exp(al_ref[...]) * _softplus(ab + dt_ref[...]), 0.0)
    beta_ref[0] = jnp.where(lane < nh, _sigmoid(pltpu.roll(ab, LANES - nh, 1)), 0.0)


def _dn_prep(qkv, ab, conv_w, a_log, dt_bias, gsum):
    bn, t, width = qkv.shape
    tm = min(512, t)
    pad = LANES - N_DIR * DN_HEADS
    al = jnp.pad(a_log.reshape(1, -1), ((0, 0), (0, pad)))
    dt = jnp.pad(dt_bias.reshape(1, -1), ((0, 0), (0, pad)))
    head = pl.BlockSpec((1, DN_HEADS, tm, HEAD_DIM), lambda b, i: (b, 0, i, 0))
    row = pl.BlockSpec((1, tm, LANES), lambda b, i: (b, i, 0))
    const = lambda shape: pl.BlockSpec(shape, lambda b, i: (0,) * len(shape))
    head_shape = jax.ShapeDtypeStruct((bn, DN_HEADS, t, HEAD_DIM), F32)
    row_shape = jax.ShapeDtypeStruct((bn, t, LANES), F32)
    return pl.pallas_call(
        _dn_prep_kernel,
        grid=(bn, t // tm),
        in_specs=_halo_specs(tm, t, width) + [const(conv_w.shape), row, const(al.shape), const(dt.shape),
                                              const(gsum.shape)],
        out_specs=[head, head, head, row, row],
        out_shape=[head_shape, head_shape, head_shape, row_shape, row_shape],
        compiler_params=_cparams("parallel", "parallel"),
        name="dn_prep",
    )(qkv, qkv, qkv, conv_w, ab, al, dt, gsum)


def _cumsum_rows(x, reverse):
    c = x.shape[0]
    row = lax.broadcasted_iota(jnp.int32, x.shape, 0)
    s = 1
    while s < c:
        if reverse:
            x = x + jnp.where(row < c - s, pltpu.roll(x, c - s, 0), 0.0)
        else:
            x = x + jnp.where(row >= s, pltpu.roll(x, s, 0), 0.0)
        s *= 2
    return x


def _dn_chunks(probs):
    c, d = DN_CHUNK, HEAD_DIM
    ii = lax.broadcasted_iota(jnp.int32, (c, c), 0)
    jj = lax.broadcasted_iota(jnp.int32, (c, c), 1)
    di = lax.broadcasted_iota(jnp.int32, (d, d), 0)
    dj = lax.broadcasted_iota(jnp.int32, (d, d), 1)
    n = len(probs)
    k16 = [pr["k"].astype(BF16) for pr in probs]
    kk = [_dot_nt(k16[i], k16[i]) for i in range(n)]
    qk = [_dot_nt(probs[i]["q"].astype(BF16), k16[i]) for i in range(n)]
    kt = [pr["k"].T for pr in probs]
    decay, e_cum, x, p = [], [], [], []
    for i, pr in enumerate(probs):
        incl = (ii >= jj) if pr["lower"] else (ii <= jj)
        strict = (ii > jj) if pr["lower"] else (ii < jj)
        dec = jnp.where(incl, jnp.exp(jnp.where(incl, pr["cum_col"] - pr["cum_row"], 0.0)), 0.0)
        ec = jnp.exp(pr["cum_col"])
        decay.append(dec)
        e_cum.append(ec)
        x.append(jnp.concatenate([pr["v"] * pr["beta_col"], pr["k"] * pr["beta_col"] * ec], axis=1))
        p.append(jnp.where(strict, -(kk[i] * pr["beta_col"] * dec), 0.0))
    for lvl in range(DN_LEVELS):
        if lvl + 1 < DN_LEVELS:
            r = [_dot3(p[i], jnp.concatenate([x[i], p[i]], axis=1)) for i in range(n)]
            x = [x[i] + r[i][:, :2 * d] for i in range(n)]
            p = [r[i][:, 2 * d:] for i in range(n)]
        else:
            x = [x[i] + _dot3(p[i], x[i]) for i in range(n)]
    lhs = [jnp.concatenate([qk[i] * decay[i], kt[i] * jnp.exp(probs[i]["tot"] - probs[i]["cum_row"])], axis=0)
           for i in range(n)]
    r = [_dot(lhs[i].astype(BF16), x[i].astype(BF16)) for i in range(n)]
    lhs = [jnp.concatenate([jnp.where(di == dj, jnp.exp(probs[i]["tot"]), 0.0) - r[i][c:, d:],
                            probs[i]["q"] * e_cum[i] - r[i][:c, d:]], axis=0) for i in range(n)]
    r2 = [_dot(lhs[i].astype(BF16), probs[i]["s"].astype(BF16)) for i in range(n)]
    return [(r2[i][:d] + r[i][c:, :d], r2[i][d:] + r[i][:c, :d]) for i in range(n)]


def _dn_scan_kernel(qf_ref, kf_ref, vf_ref, gf_ref, bf_ref, qb_ref, kb_ref, vb_ref, gb_ref, bb_ref,
                    s0_ref, of_ref, ob_ref, sfin_ref, s_ref):
    t = pl.program_id(1)

    @pl.when(t == 0)
    def _():
        s_ref[...] = s0_ref[0]

    probs = []
    for d_idx, (q_ref, k_ref, v_ref, g_ref, b_ref) in enumerate(
            ((qf_ref, kf_ref, vf_ref, gf_ref, bf_ref), (qb_ref, kb_ref, vb_ref, gb_ref, bb_ref))):
        lower = d_idx == 0
        cum = _cumsum_rows(g_ref[0], reverse=not lower)
        cum_t = cum.T
        beta = b_ref[0]
        last = DN_CHUNK - 1 if lower else 0
        for h in range(DN_HEADS):
            col = d_idx * DN_HEADS + h
            cum_row = cum_t[col:col + 1, :]
            probs.append(dict(q=q_ref[0, h], k=k_ref[0, h], v=v_ref[0, h], lower=lower,
                              cum_col=cum[:, col:col + 1], cum_row=cum_row, tot=cum_row[:, last:last + 1],
                              beta_col=beta[:, col:col + 1], s=s_ref[d_idx, h]))
    res = _dn_chunks(probs)
    for d_idx, o_ref in enumerate((of_ref, ob_ref)):
        for h in range(DN_HEADS):
            s_ref[d_idx, h] = res[d_idx * DN_HEADS + h][0]
        o_ref[0] = jnp.concatenate([res[d_idx * DN_HEADS + h][1] for h in range(DN_HEADS)], axis=1)

    @pl.when(t == pl.num_programs(1) - 1)
    def _():
        sfin_ref[0] = s_ref[...]


def _dn_scan(q, k, v, g, beta, s0):
    bn, nh, t, d = q.shape
    n = t // DN_CHUNK
    head_f = pl.BlockSpec((1, nh, DN_CHUNK, d), lambda b, i: (b, 0, i, 0))
    head_b = pl.BlockSpec((1, nh, DN_CHUNK, d), lambda b, i: (b, 0, n - 1 - i, 0))
    row_f = pl.BlockSpec((1, DN_CHUNK, LANES), lambda b, i: (b, i, 0))
    row_b = pl.BlockSpec((1, DN_CHUNK, LANES), lambda b, i: (b, n - 1 - i, 0))
    state = pl.BlockSpec((1, N_DIR, nh, d, d), lambda b, i: (b, 0, 0, 0, 0))
    out_f = pl.BlockSpec((1, DN_CHUNK, nh * d), lambda b, i: (b, i, 0))
    out_b = pl.BlockSpec((1, DN_CHUNK, nh * d), lambda b, i: (b, n - 1 - i, 0))
    o_shape = jax.ShapeDtypeStruct((bn, t, nh * d), F32)
    return pl.pallas_call(
        _dn_scan_kernel,
        grid=(bn, n),
        in_specs=[head_f, head_f, head_f, row_f, row_f, head_b, head_b, head_b, row_b, row_b, state],
        out_specs=[out_f, out_b, state],
        out_shape=[o_shape, o_shape, jax.ShapeDtypeStruct(s0.shape, F32)],
        scratch_shapes=[pltpu.VMEM((N_DIR, nh, d, d), F32)],
        compiler_params=_cparams("parallel", "arbitrary"),
        name="dn_scan",
    )(q, k, v, g, beta, q, k, v, g, beta, s0)


def _hy_prep_kernel(x_ref, xp_ref, xn_ref, w_ref, v_ref, v16_ref, x1_ref, x2_ref):
    i = pl.program_id(1)
    y = _conv3_rows(x_ref[0], xp_ref[0], xn_ref[0], w_ref[...], i == 0, i == pl.num_programs(1) - 1)
    v = y[:, :HY_CH]
    v_ref[...] = v
    v16_ref[...] = v.astype(BF16)
    x1_ref[...] = y[:, HY_CH:2 * HY_CH]
    x2_ref[...] = y[:, 2 * HY_CH:]


def _hy_prep(z, conv_w):
    bn, t, width = z.shape
    tm = min(512, t)
    out = pl.BlockSpec((tm, HY_CH), lambda b, i: (i, b))
    f32 = jax.ShapeDtypeStruct((t, bn * HY_CH), F32)
    return pl.pallas_call(
        _hy_prep_kernel,
        grid=(bn, t // tm),
        in_specs=_halo_specs(tm, t, width) + [pl.BlockSpec(conv_w.shape, lambda b, i: (0, 0))],
        out_specs=[out, out, out, out],
        out_shape=[f32, jax.ShapeDtypeStruct((t, bn * HY_CH), BF16), f32, f32],
        compiler_params=_cparams("parallel", "parallel"),
        name="hy_prep",
    )(z, z, z, conv_w)


def _dft_matrices(length):
    n = 2 * length
    k = jnp.arange(length, dtype=jnp.int32)
    ang = ((k[:, None] * k[None, :]) % n).astype(F32) * (2.0 * math.pi / n)
    alt = (1 - 2 * (k % 2)).astype(F32)
    sin = jnp.where(k[:, None] == 0, alt[None, :], jnp.sin(ang))
    return jnp.cos(ang).astype(BF16), sin.astype(BF16)


def _hy_filter_kernel(f_ref, w1_ref, b1_ref, f1_ref, w2_ref, b2_ref, f2_ref, w3_ref, b3_ref, dl_ref,
                      p_ref, q_ref, ssq_ref, nyq_ref):
    i = pl.program_id(1)
    feats = f_ref[...]
    h = jnp.sin(f1_ref[0] * (_dot3(feats, w1_ref[0]) + b1_ref[0]))
    h = jnp.sin(f2_ref[0] * (_dot3(h, w2_ref[0]) + b2_ref[0]))
    h = _dot3(h, w3_ref[0]) + b3_ref[0]
    win = jnp.exp(-feats[:, 0:1] * dl_ref[...])
    half = HY_ORDER * HY_CH
    fwd = h[:, :half] * jnp.concatenate([win] * HY_ORDER, axis=1)
    bwd = h[:, half:] * jnp.concatenate([win] * HY_ORDER, axis=1)
    row = lax.broadcasted_iota(jnp.int32, bwd.shape, 0)
    bwd = jnp.where((row == 0) & (i == 0), 0.0, bwd)
    p = fwd + bwd
    p_ref[0] = p.astype(BF16)
    q_ref[0] = (bwd - fwd).astype(BF16)
    alt = (1 - 2 * (row & 1)).astype(F32)

    @pl.when(i == 0)
    def _():
        ssq_ref[0] = jnp.zeros_like(ssq_ref[0])
        nyq_ref[0] = jnp.zeros_like(nyq_ref[0])

    ssq_ref[0] += jnp.sum(fwd * fwd + bwd * bwd, axis=0, keepdims=True)
    nyq_ref[0] += jnp.sum(p * alt, axis=0, keepdims=True)


def _hy_spec_kernel(c_ref, s_ref, p_ref, q_ref, ssq_ref, nyq_ref, hre_ref, him_ref):
    i = pl.program_id(1)
    scale = lax.rsqrt(ssq_ref[0] + RMS_EPS)
    hre = _dot(c_ref[...], p_ref[0]) * scale
    him = _dot(s_ref[...], q_ref[0]) * scale
    row = lax.broadcasted_iota(jnp.int32, him.shape, 0)
    hre_ref[0] = hre
    him_ref[0] = jnp.where((row == 0) & (i == 0), nyq_ref[0] * scale, him)


def _hy_filters(length, cmat, smat, w1, b1, f1, w2, b2, f2, w3, b3):
    depth = w1.shape[0]
    t = jnp.linspace(0.0, 1.0, length, dtype=F32)[:, None]
    omega = 2.0 * math.pi * jnp.arange(length, dtype=F32) / length
    bands = jnp.linspace(1e-4, HY_BANDS - 1, HY_BANDS, dtype=F32)
    ang = omega[:, None] * bands[None, :]
    feats = jnp.concatenate([t, jnp.cos(ang), -jnp.sin(ang), jnp.zeros((length, LANES - HY_EMB), F32)], axis=-1)
    w1p = jnp.pad(w1, ((0, 0), (0, LANES - HY_EMB), (0, 0)))
    max_decay = math.log(HY_TARGET) / HY_FAST_DECAY
    min_decay = math.log(HY_TARGET) / HY_SLOW_DECAY
    deltas = jnp.abs(jnp.linspace(min_decay, max_decay, HY_CH, dtype=F32))[None, :]
    half = HY_ORDER * HY_CH
    tm = min(512, length)
    vec = lambda a: a.reshape(depth, 1, -1)
    lay = lambda shape: pl.BlockSpec((1,) + shape, lambda l, i: (l,) + (0,) * len(shape))
    p, q, ssq, nyq = pl.pallas_call(
        _hy_filter_kernel,
        grid=(depth, length // tm),
        in_specs=[pl.BlockSpec((tm, LANES), lambda l, i: (i, 0)),
                  lay((LANES, HY_FFN)), lay((1, HY_FFN)), lay((1, HY_FFN)),
                  lay((HY_FFN, HY_FFN)), lay((1, HY_FFN)), lay((1, HY_FFN)),
                  lay((HY_FFN, 2 * half)), lay((1, 2 * half)),
                  pl.BlockSpec((1, HY_CH), lambda l, i: (0, 0))],
        out_specs=[pl.BlockSpec((1, tm, half), lambda l, i: (l, i, 0)),
                   pl.BlockSpec((1, tm, half), lambda l, i: (l, i, 0)),
                   lay((1, half)), lay((1, half))],
        out_shape=[jax.ShapeDtypeStruct((depth, length, half), BF16),
                   jax.ShapeDtypeStruct((depth, length, half), BF16),
                   jax.ShapeDtypeStruct((depth, 1, half), F32),
                   jax.ShapeDtypeStruct((depth, 1, half), F32)],
        compiler_params=_cparams("parallel", "arbitrary"),
        name="hy_filter",
    )(feats, w1p, vec(b1), vec(f1), w2, vec(b2), vec(f2), w3, vec(b3), deltas)
    tk = min(512, length)
    return pl.pallas_call(
        _hy_spec_kernel,
        grid=(depth, length // tk),
        in_specs=[pl.BlockSpec((tk, length), lambda l, i: (i, 0)),
                  pl.BlockSpec((tk, length), lambda l, i: (i, 0)),
                  lay((length, half)), lay((length, half)), lay((1, half)), lay((1, half))],
        out_specs=[pl.BlockSpec((1, tk, half), lambda l, i: (l, i, 0))] * 2,
        out_shape=[jax.ShapeDtypeStruct((depth, length, half), F32)] * 2,
        compiler_params=_cparams("parallel", "parallel"),
        name="hy_spec",
    )(cmat, smat, p, q, ssq, nyq)


def _hy_fwd_kernel(c_ref, s_ref, u_ref, hre_ref, him_ref, yre_ref, yim_ref, *, n_points, n_batch):
    i = pl.program_id(0)
    ure = _dot(c_ref[...], u_ref[...])
    usn = _dot(s_ref[...], u_ref[...])
    hre = jnp.concatenate([hre_ref[...]] * n_batch, axis=1)
    him = jnp.concatenate([him_ref[...]] * n_batch, axis=1)
    row0 = (lax.broadcasted_iota(jnp.int32, ure.shape, 0) == 0) & (i == 0)
    yre = jnp.where(row0, ure * hre * (1.0 / n_points), (ure * hre + usn * him) * (2.0 / n_points))
    yim = jnp.where(row0, usn * him * (1.0 / n_points), (usn * hre - ure * him) * (2.0 / n_points))
    yre_ref[...] = yre.astype(BF16)
    yim_ref[...] = yim.astype(BF16)


def _hy_inv_kernel(c_ref, s_ref, yre_ref, yim_ref, u_ref, gate_ref, skip_ref, o_ref, *o16_ref, n_batch):
    i = pl.program_id(0)
    tm = c_ref.shape[0]
    conv_c = _dot(c_ref[...], yre_ref[...])
    conv_s = _dot(s_ref[...], yim_ref[...])
    trow = lax.broadcasted_iota(jnp.int32, conv_c.shape, 0) + i * tm
    alt = (1 - 2 * (trow & 1)).astype(F32)
    nyq = yim_ref[0:1, :].astype(F32)
    conv = conv_c + jnp.where(trow == 0, 0.0, conv_s) + alt * nyq
    skip = jnp.concatenate([skip_ref[...]] * n_batch, axis=1)
    y = gate_ref[...] * (conv + skip * u_ref[...])
    o_ref[...] = y
    if o16_ref:
        o16_ref[0][...] = y.astype(BF16)


def _hy_long_conv(cmat, smat, u16, u, gate, hre, him, skip, order, emit_bf16):
    length, cols = u.shape
    n_batch = cols // HY_CH
    tk = min(512, length)
    full = pl.BlockSpec((length, cols), lambda i: (0, 0))
    mat = pl.BlockSpec((tk, length), lambda i: (i, 0))
    tile = pl.BlockSpec((tk, cols), lambda i: (i, 0))
    spec = pl.BlockSpec((tk, HY_CH), lambda i: (i, order))
    yre, yim = pl.pallas_call(
        functools.partial(_hy_fwd_kernel, n_points=2 * length, n_batch=n_batch),
        grid=(length // tk,),
        in_specs=[mat, mat, full, spec, spec],
        out_specs=[tile, tile],
        out_shape=[jax.ShapeDtypeStruct((length, cols), BF16)] * 2,
        compiler_params=_cparams("parallel"),
        name="hy_fwd",
    )(cmat, smat, u16, hre, him)
    tm = min(256, length)
    mat = pl.BlockSpec((tm, length), lambda i: (i, 0))
    tile = pl.BlockSpec((tm, cols), lambda i: (i, 0))
    out_shape = [jax.ShapeDtypeStruct((length, cols), F32)]
    if emit_bf16:
        out_shape.append(jax.ShapeDtypeStruct((length, cols), BF16))
    return pl.pallas_call(
        functools.partial(_hy_inv_kernel, n_batch=n_batch),
        grid=(length // tm,),
        in_specs=[mat, mat, full, full, tile, tile, pl.BlockSpec((1, HY_CH), lambda i: (0, 0))],
        out_specs=[tile] * len(out_shape),
        out_shape=out_shape,
        compiler_params=_cparams("parallel"),
        name="hy_inv",
    )(cmat, smat, yre, yim, u, gate, skip[order][None, :])


def _hyena(z, conv_w, cmat, smat, hre, him, skip):
    v, v16, x1, x2 = _hy_prep(z, conv_w)
    y1, y16 = _hy_long_conv(cmat, smat, v16, v, x1, hre, him, skip, 0, True)
    (y,) = _hy_long_conv(cmat, smat, y16, y1, x2, hre, him, skip, 1, False)
    return y


def _attn_kernel(sink_ref, q_ref, *rest, local, n_blocks):
    if local:
        kp_ref, kc_ref, kn_ref, vp_ref, vc_ref, vn_ref, kx_ref, vx_ref, o_ref = rest
        k_all = jnp.concatenate([kp_ref[0], kc_ref[0], kn_ref[0], kx_ref[0]], axis=0)
        v_all = jnp.concatenate([vp_ref[0], vc_ref[0], vn_ref[0], vx_ref[0]], axis=0)
    else:
        kx_ref, vx_ref, o_ref = rest
        k_all, v_all = kx_ref[0], vx_ref[0]
    blk = pl.program_id(1)
    q = q_ref[0]
    tq = q.shape[0]
    nk = k_all.shape[0]
    k16, v16 = k_all.astype(BF16), v_all.astype(BF16)
    lane = lax.broadcasted_iota(jnp.int32, (tq, LANES), 1)
    rows = SW_GROUP * tq
    r_idx = lax.broadcasted_iota(jnp.int32, (rows, nk), 0)
    if local:
        r = r_idx & (tq - 1)
        c = lax.broadcasted_iota(jnp.int32, (rows, nk), 1)
        rel = c - r - SW_BLOCK
        valid = (rel >= -SW_BLOCK) & (rel <= SW_BLOCK)
        valid = valid & ((c >= SW_BLOCK) | (blk > 0)) & ((c < 2 * SW_BLOCK) | (blk < n_blocks - 1))
        valid = valid | (c >= 3 * SW_BLOCK)
    out_heads = [None] * SW_Q_HEADS
    for j in range(SW_KV_HEADS):
        keep = (lane >= j * HEAD_DIM) & (lane < (j + 1) * HEAD_DIM)
        parts = []
        for g in range(SW_GROUP):
            hq = j * SW_GROUP + g
            chunk = q[:, (hq // 2) * LANES:(hq // 2 + 1) * LANES]
            if hq % 2 != j:
                chunk = pltpu.roll(chunk, HEAD_DIM, 1)
            parts.append(jnp.where(keep, chunk, 0.0))
        qz = jnp.concatenate(parts, axis=0).astype(BF16)
        s = _dot_nt(qz, k16)
        if local:
            s = jnp.where(valid, s, NEG_INF)
        sink = jnp.where(r_idx[:, 0:1] < tq, sink_ref[j * SW_GROUP],
                         jnp.where(r_idx[:, 0:1] < 2 * tq, sink_ref[j * SW_GROUP + 1], sink_ref[j * SW_GROUP + 2]))
        m = jnp.maximum(jnp.max(s, axis=-1, keepdims=True), sink)
        p = jnp.exp(s - m)
        denom = jnp.sum(p, axis=-1, keepdims=True) + jnp.exp(sink - m)
        o = _dot(p.astype(BF16), v16) / denom
        for g in range(SW_GROUP):
            hq = j * SW_GROUP + g
            og = o[g * tq:(g + 1) * tq]
            if hq % 2 != j:
                og = pltpu.roll(og, HEAD_DIM, 1)
            out_heads[hq] = og
    first_half = lane < HEAD_DIM
    o_ref[0] = jnp.concatenate(
        [jnp.where(first_half, out_heads[2 * c], out_heads[2 * c + 1]) for c in range(SW_Q_HEADS // 2)], axis=1)


def _attention(q, k, v, kx, vx, sink, local):
    bn, t, _ = q.shape
    nb = t // SW_BLOCK
    tx = kx.shape[1]
    qspec = pl.BlockSpec((1, SW_BLOCK, SW_WIDTH), lambda b, i: (b, i, 0))
    xspec = pl.BlockSpec((1, tx, SW_KV_WIDTH), lambda b, i: (b, 0, 0))
    in_specs = [pl.BlockSpec(memory_space=pltpu.SMEM), qspec]
    args = [sink, q]
    if local:
        prev = pl.BlockSpec((1, SW_BLOCK, SW_KV_WIDTH), lambda b, i: (b, jnp.maximum(i - 1, 0), 0))
        cur = pl.BlockSpec((1, SW_BLOCK, SW_KV_WIDTH), lambda b, i: (b, i, 0))
        nxt = pl.BlockSpec((1, SW_BLOCK, SW_KV_WIDTH), lambda b, i: (b, jnp.minimum(i + 1, nb - 1), 0))
        in_specs += [prev, cur, nxt, prev, cur, nxt]
        args += [k, k, k, v, v, v]
    in_specs += [xspec, xspec]
    args += [kx, vx]
    return pl.pallas_call(
        functools.partial(_attn_kernel, local=local, n_blocks=nb),
        grid=(bn, nb),
        in_specs=in_specs,
        out_specs=qspec,
        out_shape=jax.ShapeDtypeStruct((bn, t, SW_WIDTH), F32),
        compiler_params=_cparams("parallel", "parallel"),
        name="attn_local" if local else "attn_ctx",
    )(*args)


def _out_proj_kernel(of_ref, ob_ref, z_ref, gn_ref, gsum_ref, hy_ref, sw_ref, x_ref, gate_ref, w_ref, o_ref):
    o = of_ref[0] + ob_ref[0]
    mean = _dot2_exact_rhs(o * o, gsum_ref[...]) * (1.0 / HEAD_DIM)
    dn = o * lax.rsqrt(mean + RMS_EPS) * gn_ref[...] * _silu(z_ref[0])
    a, b = DN_WIDTH, DN_WIDTH + HY_CH
    mix = _dot(dn.astype(BF16), w_ref[0:a, :])
    mix += _dot(hy_ref[...].astype(BF16), w_ref[a:b, :])
    mix += _dot(sw_ref[0].astype(BF16), w_ref[b:, :])
    o_ref[0] = x_ref[0] + gate_ref[0] * mix


def _out_proj(o_f, o_b, z, gn, gsum, hy, sw, x, gate, w):
    bn, t, d = x.shape
    tm = min(512, t)
    seq = lambda n: pl.BlockSpec((1, tm, n), lambda b, i: (b, i, 0))
    const = lambda shape: pl.BlockSpec(shape, lambda b, i: (0,) * len(shape))
    return pl.pallas_call(
        _out_proj_kernel,
        grid=(bn, t // tm),
        in_specs=[seq(DN_WIDTH), seq(DN_WIDTH), seq(DN_WIDTH), const(gn.shape), const(gsum.shape),
                  pl.BlockSpec((tm, HY_CH), lambda b, i: (i, b)), seq(SW_WIDTH), seq(d),
                  pl.BlockSpec((1, 1, d), lambda b, i: (b, 0, 0)), const(w.shape)],
        out_specs=seq(d),
        out_shape=jax.ShapeDtypeStruct(x.shape, F32),
        compiler_params=_cparams("parallel", "parallel"),
        name="out_proj",
    )(o_f, o_b, z, gn, gsum, hy, sw, x, gate, w)


def _ffn_kernel(x_ref, g_ref, sh_ref, sc_ref, gate_ref, wg_ref, wu_ref, wd_ref, o_ref, h_ref, acc_ref):
    j = pl.program_id(2)

    @pl.when(j == 0)
    def _():
        h_ref[...] = _rms_mod(x_ref[0], g_ref[...], sh_ref[0], sc_ref[0]).astype(BF16)
        acc_ref[...] = jnp.zeros_like(acc_ref)

    h = h_ref[...]
    act = _silu(_dot(h, wg_ref[...])) * _dot(h, wu_ref[...])
    acc_ref[...] += _dot(act.astype(BF16), wd_ref[...])

    @pl.when(j == pl.num_programs(2) - 1)
    def _():
        o_ref[0] = x_ref[0] + gate_ref[0] * acc_ref[...]


def _ffn(x, g, shift, scale, gate, wg, wu, wd):
    bn, t, d = x.shape
    ff = wg.shape[1]
    tm = min(512, t)
    tf = ff // 2
    seq = pl.BlockSpec((1, tm, d), lambda b, i, j: (b, i, 0))
    mod = pl.BlockSpec((1, 1, d), lambda b, i, j: (b, 0, 0))
    return pl.pallas_call(
        _ffn_kernel,
        grid=(bn, t // tm, ff // tf),
        in_specs=[seq, pl.BlockSpec((1, d), lambda b, i, j: (0, 0)), mod, mod, mod,
                  pl.BlockSpec((d, tf), lambda b, i, j: (0, j)),
                  pl.BlockSpec((d, tf), lambda b, i, j: (0, j)),
                  pl.BlockSpec((tf, d), lambda b, i, j: (j, 0))],
        out_specs=seq,
        out_shape=jax.ShapeDtypeStruct(x.shape, F32),
        scratch_shapes=[pltpu.VMEM((tm, d), BF16), pltpu.VMEM((tm, d), F32)],
        compiler_params=_cparams("parallel", "parallel", "arbitrary"),
        name="ffn_dense",
    )(x, g, shift, scale, gate, wg, wu, wd)


def _top2_gate(logits):
    lane = lax.broadcasted_iota(jnp.int32, logits.shape, 1)
    lg = jnp.where(lane < N_EXPERTS, logits, NEG_INF)
    m1 = jnp.max(lg, axis=-1, keepdims=True)
    i1 = jnp.min(jnp.where(lg == m1, lane, LANES), axis=-1, keepdims=True)
    lg2 = jnp.where(lane == i1, NEG_INF, lg)
    m2 = jnp.max(lg2, axis=-1, keepdims=True)
    i2 = jnp.min(jnp.where(lg2 == m2, lane, LANES), axis=-1, keepdims=True)
    e2 = jnp.exp(m2 - m1)
    w1 = 1.0 / (1.0 + e2)
    return jnp.where(lane == i1, w1, jnp.where(lane == i2, e2 * w1, 0.0))


def _moe_kernel(x_ref, g_ref, sh_ref, sc_ref, gate_ref, r_ref, wg_ref, wu_ref, wd_ref, o_ref,
                h_ref, acc_ref, rw_ref):
    e = pl.program_id(2)
    j = pl.program_id(3)

    @pl.when((e == 0) & (j == 0))
    def _():
        h = _rms_mod(x_ref[0], g_ref[...], sh_ref[0], sc_ref[0])
        h_ref[...] = h.astype(BF16)
        rw_ref[...] = _top2_gate(_dot3(h, r_ref[...]))
        acc_ref[...] = jnp.zeros_like(acc_ref)

    h = h_ref[...]
    lane = lax.broadcasted_iota(jnp.int32, rw_ref.shape, 1)
    w_e = jnp.sum(jnp.where(lane == e, rw_ref[...], 0.0), axis=-1, keepdims=True)
    act = _silu(_dot(h, wg_ref[0])) * _dot(h, wu_ref[0]) * w_e
    acc_ref[...] += _dot(act.astype(BF16), wd_ref[0])

    @pl.when((e == pl.num_programs(2) - 1) & (j == pl.num_programs(3) - 1))
    def _():
        o_ref[0] = x_ref[0] + gate_ref[0] * acc_ref[...]


def _moe(x, g, shift, scale, gate, router, wg, wu, wd):
    bn, t, d = x.shape
    ne, _, ff = wg.shape
    tm = min(512, t)
    tf = ff // 2
    seq = pl.BlockSpec((1, tm, d), lambda b, i, e, j: (b, i, 0))
    mod = pl.BlockSpec((1, 1, d), lambda b, i, e, j: (b, 0, 0))
    return pl.pallas_call(
        _moe_kernel,
        grid=(bn, t // tm, ne, ff // tf),
        in_specs=[seq, pl.BlockSpec((1, d), lambda b, i, e, j: (0, 0)), mod, mod, mod,
                  pl.BlockSpec(router.shape, lambda b, i, e, j: (0, 0)),
                  pl.BlockSpec((1, d, tf), lambda b, i, e, j: (e, 0, j)),
                  pl.BlockSpec((1, d, tf), lambda b, i, e, j: (e, 0, j)),
                  pl.BlockSpec((1, tf, d), lambda b, i, e, j: (e, j, 0))],
        out_specs=seq,
        out_shape=jax.ShapeDtypeStruct(x.shape, F32),
        scratch_shapes=[pltpu.VMEM((tm, d), BF16), pltpu.VMEM((tm, d), F32), pltpu.VMEM((tm, LANES), F32)],
        compiler_params=_cparams("parallel", "parallel", "arbitrary", "arbitrary"),
        name="moe_dense",
    )(x, g, shift, scale, gate, router, wg, wu, wd)


def _final_norm_kernel(x_ref, g_ref, o_ref):
    x = x_ref[0]
    o_ref[0] = x * lax.rsqrt(jnp.mean(x * x, axis=-1, keepdims=True) + RMS_EPS) * g_ref[...]


def _final_norm(x, g):
    bn, t, d = x.shape
    tm = min(1024, t)
    seq = pl.BlockSpec((1, tm, d), lambda b, i: (b, i, 0))
    return pl.pallas_call(
        _final_norm_kernel,
        grid=(bn, t // tm),
        in_specs=[seq, pl.BlockSpec((1, d), lambda b, i: (0, 0))],
        out_specs=seq,
        out_shape=jax.ShapeDtypeStruct(x.shape, F32),
        compiler_params=_cparams("parallel", "parallel"),
        name="final_norm",
    )(x, g)


def _head_sum_matrix():
    idx = np.arange(DN_WIDTH) // HEAD_DIM
    return jnp.asarray(idx[:, None] == idx[None, :], dtype=BF16)


def kernel(x, c, ctx, c_ctx, w_mod, b_mod, norm_mix_g, norm_ffn_g, w_in, dn_conv_w, dn_a_log, dn_dt_bias, dn_norm_g, hy_conv_w, hy_w1, hy_b1, hy_freq1, hy_w2, hy_b2, hy_freq2, hy_w3, hy_b3, hy_skip, sw_sink, w_out, ffn_w_gate, ffn_w_up, ffn_w_down, moe_router, moe_w_gate, moe_w_up, moe_w_down, final_norm_g):
    bn, seq_len, d = x.shape
    ctx_len = ctx.shape[1]
    depth = w_mod.shape[0]

    c_rows = jnp.concatenate([c, c_ctx[None, :], jnp.zeros((SUBLANES - bn - 1, d), F32)], axis=0)
    mods = _modulation(c_rows, w_mod, b_mod).reshape(depth, SUBLANES, N_MOD, d)

    rope_tabs = _rope_tables(seq_len)
    gsum = _head_sum_matrix()
    cm_x, sm_x = _dft_matrices(seq_len)
    cm_c, sm_c = _dft_matrices(ctx_len)
    hy_params = (hy_w1, hy_b1, hy_freq1, hy_w2, hy_b2, hy_freq2, hy_w3, hy_b3)
    hre_x, him_x = _hy_filters(seq_len, cm_x, sm_x, *hy_params)
    hre_c, him_c = _hy_filters(ctx_len, cm_c, sm_c, *hy_params)
    router = jnp.pad(moe_router, ((0, 0), (0, 0), (0, LANES - N_EXPERTS)))
    zero_state = jnp.zeros((bn, N_DIR, DN_HEADS, HEAD_DIM, HEAD_DIM), F32)

    for layer in range(depth):
        last = layer == depth - 1
        mod_x = [mods[layer, :bn, m][:, None, :] for m in range(N_MOD)]
        mod_c = [jnp.broadcast_to(mods[layer, bn, m][None, None, :], (bn, 1, d)) for m in range(N_MOD)]
        g_mix = norm_mix_g[layer][None, :]
        g_ffn = norm_ffn_g[layer][None, :]
        w_in_l = _relayout_w_in(w_in[layer])
        w_out_l = w_out[layer].astype(BF16)
        gn = jnp.tile(dn_norm_g[layer], DN_HEADS)[None, :]

        qkv_x, z_x, hyp_x, swq_x, swk_x, swv_x, ab_x = _in_proj(x, g_mix, mod_x[0], mod_x[1], w_in_l, rope_tabs)
        qkv_c, z_c, hyp_c, swq_c, swk_c, swv_c, ab_c = _in_proj(ctx, g_mix, mod_c[0], mod_c[1], w_in_l, None)

        dn_in_c = _dn_prep(qkv_c, ab_c, dn_conv_w[layer], dn_a_log[layer], dn_dt_bias[layer], gsum)
        dn_in_x = _dn_prep(qkv_x, ab_x, dn_conv_w[layer], dn_a_log[layer], dn_dt_bias[layer], gsum)
        of_c, ob_c, state_c = _dn_scan(*dn_in_c, zero_state)
        of_x, ob_x, _ = _dn_scan(*dn_in_x, state_c)

        hy_x = _hyena(hyp_x, hy_conv_w[layer], cm_x, sm_x, hre_x[layer], him_x[layer], hy_skip[layer])
        sw_x = _attention(swq_x, swk_x, swv_x, swk_c, swv_c, sw_sink[layer], True)
        x = _out_proj(of_x, ob_x, z_x, gn, gsum, hy_x, sw_x, x, mod_x[2], w_out_l)

        if not last:
            hy_c = _hyena(hyp_c, hy_conv_w[layer], cm_c, sm_c, hre_c[layer], him_c[layer], hy_skip[layer])
            sw_c = _attention(swq_c, None, None, swk_c, swv_c, sw_sink[layer], False)
            ctx = _out_proj(of_c, ob_c, z_c, gn, gsum, hy_c, sw_c, ctx, mod_c[2], w_out_l)

        i = layer // 2
        streams = [(x, mod_x)] if last else [(x, mod_x), (ctx, mod_c)]
        outs = []
        for s, mod in streams:
            if layer % 2 == 0:
                outs.append(_ffn(s, g_ffn, mod[3], mod[4], mod[5], ffn_w_gate[i].astype(BF16),
                                 ffn_w_up[i].astype(BF16), ffn_w_down[i].astype(BF16)))
            else:
                outs.append(_moe(s, g_ffn, mod[3], mod[4], mod[5], router[i], moe_w_gate[i].astype(BF16),
                                 moe_w_up[i].astype(BF16), moe_w_down[i].astype(BF16)))
        x = outs[0]
        if not last:
            ctx = outs[1]

    return _final_norm(x, final_norm_g[None, :])
```

```python
import functools
import math

import jax
import jax.numpy as jnp
import numpy as np
from jax import lax
from jax.experimental import pallas as pl
from jax.experimental.pallas import tpu as pltpu

F32 = jnp.float32
BF16 = jnp.bfloat16

D_MODEL = 1024
DEPTH = 4
GRID_W = 64
N_MOD = 6
HEAD_DIM = 64
N_DIR = 2
RMS_EPS = 1e-6

DN_HEADS = 6
DN_WIDTH = DN_HEADS * HEAD_DIM
DN_CHUNK = 64
DN_LEVELS = 6

SW_Q_HEADS = 6
SW_KV_HEADS = 2
SW_GROUP = SW_Q_HEADS // SW_KV_HEADS
SW_WIDTH = SW_Q_HEADS * HEAD_DIM
SW_KV_WIDTH = SW_KV_HEADS * HEAD_DIM
SW_BLOCK = 128
ROPE_THETA = 10000.0
NEG_INF = -1e30

HY_CH = D_MODEL - DN_WIDTH - SW_WIDTH
HY_ORDER = 2
HY_BANDS = 16
HY_EMB = 1 + 2 * HY_BANDS
HY_FFN = 64
HY_FAST_DECAY = 0.3
HY_SLOW_DECAY = 1.5
HY_TARGET = 1e-2

N_EXPERTS = 8

LANES = 128
SUBLANES = 8
VMEM_LIMIT = 56 * 1024 * 1024

_IN_SIZES = (3 * DN_WIDTH, DN_WIDTH, N_DIR * DN_HEADS, N_DIR * DN_HEADS, 3 * HY_CH, SW_WIDTH, SW_KV_WIDTH, SW_KV_WIDTH)
_IN_OFF = tuple(int(v) for v in np.cumsum((0,) + _IN_SIZES))
_SEGS = (3 * DN_WIDTH, DN_WIDTH, 3 * HY_CH, SW_WIDTH, SW_KV_WIDTH, SW_KV_WIDTH, LANES)
_SEG_Q, _SEG_K = 3, 4


def _cparams(*sem):
    return pltpu.CompilerParams(dimension_semantics=sem, vmem_limit_bytes=VMEM_LIMIT)


def _dot(a, b):
    return jnp.dot(a, b, preferred_element_type=F32)


def _dot_nt(a, b):
    return lax.dot_general(a, b, (((1,), (1,)), ((), ())), preferred_element_type=F32)


def _split(x):
    hi = x.astype(BF16)
    lo = (x - hi.astype(F32)).astype(BF16)
    return hi, lo


def _dot3(a, b):
    ah, al = _split(a)
    bh, bl = _split(b)
    return _dot(ah, bh) + (_dot(al, bh) + _dot(ah, bl))


def _dot2_exact_rhs(a, b_bf16):
    ah, al = _split(a)
    return _dot(ah, b_bf16) + _dot(al, b_bf16)


def _sigmoid(x):
    return 1.0 / (1.0 + jnp.exp(-x))


def _silu(x):
    return x * _sigmoid(x)


def _softplus(x):
    return jnp.maximum(x, 0.0) + jnp.log(1.0 + jnp.exp(-jnp.abs(x)))


def _rms_mod(x, g, shift, scale):
    y = x * lax.rsqrt(jnp.mean(x * x, axis=-1, keepdims=True) + RMS_EPS)
    return (y * g) * (1.0 + scale) + shift


def _mod_kernel(c_ref, w_ref, b_ref, o_ref):
    o_ref[0] = _dot3(_silu(c_ref[...]), w_ref[0]) + b_ref[0]


def _modulation(c_rows, w_mod, b_mod):
    depth, d, n = w_mod.shape
    rows = c_rows.shape[0]
    tn = 1024
    return pl.pallas_call(
        _mod_kernel,
        grid=(depth, n // tn),
        in_specs=[
            pl.BlockSpec((rows, d), lambda l, j: (0, 0)),
            pl.BlockSpec((1, d, tn), lambda l, j: (l, 0, j)),
            pl.BlockSpec((1, 1, tn), lambda l, j: (l, 0, j)),
        ],
        out_specs=pl.BlockSpec((1, rows, tn), lambda l, j: (l, 0, j)),
        out_shape=jax.ShapeDtypeStruct((depth, rows, n), F32),
        compiler_params=_cparams("parallel", "parallel"),
        name="adaln_mod",
    )(c_rows, w_mod, b_mod.reshape(depth, 1, n))


def _rope_apply(x, cos, sin_signed):
    lane = lax.broadcasted_iota(jnp.int32, x.shape, 1)
    first = (lane & 31) < 16
    partner = jnp.where(first, pltpu.roll(x, LANES - 16, 1), pltpu.roll(x, 16, 1))
    return x * cos + partner * sin_signed


def _in_proj_kernel(x_ref, g_ref, sh_ref, sc_ref, w_ref, *rest, rope):
    if rope:
        cos_ref, sin_ref = rest[:2]
        outs = rest[2:]
    else:
        outs = rest
    h = _rms_mod(x_ref[0], g_ref[...], sh_ref[0], sc_ref[0]).astype(BF16)
    off = 0
    for idx, (o_ref, n) in enumerate(zip(outs, _SEGS)):
        r = _dot(h, w_ref[:, off:off + n])
        if idx == _SEG_Q:
            r = r * (HEAD_DIM ** -0.5)
        if rope and idx in (_SEG_Q, _SEG_K):
            cos, sin = cos_ref[...], sin_ref[...]
            r = jnp.concatenate(
                [_rope_apply(r[:, c:c + LANES], cos, sin) for c in range(0, n, LANES)], axis=1)
        o_ref[0] = r
        off += n


def _in_proj(x, g, shift, scale, w, rope_tabs):
    bn, t, d = x.shape
    tm = min(512, t)
    rope = rope_tabs is not None
    in_specs = [
        pl.BlockSpec((1, tm, d), lambda b, i: (b, i, 0)),
        pl.BlockSpec((1, d), lambda b, i: (0, 0)),
        pl.BlockSpec((1, 1, d), lambda b, i: (b, 0, 0)),
        pl.BlockSpec((1, 1, d), lambda b, i: (b, 0, 0)),
        pl.BlockSpec(w.shape, lambda b, i: (0, 0)),
    ]
    args = [x, g, shift, scale, w]
    if rope:
        in_specs += [pl.BlockSpec((tm, LANES), lambda b, i: (i, 0))] * 2
        args += list(rope_tabs)
    return pl.pallas_call(
        functools.partial(_in_proj_kernel, rope=rope),
        grid=(bn, t // tm),
        in_specs=in_specs,
        out_specs=[pl.BlockSpec((1, tm, n), lambda b, i: (b, i, 0)) for n in _SEGS],
        out_shape=[jax.ShapeDtypeStruct((bn, t, n), F32) for n in _SEGS],
        compiler_params=_cparams("parallel", "parallel"),
        name="in_proj_rope" if rope else "in_proj",
    )(*args)


def _relayout_w_in(w_in):
    o = _IN_OFF
    d = w_in.shape[0]
    pad = jnp.zeros((d, LANES - 2 * N_DIR * DN_HEADS), w_in.dtype)
    cols = [w_in[:, o[0]:o[2]], w_in[:, o[4]:o[8]], w_in[:, o[2]:o[4]], pad]
    return jnp.concatenate(cols, axis=1).astype(BF16)


def _rope_tables(length):
    n = HEAD_DIM // 4
    inv = jnp.power(ROPE_THETA, -jnp.arange(n, dtype=F32) / n)
    t = jnp.arange(length)
    row = (t // GRID_W).astype(F32)[:, None] * inv[None, :]
    col = (t % GRID_W).astype(F32)[:, None] * inv[None, :]
    cos = jnp.concatenate([jnp.cos(row), jnp.cos(row), jnp.cos(col), jnp.cos(col)], axis=1)
    sin = jnp.concatenate([-jnp.sin(row), jnp.sin(row), -jnp.sin(col), jnp.sin(col)], axis=1)
    return jnp.tile(cos, (1, 2)), jnp.tile(sin, (1, 2))


def _conv3_rows(x, prev8, next8, w, first, last):
    tm = x.shape[0]
    row = lax.broadcasted_iota(jnp.int32, x.shape, 0)
    before = jnp.where(first, 0.0, prev8[SUBLANES - 1:SUBLANES, :])
    after = jnp.where(last, 0.0, next8[0:1, :])
    xm = jnp.where(row == 0, before, pltpu.roll(x, 1, 0))
    xp = jnp.where(row == tm - 1, after, pltpu.roll(x, tm - 1, 0))
    return xm * w[0:1, :] + x * w[1:2, :] + xp * w[2:3, :]


def _halo_specs(tm, t, width):
    nb8 = t // SUBLANES
    step = tm // SUBLANES
    return [
        pl.BlockSpec((1, tm, width), lambda b, i: (b, i, 0)),
        pl.BlockSpec((1, SUBLANES, width), lambda b, i: (b, jnp.maximum(i * step - 1, 0), 0)),
        pl.BlockSpec((1, SUBLANES, width), lambda b, i: (b, jnp.minimum((i + 1) * step, nb8 - 1), 0)),
    ]


def _dn_prep_kernel(x_ref, xp_ref, xn_ref, w_ref, ab_ref, al_ref, dt_ref, gsum_ref,
                    q_ref, k_ref, v_ref, g_ref, beta_ref):
    i = pl.program_id(1)
    y = _silu(_conv3_rows(x_ref[0], xp_ref[0], xn_ref[0], w_ref[...], i == 0, i == pl.num_programs(1) - 1))
    q, k, v = y[:, :DN_WIDTH], y[:, DN_WIDTH:2 * DN_WIDTH], y[:, 2 * DN_WIDTH:]
    gs = gsum_ref[...]
    q = q * lax.rsqrt(_dot2_exact_rhs(q * q, gs) + RMS_EPS) * (HEAD_DIM ** -0.5)
    k = k * lax.rsqrt(_dot2_exact_rhs(k * k, gs) + RMS_EPS)
    for h in range(DN_HEADS):
        sl = slice(h * HEAD_DIM, (h + 1) * HEAD_DIM)
        q_ref[0, h] = q[:, sl]
        k_ref[0, h] = k[:, sl]
        v_ref[0, h] = v[:, sl]
    ab = ab_ref[0]
    lane = lax.broadcasted_iota(jnp.int32, ab.shape, 1)
    nh = N_DIR * DN_HEADS
    g_ref[0] = jnp.where(lane < nh, -jnp.exp(al_ref[...]) * _softplus(ab + dt_ref[...]), 0.0)
    beta_ref[0] = jnp.where(lane < nh, _sigmoid(pltpu.roll(ab, LANES - nh, 1)), 0.0)


def _dn_prep(qkv, ab, conv_w, a_log, dt_bias, gsum):
    bn, t, width = qkv.shape
    tm = min(512, t)
    pad = LANES - N_DIR * DN_HEADS
    al = jnp.pad(a_log.reshape(1, -1), ((0, 0), (0, pad)))
    dt = jnp.pad(dt_bias.reshape(1, -1), ((0, 0), (0, pad)))
    head = pl.BlockSpec((1, DN_HEADS, tm, HEAD_DIM), lambda b, i: (b, 0, i, 0))
    row = pl.BlockSpec((1, tm, LANES), lambda b, i: (b, i, 0))
    const = lambda shape: pl.BlockSpec(shape, lambda b, i: (0,) * len(shape))
    head_shape = jax.ShapeDtypeStruct((bn, DN_HEADS, t, HEAD_DIM), F32)
    row_shape = jax.ShapeDtypeStruct((bn, t, LANES), F32)
    return pl.pallas_call(
        _dn_prep_kernel,
        grid=(bn, t // tm),
        in_specs=_halo_specs(tm, t, width) + [const(conv_w.shape), row, const(al.shape), const(dt.shape),
                                              const(gsum.shape)],
        out_specs=[head, head, head, row, row],
        out_shape=[head_shape, head_shape, head_shape, row_shape, row_shape],
        compiler_params=_cparams("parallel", "parallel"),
        name="dn_prep",
    )(qkv, qkv, qkv, conv_w, ab, al, dt, gsum)


def _cumsum_rows(x, reverse):
    c = x.shape[0]
    row = lax.broadcasted_iota(jnp.int32, x.shape, 0)
    s = 1
    while s < c:
        if reverse:
            x = x + jnp.where(row < c - s, pltpu.roll(x, c - s, 0), 0.0)
        else:
            x = x + jnp.where(row >= s, pltpu.roll(x, s, 0), 0.0)
        s *= 2
    return x


def _dn_chunks(probs):
    c, d = DN_CHUNK, HEAD_DIM
    ii = lax.broadcasted_iota(jnp.int32, (c, c), 0)
    jj = lax.broadcasted_iota(jnp.int32, (c, c), 1)
    di = lax.broadcasted_iota(jnp.int32, (d, d), 0)
    dj = lax.broadcasted_iota(jnp.int32, (d, d), 1)
    n = len(probs)
    k16 = [pr["k"].astype(BF16) for pr in probs]
    kk = [_dot_nt(k16[i], k16[i]) for i in range(n)]
    qk = [_dot_nt(probs[i]["q"].astype(BF16), k16[i]) for i in range(n)]
    kt = [pr["k"].T for pr in probs]
    decay, e_cum, x, p = [], [], [], []
    for i, pr in enumerate(probs):
        incl = (ii >= jj) if pr["lower"] else (ii <= jj)
        strict = (ii > jj) if pr["lower"] else (ii < jj)
        dec = jnp.where(incl, jnp.exp(jnp.where(incl, pr["cum_col"] - pr["cum_row"], 0.0)), 0.0)
        ec = jnp.exp(pr["cum_col"])
        decay.append(dec)
        e_cum.append(ec)
        x.append(jnp.concatenate([pr["v"] * pr["beta_col"], pr["k"] * pr["beta_col"] * ec], axis=1))
        p.append(jnp.where(strict, -(kk[i] * pr["beta_col"] * dec), 0.0))
    for lvl in range(DN_LEVELS):
        if lvl + 1 < DN_LEVELS:
            r = [_dot3(p[i], jnp.concatenate([x[i], p[i]], axis=1)) for i in range(n)]
            x = [x[i] + r[i][:, :2 * d] for i in range(n)]
            p = [r[i][:, 2 * d:] for i in range(n)]
        else:
            x = [x[i] + _dot3(p[i], x[i]) for i in range(n)]
    lhs = [jnp.concatenate([qk[i] * decay[i], kt[i] * jnp.exp(probs[i]["tot"] - probs[i]["cum_row"])], axis=0)
           for i in range(n)]
    r = [_dot(lhs[i].astype(BF16), x[i].astype(BF16)) for i in range(n)]
    lhs = [jnp.concatenate([jnp.where(di == dj, jnp.exp(probs[i]["tot"]), 0.0) - r[i][c:, d:],
                            probs[i]["q"] * e_cum[i] - r[i][:c, d:]], axis=0) for i in range(n)]
    r2 = [_dot(lhs[i].astype(BF16), probs[i]["s"].astype(BF16)) for i in range(n)]
    return [(r2[i][:d] + r[i][c:, :d], r2[i][d:] + r[i][:c, :d]) for i in range(n)]


def _dn_scan_kernel(qf_ref, kf_ref, vf_ref, gf_ref, bf_ref, qb_ref, kb_ref, vb_ref, gb_ref, bb_ref,
                    s0_ref, of_ref, ob_ref, sfin_ref, s_ref):
    t = pl.program_id(1)

    @pl.when(t == 0)
    def _():
        s_ref[...] = s0_ref[0]

    probs = []
    for d_idx, (q_ref, k_ref, v_ref, g_ref, b_ref) in enumerate(
            ((qf_ref, kf_ref, vf_ref, gf_ref, bf_ref), (qb_ref, kb_ref, vb_ref, gb_ref, bb_ref))):
        lower = d_idx == 0
        cum = _cumsum_rows(g_ref[0], reverse=not lower)
        cum_t = cum.T
        beta = b_ref[0]
        last = DN_CHUNK - 1 if lower else 0
        for h in range(DN_HEADS):
            col = d_idx * DN_HEADS + h
            cum_row = cum_t[col:col + 1, :]
            probs.append(dict(q=q_ref[0, h], k=k_ref[0, h], v=v_ref[0, h], lower=lower,
                              cum_col=cum[:, col:col + 1], cum_row=cum_row, tot=cum_row[:, last:last + 1],
                              beta_col=beta[:, col:col + 1], s=s_ref[d_idx, h]))
    res = _dn_chunks(probs)
    for d_idx, o_ref in enumerate((of_ref, ob_ref)):
        for h in range(DN_HEADS):
            s_ref[d_idx, h] = res[d_idx * DN_HEADS + h][0]
        o_ref[0] = jnp.concatenate([res[d_idx * DN_HEADS + h][1] for h in range(DN_HEADS)], axis=1)

    @pl.when(t == pl.num_programs(1) - 1)
    def _():
        sfin_ref[0] = s_ref[...]


def _dn_scan(q, k, v, g, beta, s0):
    bn, nh, t, d = q.shape
    n = t // DN_CHUNK
    head_f = pl.BlockSpec((1, nh, DN_CHUNK, d), lambda b, i: (b, 0, i, 0))
    head_b = pl.BlockSpec((1, nh, DN_CHUNK, d), lambda b, i: (b, 0, n - 1 - i, 0))
    row_f = pl.BlockSpec((1, DN_CHUNK, LANES), lambda b, i: (b, i, 0))
    row_b = pl.BlockSpec((1, DN_CHUNK, LANES), lambda b, i: (b, n - 1 - i, 0))
    state = pl.BlockSpec((1, N_DIR, nh, d, d), lambda b, i: (b, 0, 0, 0, 0))
    out_f = pl.BlockSpec((1, DN_CHUNK, nh * d), lambda b, i: (b, i, 0))
    out_b = pl.BlockSpec((1, DN_CHUNK, nh * d), lambda b, i: (b, n - 1 - i, 0))
    o_shape = jax.ShapeDtypeStruct((bn, t, nh * d), F32)
    return pl.pallas_call(
        _dn_scan_kernel,
        grid=(bn, n),
        in_specs=[head_f, head_f, head_f, row_f, row_f, head_b, head_b, head_b, row_b, row_b, state],
        out_specs=[out_f, out_b, state],
        out_shape=[o_shape, o_shape, jax.ShapeDtypeStruct(s0.shape, F32)],
        scratch_shapes=[pltpu.VMEM((N_DIR, nh, d, d), F32)],
        compiler_params=_cparams("parallel", "arbitrary"),
        name="dn_scan",
    )(q, k, v, g, beta, q, k, v, g, beta, s0)


def _hy_prep_kernel(x_ref, xp_ref, xn_ref, w_ref, v_ref, v16_ref, x1_ref, x2_ref):
    i = pl.program_id(1)
    y = _conv3_rows(x_ref[0], xp_ref[0], xn_ref[0], w_ref[...], i == 0, i == pl.num_programs(1) - 1)
    v = y[:, :HY_CH]
    v_ref[...] = v
    v16_ref[...] = v.astype(BF16)
    x1_ref[...] = y[:, HY_CH:2 * HY_CH]
    x2_ref[...] = y[:, 2 * HY_CH:]


def _hy_prep(z, conv_w):
    bn, t, width = z.shape
    tm = min(512, t)
    out = pl.BlockSpec((tm, HY_CH), lambda b, i: (i, b))
    f32 = jax.ShapeDtypeStruct((t, bn * HY_CH), F32)
    return pl.pallas_call(
        _hy_prep_kernel,
        grid=(bn, t // tm),
        in_specs=_halo_specs(tm, t, width) + [pl.BlockSpec(conv_w.shape, lambda b, i: (0, 0))],
        out_specs=[out, out, out, out],
        out_shape=[f32, jax.ShapeDtypeStruct((t, bn * HY_CH), BF16), f32, f32],
        compiler_params=_cparams("parallel", "parallel"),
        name="hy_prep",
    )(z, z, z, conv_w)


def _dft_matrices(length):
    n = 2 * length
    k = jnp.arange(length, dtype=jnp.int32)
    ang = ((k[:, None] * k[None, :]) % n).astype(F32) * (2.0 * math.pi / n)
    alt = (1 - 2 * (k % 2)).astype(F32)
    sin = jnp.where(k[:, None] == 0, alt[None, :], jnp.sin(ang))
    return jnp.cos(ang).astype(BF16), sin.astype(BF16)


def _hy_filter_kernel(f_ref, w1_ref, b1_ref, f1_ref, w2_ref, b2_ref, f2_ref, w3_ref, b3_ref, dl_ref,
                      p_ref, q_ref, ssq_ref, nyq_ref):
    i = pl.program_id(1)
    feats = f_ref[...]
    h = jnp.sin(f1_ref[0] * (_dot3(feats, w1_ref[0]) + b1_ref[0]))
    h = jnp.sin(f2_ref[0] * (_dot3(h, w2_ref[0]) + b2_ref[0]))
    h = _dot3(h, w3_ref[0]) + b3_ref[0]
    win = jnp.exp(-feats[:, 0:1] * dl_ref[...])
    half = HY_ORDER * HY_CH
    fwd = h[:, :half] * jnp.concatenate([win] * HY_ORDER, axis=1)
    bwd = h[:, half:] * jnp.concatenate([win] * HY_ORDER, axis=1)
    row = lax.broadcasted_iota(jnp.int32, bwd.shape, 0)
    bwd = jnp.where((row == 0) & (i == 0), 0.0, bwd)
    p = fwd + bwd
    p_ref[0] = p.astype(BF16)
    q_ref[0] = (bwd - fwd).astype(BF16)
    alt = (1 - 2 * (row & 1)).astype(F32)

    @pl.when(i == 0)
    def _():
        ssq_ref[0] = jnp.zeros_like(ssq_ref[0])
        nyq_ref[0] = jnp.zeros_like(nyq_ref[0])

    ssq_ref[0] += jnp.sum(fwd * fwd + bwd * bwd, axis=0, keepdims=True)
    nyq_ref[0] += jnp.sum(p * alt, axis=0, keepdims=True)


def _hy_spec_kernel(c_ref, s_ref, p_ref, q_ref, ssq_ref, nyq_ref, hre_ref, him_ref):
    i = pl.program_id(1)
    scale = lax.rsqrt(ssq_ref[0] + RMS_EPS)
    hre = _dot(c_ref[...], p_ref[0]) * scale
    him = _dot(s_ref[...], q_ref[0]) * scale
    row = lax.broadcasted_iota(jnp.int32, him.shape, 0)
    hre_ref[0] = hre
    him_ref[0] = jnp.where((row == 0) & (i == 0), nyq_ref[0] * scale, him)


def _hy_filters(length, cmat, smat, w1, b1, f1, w2, b2, f2, w3, b3):
    depth = w1.shape[0]
    t = jnp.linspace(0.0, 1.0, length, dtype=F32)[:, None]
    omega = 2.0 * math.pi * jnp.arange(length, dtype=F32) / length
    bands = jnp.linspace(1e-4, HY_BANDS - 1, HY_BANDS, dtype=F32)
    ang = omega[:, None] * bands[None, :]
    feats = jnp.concatenate([t, jnp.cos(ang), -jnp.sin(ang), jnp.zeros((length, LANES - HY_EMB), F32)], axis=-1)
    w1p = jnp.pad(w1, ((0, 0), (0, LANES - HY_EMB), (0, 0)))
    max_decay = math.log(HY_TARGET) / HY_FAST_DECAY
    min_decay = math.log(HY_TARGET) / HY_SLOW_DECAY
    deltas = jnp.abs(jnp.linspace(min_decay, max_decay, HY_CH, dtype=F32))[None, :]
    half = HY_ORDER * HY_CH
    tm = min(512, length)
    vec = lambda a: a.reshape(depth, 1, -1)
    lay = lambda shape: pl.BlockSpec((1,) + shape, lambda l, i: (l,) + (0,) * len(shape))
    p, q, ssq, nyq = pl.pallas_call(
        _hy_filter_kernel,
        grid=(depth, length // tm),
        in_specs=[pl.BlockSpec((tm, LANES), lambda l, i: (i, 0)),
                  lay((LANES, HY_FFN)), lay((1, HY_FFN)), lay((1, HY_FFN)),
                  lay((HY_FFN, HY_FFN)), lay((1, HY_FFN)), lay((1, HY_FFN)),
                  lay((HY_FFN, 2 * half)), lay((1, 2 * half)),
                  pl.BlockSpec((1, HY_CH), lambda l, i: (0, 0))],
        out_specs=[pl.BlockSpec((1, tm, half), lambda l, i: (l, i, 0)),
                   pl.BlockSpec((1, tm, half), lambda l, i: (l, i, 0)),
                   lay((1, half)), lay((1, half))],
        out_shape=[jax.ShapeDtypeStruct((depth, length, half), BF16),
                   jax.ShapeDtypeStruct((depth, length, half), BF16),
                   jax.ShapeDtypeStruct((depth, 1, half), F32),
                   jax.ShapeDtypeStruct((depth, 1, half), F32)],
        compiler_params=_cparams("parallel", "arbitrary"),
        name="hy_filter",
    )(feats, w1p, vec(b1), vec(f1), w2, vec(b2), vec(f2), w3, vec(b3), deltas)
    tk = min(512, length)
    return pl.pallas_call(
        _hy_spec_kernel,
        grid=(depth, length // tk),
        in_specs=[pl.BlockSpec((tk, length), lambda l, i: (i, 0)),
                  pl.BlockSpec((tk, length), lambda l, i: (i, 0)),
                  lay((length, half)), lay((length, half)), lay((1, half)), lay((1, half))],
        out_specs=[pl.BlockSpec((1, tk, half), lambda l, i: (l, i, 0))] * 2,
        out_shape=[jax.ShapeDtypeStruct((depth, length, half), F32)] * 2,
        compiler_params=_cparams("parallel", "parallel"),
        name="hy_spec",
    )(cmat, smat, p, q, ssq, nyq)


def _hy_fwd_kernel(c_ref, s_ref, u_ref, hre_ref, him_ref, yre_ref, yim_ref, *, n_points, n_batch):
    i = pl.program_id(0)
    ure = _dot(c_ref[...], u_ref[...])
    usn = _dot(s_ref[...], u_ref[...])
    hre = jnp.concatenate([hre_ref[...]] * n_batch, axis=1)
    him = jnp.concatenate([him_ref[...]] * n_batch, axis=1)
    row0 = (lax.broadcasted_iota(jnp.int32, ure.shape, 0) == 0) & (i == 0)
    yre = jnp.where(row0, ure * hre * (1.0 / n_points), (ure * hre + usn * him) * (2.0 / n_points))
    yim = jnp.where(row0, usn * him * (1.0 / n_points), (usn * hre - ure * him) * (2.0 / n_points))
    yre_ref[...] = yre.astype(BF16)
    yim_ref[...] = yim.astype(BF16)


def _hy_inv_kernel(c_ref, s_ref, yre_ref, yim_ref, u_ref, gate_ref, skip_ref, o_ref, *o16_ref, n_batch):
    i = pl.program_id(0)
    tm = c_ref.shape[0]
    conv_c = _dot(c_ref[...], yre_ref[...])
    conv_s = _dot(s_ref[...], yim_ref[...])
    trow = lax.broadcasted_iota(jnp.int32, conv_c.shape, 0) + i * tm
    alt = (1 - 2 * (trow & 1)).astype(F32)
    nyq = yim_ref[0:1, :].astype(F32)
    conv = conv_c + jnp.where(trow == 0, 0.0, conv_s) + alt * nyq
    skip = jnp.concatenate([skip_ref[...]] * n_batch, axis=1)
    y = gate_ref[...] * (conv + skip * u_ref[...])
    o_ref[...] = y
    if o16_ref:
        o16_ref[0][...] = y.astype(BF16)


def _hy_long_conv(cmat, smat, u16, u, gate, hre, him, skip, order, emit_bf16):
    length, cols = u.shape
    n_batch = cols // HY_CH
    tk = min(512, length)
    full = pl.BlockSpec((length, cols), lambda i: (0, 0))
    mat = pl.BlockSpec((tk, length), lambda i: (i, 0))
    tile = pl.BlockSpec((tk, cols), lambda i: (i, 0))
    spec = pl.BlockSpec((tk, HY_CH), lambda i: (i, order))
    yre, yim = pl.pallas_call(
        functools.partial(_hy_fwd_kernel, n_points=2 * length, n_batch=n_batch),
        grid=(length // tk,),
        in_specs=[mat, mat, full, spec, spec],
        out_specs=[tile, tile],
        out_shape=[jax.ShapeDtypeStruct((length, cols), BF16)] * 2,
        compiler_params=_cparams("parallel"),
        name="hy_fwd",
    )(cmat, smat, u16, hre, him)
    tm = min(256, length)
    mat = pl.BlockSpec((tm, length), lambda i: (i, 0))
    tile = pl.BlockSpec((tm, cols), lambda i: (i, 0))
    out_shape = [jax.ShapeDtypeStruct((length, cols), F32)]
    if emit_bf16:
        out_shape.append(jax.ShapeDtypeStruct((length, cols), BF16))
    return pl.pallas_call(
        functools.partial(_hy_inv_kernel, n_batch=n_batch),
        grid=(length // tm,),
        in_specs=[mat, mat, full, full, tile, tile, pl.BlockSpec((1, HY_CH), lambda i: (0, 0))],
        out_specs=[tile] * len(out_shape),
        out_shape=out_shape,
        compiler_params=_cparams("parallel"),
        name="hy_inv",
    )(cmat, smat, yre, yim, u, gate, skip[order][None, :])


def _hyena(z, conv_w, cmat, smat, hre, him, skip):
    v, v16, x1, x2 = _hy_prep(z, conv_w)
    y1, y16 = _hy_long_conv(cmat, smat, v16, v, x1, hre, him, skip, 0, True)
    (y,) = _hy_long_conv(cmat, smat, y16, y1, x2, hre, him, skip, 1, False)
    return y


def _attn_kernel(sink_ref, q_ref, *rest, local, n_blocks):
    if local:
        kp_ref, kc_ref, kn_ref, vp_ref, vc_ref, vn_ref, kx_ref, vx_ref, o_ref = rest
        k_all = jnp.concatenate([kp_ref[0], kc_ref[0], kn_ref[0], kx_ref[0]], axis=0)
        v_all = jnp.concatenate([vp_ref[0], vc_ref[0], vn_ref[0], vx_ref[0]], axis=0)
    else:
        kx_ref, vx_ref, o_ref = rest
        k_all, v_all = kx_ref[0], vx_ref[0]
    blk = pl.program_id(1)
    q = q_ref[0]
    tq = q.shape[0]
    nk = k_all.shape[0]
    k16, v16 = k_all.astype(BF16), v_all.astype(BF16)
    lane = lax.broadcasted_iota(jnp.int32, (tq, LANES), 1)
    rows = SW_GROUP * tq
    r_idx = lax.broadcasted_iota(jnp.int32, (rows, nk), 0)
    if local:
        r = r_idx & (tq - 1)
        c = lax.broadcasted_iota(jnp.int32, (rows, nk), 1)
        rel = c - r - SW_BLOCK
        valid = (rel >= -SW_BLOCK) & (rel <= SW_BLOCK)
        valid = valid & ((c >= SW_BLOCK) | (blk > 0)) & ((c < 2 * SW_BLOCK) | (blk < n_blocks - 1))
        valid = valid | (c >= 3 * SW_BLOCK)
    out_heads = [None] * SW_Q_HEADS
    for j in range(SW_KV_HEADS):
        keep = (lane >= j * HEAD_DIM) & (lane < (j + 1) * HEAD_DIM)
        parts = []
        for g in range(SW_GROUP):
            hq = j * SW_GROUP + g
            chunk = q[:, (hq // 2) * LANES:(hq // 2 + 1) * LANES]
            if hq % 2 != j:
                chunk = pltpu.roll(chunk, HEAD_DIM, 1)
            parts.append(jnp.where(keep, chunk, 0.0))
        qz = jnp.concatenate(parts, axis=0).astype(BF16)
        s = _dot_nt(qz, k16)
        if local:
            s = jnp.where(valid, s, NEG_INF)
        sink = jnp.where(r_idx[:, 0:1] < tq, sink_ref[j * SW_GROUP],
                         jnp.where(r_idx[:, 0:1] < 2 * tq, sink_ref[j * SW_GROUP + 1], sink_ref[j * SW_GROUP + 2]))
        m = jnp.maximum(jnp.max(s, axis=-1, keepdims=True), sink)
        p = jnp.exp(s - m)
        denom = jnp.sum(p, axis=-1, keepdims=True) + jnp.exp(sink - m)
        o = _dot(p.astype(BF16), v16) / denom
        for g in range(SW_GROUP):
            hq = j * SW_GROUP + g
            og = o[g * tq:(g + 1) * tq]
            if hq % 2 != j:
                og = pltpu.roll(og, HEAD_DIM, 1)
            out_heads[hq] = og
    first_half = lane < HEAD_DIM
    o_ref[0] = jnp.concatenate(
        [jnp.where(first_half, out_heads[2 * c], out_heads[2 * c + 1]) for c in range(SW_Q_HEADS // 2)], axis=1)


def _attention(q, k, v, kx, vx, sink, local):
    bn, t, _ = q.shape
    nb = t // SW_BLOCK
    tx = kx.shape[1]
    qspec = pl.BlockSpec((1, SW_BLOCK, SW_WIDTH), lambda b, i: (b, i, 0))
    xspec = pl.BlockSpec((1, tx, SW_KV_WIDTH), lambda b, i: (b, 0, 0))
    in_specs = [pl.BlockSpec(memory_space=pltpu.SMEM), qspec]
    args = [sink, q]
    if local:
        prev = pl.BlockSpec((1, SW_BLOCK, SW_KV_WIDTH), lambda b, i: (b, jnp.maximum(i - 1, 0), 0))
        cur = pl.BlockSpec((1, SW_BLOCK, SW_KV_WIDTH), lambda b, i: (b, i, 0))
        nxt = pl.BlockSpec((1, SW_BLOCK, SW_KV_WIDTH), lambda b, i: (b, jnp.minimum(i + 1, nb - 1), 0))
        in_specs += [prev, cur, nxt, prev, cur, nxt]
        args += [k, k, k, v, v, v]
    in_specs += [xspec, xspec]
    args += [kx, vx]
    return pl.pallas_call(
        functools.partial(_attn_kernel, local=local, n_blocks=nb),
        grid=(bn, nb),
        in_specs=in_specs,
        out_specs=qspec,
        out_shape=jax.ShapeDtypeStruct((bn, t, SW_WIDTH), F32),
        compiler_params=_cparams("parallel", "parallel"),
        name="attn_local" if local else "attn_ctx",
    )(*args)


def _out_proj_kernel(of_ref, ob_ref, z_ref, gn_ref, gsum_ref, hy_ref, sw_ref, x_ref, gate_ref, w_ref, o_ref):
    o = of_ref[0] + ob_ref[0]
    mean = _dot2_exact_rhs(o * o, gsum_ref[...]) * (1.0 / HEAD_DIM)
    dn = o * lax.rsqrt(mean + RMS_EPS) * gn_ref[...] * _silu(z_ref[0])
    a, b = DN_WIDTH, DN_WIDTH + HY_CH
    mix = _dot(dn.astype(BF16), w_ref[0:a, :])
    mix += _dot(hy_ref[...].astype(BF16), w_ref[a:b, :])
    mix += _dot(sw_ref[0].astype(BF16), w_ref[b:, :])
    o_ref[0] = x_ref[0] + gate_ref[0] * mix


def _out_proj(o_f, o_b, z, gn, gsum, hy, sw, x, gate, w):
    bn, t, d = x.shape
    tm = min(512, t)
    seq = lambda n: pl.BlockSpec((1, tm, n), lambda b, i: (b, i, 0))
    const = lambda shape: pl.BlockSpec(shape, lambda b, i: (0,) * len(shape))
    return pl.pallas_call(
        _out_proj_kernel,
        grid=(bn, t // tm),
        in_specs=[seq(DN_WIDTH), seq(DN_WIDTH), seq(DN_WIDTH), const(gn.shape), const(gsum.shape),
                  pl.BlockSpec((tm, HY_CH), lambda b, i: (i, b)), seq(SW_WIDTH), seq(d),
                  pl.BlockSpec((1, 1, d), lambda b, i: (b, 0, 0)), const(w.shape)],
        out_specs=seq(d),
        out_shape=jax.ShapeDtypeStruct(x.shape, F32),
        compiler_params=_cparams("parallel", "parallel"),
        name="out_proj",
    )(o_f, o_b, z, gn, gsum, hy, sw, x, gate, w)


def _ffn_kernel(x_ref, g_ref, sh_ref, sc_ref, gate_ref, wg_ref, wu_ref, wd_ref, o_ref, h_ref, acc_ref):
    j = pl.program_id(2)

    @pl.when(j == 0)
    def _():
        h_ref[...] = _rms_mod(x_ref[0], g_ref[...], sh_ref[0], sc_ref[0]).astype(BF16)
        acc_ref[...] = jnp.zeros_like(acc_ref)

    h = h_ref[...]
    act = _silu(_dot(h, wg_ref[...])) * _dot(h, wu_ref[...])
    acc_ref[...] += _dot(act.astype(BF16), wd_ref[...])

    @pl.when(j == pl.num_programs(2) - 1)
    def _():
        o_ref[0] = x_ref[0] + gate_ref[0] * acc_ref[...]


def _ffn(x, g, shift, scale, gate, wg, wu, wd):
    bn, t, d = x.shape
    ff = wg.shape[1]
    tm = min(512, t)
    tf = ff // 2
    seq = pl.BlockSpec((1, tm, d), lambda b, i, j: (b, i, 0))
    mod = pl.BlockSpec((1, 1, d), lambda b, i, j: (b, 0, 0))
    return pl.pallas_call(
        _ffn_kernel,
        grid=(bn, t // tm, ff // tf),
        in_specs=[seq, pl.BlockSpec((1, d), lambda b, i, j: (0, 0)), mod, mod, mod,
                  pl.BlockSpec((d, tf), lambda b, i, j: (0, j)),
                  pl.BlockSpec((d, tf), lambda b, i, j: (0, j)),
                  pl.BlockSpec((tf, d), lambda b, i, j: (j, 0))],
        out_specs=seq,
        out_shape=jax.ShapeDtypeStruct(x.shape, F32),
        scratch_shapes=[pltpu.VMEM((tm, d), BF16), pltpu.VMEM((tm, d), F32)],
        compiler_params=_cparams("parallel", "parallel", "arbitrary"),
        name="ffn_dense",
    )(x, g, shift, scale, gate, wg, wu, wd)


MOE_TILE = 512
_INFO_E, _INFO_W, _INFO_RANK = 0, 2, 4


def _route_kernel(x_ref, g_ref, sh_ref, sc_ref, r_ref, h_ref, info_ref, cnt_ref, carry_ref):
    @pl.when((pl.program_id(0) == 0) & (pl.program_id(1) == 0))
    def _():
        carry_ref[...] = jnp.zeros_like(carry_ref)

    h = _rms_mod(x_ref[0], g_ref[...], sh_ref[0], sc_ref[0])
    h_ref[0] = h
    logits = _dot3(h, r_ref[...])
    tm = logits.shape[0]
    lane = lax.broadcasted_iota(jnp.int32, logits.shape, 1)
    lg = jnp.where(lane < N_EXPERTS, logits, NEG_INF)
    m1 = jnp.max(lg, axis=-1, keepdims=True)
    i1 = jnp.min(jnp.where(lg == m1, lane, LANES), axis=-1, keepdims=True)
    lg2 = jnp.where(lane == i1, NEG_INF, lg)
    m2 = jnp.max(lg2, axis=-1, keepdims=True)
    i2 = jnp.min(jnp.where(lg2 == m2, lane, LANES), axis=-1, keepdims=True)
    e2 = jnp.exp(m2 - m1)
    w1 = 1.0 / (1.0 + e2)
    w2 = e2 * w1
    chosen = jnp.where((lane == i1) | (lane == i2), 1.0, 0.0)
    rr = lax.broadcasted_iota(jnp.int32, (tm, tm), 0)
    cc = lax.broadcasted_iota(jnp.int32, (tm, tm), 1)
    earlier = jnp.where(rr > cc, 1.0, 0.0).astype(BF16)
    before = _dot(earlier, chosen.astype(BF16)) + carry_ref[...]
    rank1 = jnp.sum(jnp.where(lane == i1, before, 0.0), axis=-1, keepdims=True)
    rank2 = jnp.sum(jnp.where(lane == i2, before, 0.0), axis=-1, keepdims=True)
    info = jnp.zeros_like(logits)
    for pos, val in ((_INFO_E, i1.astype(F32)), (_INFO_E + 1, i2.astype(F32)), (_INFO_W, w1), (_INFO_W + 1, w2),
                     (_INFO_RANK, rank1), (_INFO_RANK + 1, rank2)):
        info = jnp.where(lane == pos, val, info)
    info_ref[0] = info
    carry_ref[...] += jnp.sum(chosen, axis=0, keepdims=True)
    cnt_ref[...] = carry_ref[...]


def _row_copies(src_row, dst_row, sem, tm, wait):
    def body(r, carry):
        for c in range(2):
            cp = pltpu.make_async_copy(src_row(r, c), dst_row(r, c), sem.at[c])
            if wait:
                cp.wait()
            else:
                cp.start()
        return carry
    lax.fori_loop(0, tm, body, 0, unroll=8)


def _scatter_kernel(dest_ref, h_ref, xs_in_ref, xs_ref, sem):
    del xs_in_ref
    tm = h_ref.shape[1]
    src = lambda r, c: h_ref.at[0, pl.ds(r, 1)]
    dst = lambda r, c: xs_ref.at[pl.ds(dest_ref[0, 0, c * tm + r], 1)]
    _row_copies(src, dst, sem, tm, wait=False)
    _row_copies(src, dst, sem, tm, wait=True)


def _group_ffn_kernel(te_ref, tb_ref, nv_ref, xs_ref, wg_ref, wu_ref, wd_ref, ys_ref, x16_ref, acc_ref):
    j = pl.program_id(0)
    f = pl.program_id(1)

    @pl.when(j < nv_ref[0])
    def _():
        @pl.when(f == 0)
        def _():
            x16_ref[...] = xs_ref[...].astype(BF16)
            acc_ref[...] = jnp.zeros_like(acc_ref)

        h = x16_ref[...]
        act = _silu(_dot(h, wg_ref[0])) * _dot(h, wu_ref[0])
        acc_ref[...] += _dot(act.astype(BF16), wd_ref[0])

        @pl.when(f == pl.num_programs(1) - 1)
        def _():
            ys_ref[...] = acc_ref[...]

    @pl.when(j >= nv_ref[0])
    def _():
        ys_ref[...] = jnp.zeros_like(ys_ref)


def _combine_kernel(dest_ref, x_ref, gate_ref, info_ref, ys_ref, o_ref, buf_ref, sem):
    tm = x_ref.shape[1]
    src = lambda r, c: ys_ref.at[pl.ds(dest_ref[0, 0, c * tm + r], 1)]
    dst = lambda r, c: buf_ref.at[c, pl.ds(r, 1)]
    _row_copies(src, dst, sem, tm, wait=False)
    _row_copies(src, dst, sem, tm, wait=True)
    info = info_ref[0]
    mix = info[:, _INFO_W:_INFO_W + 1] * buf_ref[0] + info[:, _INFO_W + 1:_INFO_W + 2] * buf_ref[1]
    o_ref[0] = x_ref[0] + gate_ref[0] * mix


def _moe(x, g, shift, scale, gate, router, wg, wu, wd):
    bn, t, d = x.shape
    ne, _, ff = wg.shape
    tm = MOE_TILE
    nt = t // tm
    n_tok = bn * t
    n_slots = 2 * n_tok // tm + ne
    seq = pl.BlockSpec((1, tm, d), lambda b, i: (b, i, 0))
    mod = pl.BlockSpec((1, 1, d), lambda b, i: (b, 0, 0))
    rec = pl.BlockSpec((1, tm, LANES), lambda b, i: (b, i, 0))
    h, info, counts = pl.pallas_call(
        _route_kernel,
        grid=(bn, nt),
        in_specs=[seq, pl.BlockSpec((1, d), lambda b, i: (0, 0)), mod, mod,
                  pl.BlockSpec(router.shape, lambda b, i: (0, 0))],
        out_specs=[seq, rec, pl.BlockSpec((1, LANES), lambda b, i: (0, 0))],
        out_shape=[jax.ShapeDtypeStruct(x.shape, F32), jax.ShapeDtypeStruct((bn, t, LANES), F32),
                   jax.ShapeDtypeStruct((1, LANES), F32)],
        scratch_shapes=[pltpu.VMEM((1, LANES), F32)],
        compiler_params=_cparams("arbitrary", "arbitrary"),
        name="moe_route",
    )(x, g, shift, scale, router)

    flat = info.reshape(n_tok, LANES)
    expert = flat[:, _INFO_E:_INFO_E + 2].astype(jnp.int32)
    rank = flat[:, _INFO_RANK:_INFO_RANK + 2].astype(jnp.int32)
    tiles_e = (counts[0, :ne].astype(jnp.int32) + tm - 1) // tm
    ends = jnp.cumsum(tiles_e)
    dest = (ends - tiles_e)[expert] * tm + rank
    dest = dest.reshape(bn * nt, tm, 2).transpose(0, 2, 1).reshape(bn * nt, 1, 2 * tm)
    n_valid = ends[-1]
    slot = jnp.minimum(jnp.arange(n_slots, dtype=jnp.int32), n_valid - 1)
    slot_expert = jnp.minimum(jnp.searchsorted(ends, slot, side="right"), ne - 1).astype(jnp.int32)

    dspec = pl.BlockSpec((1, 1, 2 * tm), lambda b, i: (b * nt + i, 0, 0), memory_space=pltpu.SMEM)
    anyspec = pl.BlockSpec(memory_space=pl.ANY)
    xs = pl.pallas_call(
        _scatter_kernel,
        grid=(bn, nt),
        in_specs=[dspec, seq, anyspec],
        out_specs=anyspec,
        out_shape=jax.ShapeDtypeStruct((n_slots * tm, d), F32),
        scratch_shapes=[pltpu.SemaphoreType.DMA((2,))],
        input_output_aliases={2: 0},
        compiler_params=_cparams("arbitrary", "arbitrary"),
        name="moe_scatter",
    )(dest, h, jnp.zeros((n_slots * tm, d), F32))

    nf = 2
    tf = ff // nf
    last = nf - 1
    fidx = lambda j, f, nv: jnp.where(j < nv[0], f, last)
    ys = pl.pallas_call(
        _group_ffn_kernel,
        grid_spec=pltpu.PrefetchScalarGridSpec(
            num_scalar_prefetch=3,
            grid=(n_slots, nf),
            in_specs=[pl.BlockSpec((tm, d), lambda j, f, te, tb, nv: (tb[j], 0)),
                      pl.BlockSpec((1, d, tf), lambda j, f, te, tb, nv: (te[j], 0, fidx(j, f, nv))),
                      pl.BlockSpec((1, d, tf), lambda j, f, te, tb, nv: (te[j], 0, fidx(j, f, nv))),
                      pl.BlockSpec((1, tf, d), lambda j, f, te, tb, nv: (te[j], fidx(j, f, nv), 0))],
            out_specs=pl.BlockSpec((tm, d), lambda j, f, te, tb, nv: (j, 0)),
            scratch_shapes=[pltpu.VMEM((tm, d), BF16), pltpu.VMEM((tm, d), F32)]),
        out_shape=jax.ShapeDtypeStruct((n_slots * tm, d), F32),
        compiler_params=_cparams("arbitrary", "arbitrary"),
        name="moe_group_ffn",
    )(slot_expert, slot, n_valid.reshape(1), xs, wg, wu, wd)

    return pl.pallas_call(
        _combine_kernel,
        grid=(bn, nt),
        in_specs=[dspec, seq, mod, rec, anyspec],
        out_specs=seq,
        out_shape=jax.ShapeDtypeStruct(x.shape, F32),
        scratch_shapes=[pltpu.VMEM((2, tm, d), F32), pltpu.SemaphoreType.DMA((2,))],
        compiler_params=_cparams("arbitrary", "arbitrary"),
        name="moe_combine",
    )(dest, x, gate, info, ys)


def _final_norm_kernel(x_ref, g_ref, o_ref):
    x = x_ref[0]
    o_ref[0] = x * lax.rsqrt(jnp.mean(x * x, axis=-1, keepdims=True) + RMS_EPS) * g_ref[...]


def _final_norm(x, g):
    bn, t, d = x.shape
    tm = min(1024, t)
    seq = pl.BlockSpec((1, tm, d), lambda b, i: (b, i, 0))
    return pl.pallas_call(
        _final_norm_kernel,
        grid=(bn, t // tm),
        in_specs=[seq, pl.BlockSpec((1, d), lambda b, i: (0, 0))],
        out_specs=seq,
        out_shape=jax.ShapeDtypeStruct(x.shape, F32),
        compiler_params=_cparams("parallel", "parallel"),
        name="final_norm",
    )(x, g)


def _head_sum_matrix():
    idx = np.arange(DN_WIDTH) // HEAD_DIM
    return jnp.asarray(idx[:, None] == idx[None, :], dtype=BF16)


def kernel(x, c, ctx, c_ctx, w_mod, b_mod, norm_mix_g, norm_ffn_g, w_in, dn_conv_w, dn_a_log, dn_dt_bias, dn_norm_g, hy_conv_w, hy_w1, hy_b1, hy_freq1, hy_w2, hy_b2, hy_freq2, hy_w3, hy_b3, hy_skip, sw_sink, w_out, ffn_w_gate, ffn_w_up, ffn_w_down, moe_router, moe_w_gate, moe_w_up, moe_w_down, final_norm_g):
    bn, seq_len, d = x.shape
    ctx_len = ctx.shape[1]
    depth = w_mod.shape[0]

    c_rows = jnp.concatenate([c, c_ctx[None, :], jnp.zeros((SUBLANES - bn - 1, d), F32)], axis=0)
    mods = _modulation(c_rows, w_mod, b_mod).reshape(depth, SUBLANES, N_MOD, d)

    rope_tabs = _rope_tables(seq_len)
    gsum = _head_sum_matrix()
    cm_x, sm_x = _dft_matrices(seq_len)
    cm_c, sm_c = _dft_matrices(ctx_len)
    hy_params = (hy_w1, hy_b1, hy_freq1, hy_w2, hy_b2, hy_freq2, hy_w3, hy_b3)
    hre_x, him_x = _hy_filters(seq_len, cm_x, sm_x, *hy_params)
    hre_c, him_c = _hy_filters(ctx_len, cm_c, sm_c, *hy_params)
    router = jnp.pad(moe_router, ((0, 0), (0, 0), (0, LANES - N_EXPERTS)))
    zero_state = jnp.zeros((bn, N_DIR, DN_HEADS, HEAD_DIM, HEAD_DIM), F32)

    for layer in range(depth):
        last = layer == depth - 1
        mod_x = [mods[layer, :bn, m][:, None, :] for m in range(N_MOD)]
        mod_c = [jnp.broadcast_to(mods[layer, bn, m][None, None, :], (bn, 1, d)) for m in range(N_MOD)]
        g_mix = norm_mix_g[layer][None, :]
        g_ffn = norm_ffn_g[layer][None, :]
        w_in_l = _relayout_w_in(w_in[layer])
        w_out_l = w_out[layer].astype(BF16)
        gn = jnp.tile(dn_norm_g[layer], DN_HEADS)[None, :]

        qkv_x, z_x, hyp_x, swq_x, swk_x, swv_x, ab_x = _in_proj(x, g_mix, mod_x[0], mod_x[1], w_in_l, rope_tabs)
        qkv_c, z_c, hyp_c, swq_c, swk_c, swv_c, ab_c = _in_proj(ctx, g_mix, mod_c[0], mod_c[1], w_in_l, None)

        dn_in_c = _dn_prep(qkv_c, ab_c, dn_conv_w[layer], dn_a_log[layer], dn_dt_bias[layer], gsum)
        dn_in_x = _dn_prep(qkv_x, ab_x, dn_conv_w[layer], dn_a_log[layer], dn_dt_bias[layer], gsum)
        of_c, ob_c, state_c = _dn_scan(*dn_in_c, zero_state)
        of_x, ob_x, _ = _dn_scan(*dn_in_x, state_c)

        hy_x = _hyena(hyp_x, hy_conv_w[layer], cm_x, sm_x, hre_x[layer], him_x[layer], hy_skip[layer])
        sw_x = _attention(swq_x, swk_x, swv_x, swk_c, swv_c, sw_sink[layer], True)
        x = _out_proj(of_x, ob_x, z_x, gn, gsum, hy_x, sw_x, x, mod_x[2], w_out_l)

        if not last:
            hy_c = _hyena(hyp_c, hy_conv_w[layer], cm_c, sm_c, hre_c[layer], him_c[layer], hy_skip[layer])
            sw_c = _attention(swq_c, None, None, swk_c, swv_c, sw_sink[layer], False)
            ctx = _out_proj(of_c, ob_c, z_c, gn, gsum, hy_c, sw_c, ctx, mod_c[2], w_out_l)

        i = layer // 2
        streams = [(x, mod_x)] if last else [(x, mod_x), (ctx, mod_c)]
        outs = []
        for s, mod in streams:
            if layer % 2 == 0:
                outs.append(_ffn(s, g_ffn, mod[3], mod[4], mod[5], ffn_w_gate[i].astype(BF16),
                                 ffn_w_up[i].astype(BF16), ffn_w_down[i].astype(BF16)))
            else:
                shape = s.shape
                if mod is mod_c:
                    s = s.reshape(1, -1, d)
                    mod = [m[:1] for m in mod]
                outs.append(_moe(s, g_ffn, mod[3], mod[4], mod[5], router[i], moe_w_gate[i].astype(BF16),
                                 moe_w_up[i].astype(BF16), moe_w_down[i].astype(BF16)).reshape(shape))
        x = outs[0]
        if not last:
            ctx = outs[1]

    return _final_norm(x, final_norm_g[None, :])
```

```python
import functools
import math

import jax
import jax.numpy as jnp
import numpy as np
from jax import lax
from jax.experimental import pallas as pl
from jax.experimental.pallas import tpu as pltpu

F32 = jnp.float32
BF16 = jnp.bfloat16

D_MODEL = 1024
DEPTH = 4
GRID_W = 64
N_MOD = 6
HEAD_DIM = 64
N_DIR = 2
RMS_EPS = 1e-6

DN_HEADS = 6
DN_WIDTH = DN_HEADS * HEAD_DIM
DN_CHUNK = 64
DN_BLOCK = 8
DN_SUB = 2

SW_Q_HEADS = 6
SW_KV_HEADS = 2
SW_GROUP = SW_Q_HEADS // SW_KV_HEADS
SW_WIDTH = SW_Q_HEADS * HEAD_DIM
SW_KV_WIDTH = SW_KV_HEADS * HEAD_DIM
SW_BLOCK = 128
ROPE_THETA = 10000.0
NEG_INF = -1e30

HY_CH = D_MODEL - DN_WIDTH - SW_WIDTH
HY_ORDER = 2
HY_BANDS = 16
HY_EMB = 1 + 2 * HY_BANDS
HY_FFN = 64
HY_FAST_DECAY = 0.3
HY_SLOW_DECAY = 1.5
HY_TARGET = 1e-2

N_EXPERTS = 8

LANES = 128
SUBLANES = 8
VMEM_LIMIT = 56 * 1024 * 1024

_IN_SIZES = (3 * DN_WIDTH, DN_WIDTH, N_DIR * DN_HEADS, N_DIR * DN_HEADS, 3 * HY_CH, SW_WIDTH, SW_KV_WIDTH, SW_KV_WIDTH)
_IN_OFF = tuple(int(v) for v in np.cumsum((0,) + _IN_SIZES))
_SEGS = (3 * DN_WIDTH, DN_WIDTH, 3 * HY_CH, SW_WIDTH, SW_KV_WIDTH, SW_KV_WIDTH, LANES)
_SEG_Q, _SEG_K = 3, 4


def _cparams(*sem):
    return pltpu.CompilerParams(dimension_semantics=sem, vmem_limit_bytes=VMEM_LIMIT)


def _dot(a, b):
    return jnp.dot(a, b, preferred_element_type=F32)


def _dot_nt(a, b):
    return lax.dot_general(a, b, (((1,), (1,)), ((), ())), preferred_element_type=F32)


def _split(x):
    hi = x.astype(BF16)
    lo = (x - hi.astype(F32)).astype(BF16)
    return hi, lo


def _dot3(a, b):
    ah, al = _split(a)
    bh, bl = _split(b)
    return _dot(ah, bh) + (_dot(al, bh) + _dot(ah, bl))


def _dot2_exact_rhs(a, b_bf16):
    ah, al = _split(a)
    return _dot(ah, b_bf16) + _dot(al, b_bf16)


def _dot2_exact_lhs(a, b_bf16):
    ah, al = _split(a)
    return _dot(ah, b_bf16) + _dot(al, b_bf16)


def _sigmoid(x):
    return 1.0 / (1.0 + jnp.exp(-x))


def _silu(x):
    return x * _sigmoid(x)


def _softplus(x):
    return jnp.maximum(x, 0.0) + jnp.log(1.0 + jnp.exp(-jnp.abs(x)))


def _rms_mod(x, g, shift, scale):
    y = x * lax.rsqrt(jnp.mean(x * x, axis=-1, keepdims=True) + RMS_EPS)
    return (y * g) * (1.0 + scale) + shift


def _mod_kernel(c_ref, w_ref, b_ref, o_ref):
    o_ref[0] = _dot3(_silu(c_ref[...]), w_ref[0]) + b_ref[0]


def _modulation(c_rows, w_mod, b_mod):
    depth, d, n = w_mod.shape
    rows = c_rows.shape[0]
    tn = 1024
    return pl.pallas_call(
        _mod_kernel,
        grid=(depth, n // tn),
        in_specs=[
            pl.BlockSpec((rows, d), lambda l, j: (0, 0)),
            pl.BlockSpec((1, d, tn), lambda l, j: (l, 0, j)),
            pl.BlockSpec((1, 1, tn), lambda l, j: (l, 0, j)),
        ],
        out_specs=pl.BlockSpec((1, rows, tn), lambda l, j: (l, 0, j)),
        out_shape=jax.ShapeDtypeStruct((depth, rows, n), F32),
        compiler_params=_cparams("parallel", "parallel"),
        name="adaln_mod",
    )(c_rows, w_mod, b_mod.reshape(depth, 1, n))


def _rope_apply(x, cos, sin_signed):
    lane = lax.broadcasted_iota(jnp.int32, x.shape, 1)
    first = (lane & 31) < 16
    partner = jnp.where(first, pltpu.roll(x, LANES - 16, 1), pltpu.roll(x, 16, 1))
    return x * cos + partner * sin_signed


def _in_proj_kernel(x_ref, g_ref, sh_ref, sc_ref, w_ref, *rest, rope):
    if rope:
        cos_ref, sin_ref = rest[:2]
        outs = rest[2:]
    else:
        outs = rest
    h = _rms_mod(x_ref[0], g_ref[...], sh_ref[0], sc_ref[0]).astype(BF16)
    off = 0
    for idx, (o_ref, n) in enumerate(zip(outs, _SEGS)):
        r = _dot(h, w_ref[:, off:off + n])
        if idx == _SEG_Q:
            r = r * (HEAD_DIM ** -0.5)
        if rope and idx in (_SEG_Q, _SEG_K):
            cos, sin = cos_ref[...], sin_ref[...]
            r = jnp.concatenate(
                [_rope_apply(r[:, c:c + LANES], cos, sin) for c in range(0, n, LANES)], axis=1)
        o_ref[0] = r
        off += n


def _in_proj(x, g, shift, scale, w, rope_tabs):
    bn, t, d = x.shape
    tm = min(512, t)
    rope = rope_tabs is not None
    in_specs = [
        pl.BlockSpec((1, tm, d), lambda b, i: (b, i, 0)),
        pl.BlockSpec((1, d), lambda b, i: (0, 0)),
        pl.BlockSpec((1, 1, d), lambda b, i: (b, 0, 0)),
        pl.BlockSpec((1, 1, d), lambda b, i: (b, 0, 0)),
        pl.BlockSpec(w.shape, lambda b, i: (0, 0)),
    ]
    args = [x, g, shift, scale, w]
    if rope:
        in_specs += [pl.BlockSpec((tm, LANES), lambda b, i: (i, 0))] * 2
        args += list(rope_tabs)
    return pl.pallas_call(
        functools.partial(_in_proj_kernel, rope=rope),
        grid=(bn, t // tm),
        in_specs=in_specs,
        out_specs=[pl.BlockSpec((1, tm, n), lambda b, i: (b, i, 0)) for n in _SEGS],
        out_shape=[jax.ShapeDtypeStruct((bn, t, n), F32) for n in _SEGS],
        compiler_params=_cparams("parallel", "parallel"),
        name="in_proj_rope" if rope else "in_proj",
    )(*args)


def _relayout_w_in(w_in):
    o = _IN_OFF
    d = w_in.shape[0]
    pad = jnp.zeros((d, LANES - 2 * N_DIR * DN_HEADS), w_in.dtype)
    cols = [w_in[:, o[0]:o[2]], w_in[:, o[4]:o[8]], w_in[:, o[2]:o[4]], pad]
    return jnp.concatenate(cols, axis=1).astype(BF16)


def _rope_tables(length):
    n = HEAD_DIM // 4
    inv = jnp.power(ROPE_THETA, -jnp.arange(n, dtype=F32) / n)
    t = jnp.arange(length)
    row = (t // GRID_W).astype(F32)[:, None] * inv[None, :]
    col = (t % GRID_W).astype(F32)[:, None] * inv[None, :]
    cos = jnp.concatenate([jnp.cos(row), jnp.cos(row), jnp.cos(col), jnp.cos(col)], axis=1)
    sin = jnp.concatenate([-jnp.sin(row), jnp.sin(row), -jnp.sin(col), jnp.sin(col)], axis=1)
    return jnp.tile(cos, (1, 2)), jnp.tile(sin, (1, 2))


def _conv3_rows(x, prev8, next8, w, first, last):
    tm = x.shape[0]
    row = lax.broadcasted_iota(jnp.int32, x.shape, 0)
    before = jnp.where(first, 0.0, prev8[SUBLANES - 1:SUBLANES, :])
    after = jnp.where(last, 0.0, next8[0:1, :])
    xm = jnp.where(row == 0, before, pltpu.roll(x, 1, 0))
    xp = jnp.where(row == tm - 1, after, pltpu.roll(x, tm - 1, 0))
    return xm * w[0:1, :] + x * w[1:2, :] + xp * w[2:3, :]


def _halo_specs(tm, t, width):
    nb8 = t // SUBLANES
    step = tm // SUBLANES
    return [
        pl.BlockSpec((1, tm, width), lambda b, i: (b, i, 0)),
        pl.BlockSpec((1, SUBLANES, width), lambda b, i: (b, jnp.maximum(i * step - 1, 0), 0)),
        pl.BlockSpec((1, SUBLANES, width), lambda b, i: (b, jnp.minimum((i + 1) * step, nb8 - 1), 0)),
    ]


def _dn_prep_kernel(x_ref, xp_ref, xn_ref, w_ref, ab_ref, al_ref, dt_ref, gsum_ref,
                    q_ref, k_ref, v_ref, g_ref, beta_ref):
    i = pl.program_id(1)
    y = _silu(_conv3_rows(x_ref[0], xp_ref[0], xn_ref[0], w_ref[...], i == 0, i == pl.num_programs(1) - 1))
    q, k, v = y[:, :DN_WIDTH], y[:, DN_WIDTH:2 * DN_WIDTH], y[:, 2 * DN_WIDTH:]
    gs = gsum_ref[...]
    q = q * lax.rsqrt(_dot2_exact_rhs(q * q, gs) + RMS_EPS) * (HEAD_DIM ** -0.5)
    k = k * lax.rsqrt(_dot2_exact_rhs(k * k, gs) + RMS_EPS)
    for h in range(DN_HEADS):
        sl = slice(h * HEAD_DIM, (h + 1) * HEAD_DIM)
        q_ref[0, h] = q[:, sl]
        k_ref[0, h] = k[:, sl]
        v_ref[0, h] = v[:, sl]
    ab = ab_ref[0]
    lane = lax.broadcasted_iota(jnp.int32, ab.shape, 1)
    nh = N_DIR * DN_HEADS
    g_ref[0] = jnp.where(lane < nh, -jnp.exp(al_ref[...]) * _softplus(ab + dt_ref[...]), 0.0)
    beta_ref[0] = jnp.where(lane < nh, _sigmoid(pltpu.roll(ab, LANES - nh, 1)), 0.0)


def _dn_prep(qkv, ab, conv_w, a_log, dt_bias, gsum):
    bn, t, width = qkv.shape
    tm = min(512, t)
    pad = LANES - N_DIR * DN_HEADS
    al = jnp.pad(a_log.reshape(1, -1), ((0, 0), (0, pad)))
    dt = jnp.pad(dt_bias.reshape(1, -1), ((0, 0), (0, pad)))
    head = pl.BlockSpec((1, DN_HEADS, tm, HEAD_DIM), lambda b, i: (b, 0, i, 0))
    row = pl.BlockSpec((1, tm, LANES), lambda b, i: (b, i, 0))
    const = lambda shape: pl.BlockSpec(shape, lambda b, i: (0,) * len(shape))
    head_shape = jax.ShapeDtypeStruct((bn, DN_HEADS, t, HEAD_DIM), F32)
    row_shape = jax.ShapeDtypeStruct((bn, t, LANES), F32)
    return pl.pallas_call(
        _dn_prep_kernel,
        grid=(bn, t // tm),
        in_specs=_halo_specs(tm, t, width) + [const(conv_w.shape), row, const(al.shape), const(dt.shape),
                                              const(gsum.shape)],
        out_specs=[head, head, head, row, row],
        out_shape=[head_shape, head_shape, head_shape, row_shape, row_shape],
        compiler_params=_cparams("parallel", "parallel"),
        name="dn_prep",
    )(qkv, qkv, qkv, conv_w, ab, al, dt, gsum)


def _cumsum_rows(x, reverse):
    c = x.shape[0]
    row = lax.broadcasted_iota(jnp.int32, x.shape, 0)
    s = 1
    while s < c:
        if reverse:
            x = x + jnp.where(row < c - s, pltpu.roll(x, c - s, 0), 0.0)
        else:
            x = x + jnp.where(row >= s, pltpu.roll(x, s, 0), 0.0)
        s *= 2
    return x


def _mm1(ps, cs):
    return [_dot(p.astype(BF16), c.astype(BF16)) for p, c in zip(ps, cs)]


def _mm3(ps, cs):
    out = []
    for p, c in zip(ps, cs):
        c16 = c.astype(BF16)
        c_lo = (c - c16.astype(F32)).astype(BF16)
        p_hi = p.astype(BF16).astype(F32)
        lhs = jnp.concatenate([p_hi, p - p_hi, p_hi], axis=1).astype(BF16)
        out.append(_dot(lhs, jnp.concatenate([c16, c16, c_lo], axis=0)))
    return out


def _dn_chunks(probs):
    c, d = DN_CHUNK, HEAD_DIM
    ii = lax.broadcasted_iota(jnp.int32, (c, c), 0)
    jj = lax.broadcasted_iota(jnp.int32, (c, c), 1)
    di = lax.broadcasted_iota(jnp.int32, (d, d), 0)
    dj = lax.broadcasted_iota(jnp.int32, (d, d), 1)
    n = len(probs)
    k16 = [pr["k"].astype(BF16) for pr in probs]
    kk = [_dot_nt(k16[i], k16[i]) for i in range(n)]
    qk = [_dot_nt(probs[i]["q"].astype(BF16), k16[i]) for i in range(n)]
    kt = [pr["k"].T for pr in probs]
    decay, e_cum, x, p = [], [], [], []
    for i, pr in enumerate(probs):
        incl = (ii >= jj) if pr["lower"] else (ii <= jj)
        strict = (ii > jj) if pr["lower"] else (ii < jj)
        dec = jnp.where(incl, jnp.exp(jnp.where(incl, pr["cum_col"] - pr["cum_row"], 0.0)), 0.0)
        ec = jnp.exp(pr["cum_col"])
        decay.append(dec)
        e_cum.append(ec)
        x.append(jnp.concatenate([pr["v"] * pr["beta_col"], pr["k"] * pr["beta_col"] * ec], axis=1))
        p.append(jnp.where(strict, -(kk[i] * pr["beta_col"] * dec), 0.0))
    same_blk = (ii // DN_BLOCK) == (jj // DN_BLOCK)
    eye = jnp.where(ii == jj, 1.0, 0.0)
    p0 = [jnp.where(same_blk, p[i], 0.0) for i in range(n)]
    a_off = [jnp.where(same_blk, 0.0, -p[i]) for i in range(n)]
    q = _mm1(p0, p0)
    xd = [eye + p0[i] for i in range(n)]
    lvl = 2
    while lvl < DN_BLOCK:
        last = 2 * lvl >= DN_BLOCK
        r = _mm1(q, xd if last else [jnp.concatenate([xd[i], q[i]], axis=1) for i in range(n)])
        xd = [xd[i] + r[i][:, :c] for i in range(n)]
        if not last:
            q = [r[i][:, c:] for i in range(n)]
        lvl *= 2
    r = _mm1(xd, [jnp.concatenate([x[i], a_off[i]], axis=1) for i in range(n)])
    x = [r[i][:, :2 * d] for i in range(n)]
    q = [-r[i][:, 2 * d:] for i in range(n)]
    lvl = 1
    while lvl < c // DN_BLOCK:
        last = 2 * lvl >= c // DN_BLOCK
        r = _mm3(q, x if last else [jnp.concatenate([x[i], q[i]], axis=1) for i in range(n)])
        x = [x[i] + r[i][:, :2 * d] for i in range(n)]
        if not last:
            q = [r[i][:, 2 * d:] for i in range(n)]
        lvl *= 2
    lhs = [jnp.concatenate([qk[i] * decay[i], kt[i] * jnp.exp(probs[i]["tot"] - probs[i]["cum_row"])], axis=0)
           for i in range(n)]
    r = [_dot(lhs[i].astype(BF16), x[i].astype(BF16)) for i in range(n)]
    lhs = [jnp.concatenate([jnp.where(di == dj, jnp.exp(probs[i]["tot"]), 0.0) - r[i][c:, d:],
                            probs[i]["q"] * e_cum[i] - r[i][:c, d:]], axis=0).astype(BF16) for i in range(n)]
    return [(lhs[i], r[i][c:, :d], r[i][:c, :d]) for i in range(n)]


def _dn_scan_kernel(qf_ref, kf_ref, vf_ref, gf_ref, bf_ref, qb_ref, kb_ref, vb_ref, gb_ref, bb_ref,
                    s0_ref, of_ref, ob_ref, sfin_ref, s_ref):
    t = pl.program_id(1)
    c, d = DN_CHUNK, HEAD_DIM

    @pl.when(t == 0)
    def _():
        s_ref[...] = s0_ref[0]

    probs = []
    for sub in range(DN_SUB):
        for d_idx, (q_ref, k_ref, v_ref, g_ref, b_ref) in enumerate(
                ((qf_ref, kf_ref, vf_ref, gf_ref, bf_ref), (qb_ref, kb_ref, vb_ref, gb_ref, bb_ref))):
            lower = d_idx == 0
            rows = pl.ds((sub if lower else DN_SUB - 1 - sub) * c, c)
            cum = _cumsum_rows(g_ref[0, rows, :], reverse=not lower)
            cum_t = cum.T
            beta = b_ref[0, rows, :]
            last = c - 1 if lower else 0
            for h in range(DN_HEADS):
                col = d_idx * DN_HEADS + h
                cum_row = cum_t[col:col + 1, :]
                probs.append(dict(q=q_ref[0, h, rows, :], k=k_ref[0, h, rows, :], v=v_ref[0, h, rows, :],
                                  lower=lower, cum_col=cum[:, col:col + 1], cum_row=cum_row,
                                  tot=cum_row[:, last:last + 1], beta_col=beta[:, col:col + 1]))
    local = _dn_chunks(probs)

    state = [s_ref[d_idx, h] for d_idx in range(N_DIR) for h in range(DN_HEADS)]
    for sub in range(DN_SUB):
        base = sub * N_DIR * DN_HEADS
        r = [_dot(local[base + i][0], state[i].astype(BF16)) for i in range(len(state))]
        state = [r[i][:d] + local[base + i][1] for i in range(len(state))]
        for d_idx, o_ref in enumerate((of_ref, ob_ref)):
            row0 = (sub if d_idx == 0 else DN_SUB - 1 - sub) * c
            o_ref[0, row0:row0 + c, :] = jnp.concatenate(
                [r[d_idx * DN_HEADS + h][d:] + local[base + d_idx * DN_HEADS + h][2] for h in range(DN_HEADS)],
                axis=1)
    for d_idx in range(N_DIR):
        for h in range(DN_HEADS):
            s_ref[d_idx, h] = state[d_idx * DN_HEADS + h]

    @pl.when(t == pl.num_programs(1) - 1)
    def _():
        sfin_ref[0] = s_ref[...]


def _dn_scan(q, k, v, g, beta, s0):
    bn, nh, t, d = q.shape
    rows = DN_SUB * DN_CHUNK
    n = t // rows
    head_f = pl.BlockSpec((1, nh, rows, d), lambda b, i: (b, 0, i, 0))
    head_b = pl.BlockSpec((1, nh, rows, d), lambda b, i: (b, 0, n - 1 - i, 0))
    row_f = pl.BlockSpec((1, rows, LANES), lambda b, i: (b, i, 0))
    row_b = pl.BlockSpec((1, rows, LANES), lambda b, i: (b, n - 1 - i, 0))
    state = pl.BlockSpec((1, N_DIR, nh, d, d), lambda b, i: (b, 0, 0, 0, 0))
    out_f = pl.BlockSpec((1, rows, nh * d), lambda b, i: (b, i, 0))
    out_b = pl.BlockSpec((1, rows, nh * d), lambda b, i: (b, n - 1 - i, 0))
    o_shape = jax.ShapeDtypeStruct((bn, t, nh * d), F32)
    return pl.pallas_call(
        _dn_scan_kernel,
        grid=(bn, n),
        in_specs=[head_f, head_f, head_f, row_f, row_f, head_b, head_b, head_b, row_b, row_b, state],
        out_specs=[out_f, out_b, state],
        out_shape=[o_shape, o_shape, jax.ShapeDtypeStruct(s0.shape, F32)],
        scratch_shapes=[pltpu.VMEM((N_DIR, nh, d, d), F32)],
        compiler_params=_cparams("parallel", "arbitrary"),
        name="dn_scan",
    )(q, k, v, g, beta, q, k, v, g, beta, s0)


def _hy_prep_kernel(x_ref, xp_ref, xn_ref, w_ref, v_ref, v16_ref, x1_ref, x2_ref):
    i = pl.program_id(1)
    y = _conv3_rows(x_ref[0], xp_ref[0], xn_ref[0], w_ref[...], i == 0, i == pl.num_programs(1) - 1)
    v = y[:, :HY_CH]
    v_ref[...] = v
    v16_ref[...] = v.astype(BF16)
    x1_ref[...] = y[:, HY_CH:2 * HY_CH]
    x2_ref[...] = y[:, 2 * HY_CH:]


def _hy_prep(z, conv_w):
    bn, t, width = z.shape
    tm = min(512, t)
    out = pl.BlockSpec((tm, HY_CH), lambda b, i: (i, b))
    f32 = jax.ShapeDtypeStruct((t, bn * HY_CH), F32)
    return pl.pallas_call(
        _hy_prep_kernel,
        grid=(bn, t // tm),
        in_specs=_halo_specs(tm, t, width) + [pl.BlockSpec(conv_w.shape, lambda b, i: (0, 0))],
        out_specs=[out, out, out, out],
        out_shape=[f32, jax.ShapeDtypeStruct((t, bn * HY_CH), BF16), f32, f32],
        compiler_params=_cparams("parallel", "parallel"),
        name="hy_prep",
    )(z, z, z, conv_w)


def _dft_matrices(length):
    n = 2 * length
    k = jnp.arange(length, dtype=jnp.int32)
    ang = ((k[:, None] * k[None, :]) % n).astype(F32) * (2.0 * math.pi / n)
    alt = (1 - 2 * (k % 2)).astype(F32)
    sin = jnp.where(k[:, None] == 0, alt[None, :], jnp.sin(ang))
    return jnp.cos(ang).astype(BF16), sin.astype(BF16)


def _hy_filter_kernel(f_ref, w1_ref, b1_ref, f1_ref, w2_ref, b2_ref, f2_ref, w3_ref, b3_ref, dl_ref,
                      p_ref, q_ref, ssq_ref, nyq_ref, taps_ref):
    i = pl.program_id(1)
    feats = f_ref[...]
    h = jnp.sin(f1_ref[0] * (_dot3(feats, w1_ref[0]) + b1_ref[0]))
    h = jnp.sin(f2_ref[0] * (_dot3(h, w2_ref[0]) + b2_ref[0]))
    h = _dot3(h, w3_ref[0]) + b3_ref[0]
    win = jnp.exp(-feats[:, 0:1] * dl_ref[...])
    half = HY_ORDER * HY_CH
    fwd = h[:, :half] * jnp.concatenate([win] * HY_ORDER, axis=1)
    bwd = h[:, half:] * jnp.concatenate([win] * HY_ORDER, axis=1)
    row = lax.broadcasted_iota(jnp.int32, bwd.shape, 0)
    bwd = jnp.where((row == 0) & (i == 0), 0.0, bwd)
    p = fwd + bwd
    p_ref[0] = p.astype(BF16)
    q_ref[0] = (bwd - fwd).astype(BF16)
    taps_ref[0] = jnp.concatenate([fwd, bwd], axis=1).astype(BF16)
    alt = (1 - 2 * (row & 1)).astype(F32)

    @pl.when(i == 0)
    def _():
        ssq_ref[0] = jnp.zeros_like(ssq_ref[0])
        nyq_ref[0] = jnp.zeros_like(nyq_ref[0])

    ssq_ref[0] += jnp.sum(fwd * fwd + bwd * bwd, axis=0, keepdims=True)
    nyq_ref[0] += jnp.sum(p * alt, axis=0, keepdims=True)


def _hy_spec_kernel(c_ref, s_ref, p_ref, q_ref, ssq_ref, nyq_ref, hre_ref, him_ref):
    i = pl.program_id(1)
    scale = lax.rsqrt(ssq_ref[0] + RMS_EPS)
    hre = _dot(c_ref[...], p_ref[0]) * scale
    him = _dot(s_ref[...], q_ref[0]) * scale
    row = lax.broadcasted_iota(jnp.int32, him.shape, 0)
    hre_ref[0] = hre
    him_ref[0] = jnp.where((row == 0) & (i == 0), nyq_ref[0] * scale, him)


def _hy_taps(length, w1, b1, f1, w2, b2, f2, w3, b3):
    depth = w1.shape[0]
    t = jnp.linspace(0.0, 1.0, length, dtype=F32)[:, None]
    omega = 2.0 * math.pi * jnp.arange(length, dtype=F32) / length
    bands = jnp.linspace(1e-4, HY_BANDS - 1, HY_BANDS, dtype=F32)
    ang = omega[:, None] * bands[None, :]
    feats = jnp.concatenate([t, jnp.cos(ang), -jnp.sin(ang), jnp.zeros((length, LANES - HY_EMB), F32)], axis=-1)
    w1p = jnp.pad(w1, ((0, 0), (0, LANES - HY_EMB), (0, 0)))
    max_decay = math.log(HY_TARGET) / HY_FAST_DECAY
    min_decay = math.log(HY_TARGET) / HY_SLOW_DECAY
    deltas = jnp.abs(jnp.linspace(min_decay, max_decay, HY_CH, dtype=F32))[None, :]
    half = HY_ORDER * HY_CH
    tm = min(512, length)
    vec = lambda a: a.reshape(depth, 1, -1)
    lay = lambda shape: pl.BlockSpec((1,) + shape, lambda l, i: (l,) + (0,) * len(shape))
    rows = lambda n: pl.BlockSpec((1, tm, n), lambda l, i: (l, i, 0))
    return pl.pallas_call(
        _hy_filter_kernel,
        grid=(depth, length // tm),
        in_specs=[pl.BlockSpec((tm, LANES), lambda l, i: (i, 0)),
                  lay((LANES, HY_FFN)), lay((1, HY_FFN)), lay((1, HY_FFN)),
                  lay((HY_FFN, HY_FFN)), lay((1, HY_FFN)), lay((1, HY_FFN)),
                  lay((HY_FFN, 2 * half)), lay((1, 2 * half)),
                  pl.BlockSpec((1, HY_CH), lambda l, i: (0, 0))],
        out_specs=[rows(half), rows(half), lay((1, half)), lay((1, half)), rows(2 * half)],
        out_shape=[jax.ShapeDtypeStruct((depth, length, half), BF16),
                   jax.ShapeDtypeStruct((depth, length, half), BF16),
                   jax.ShapeDtypeStruct((depth, 1, half), F32),
                   jax.ShapeDtypeStruct((depth, 1, half), F32),
                   jax.ShapeDtypeStruct((depth, length, 2 * half), BF16)],
        compiler_params=_cparams("parallel", "arbitrary"),
        name="hy_filter",
    )(feats, w1p, vec(b1), vec(f1), w2, vec(b2), vec(f2), w3, vec(b3), deltas)


def _hy_spec_dense(cmat, smat, p, q, ssq, nyq):
    depth, length, half = p.shape
    lay = lambda shape: pl.BlockSpec((1,) + shape, lambda l, i: (l,) + (0,) * len(shape))
    tk = min(512, length)
    return pl.pallas_call(
        _hy_spec_kernel,
        grid=(depth, length // tk),
        in_specs=[pl.BlockSpec((tk, length), lambda l, i: (i, 0)),
                  pl.BlockSpec((tk, length), lambda l, i: (i, 0)),
                  lay((length, half)), lay((length, half)), lay((1, half)), lay((1, half))],
        out_specs=[pl.BlockSpec((1, tk, half), lambda l, i: (l, i, 0))] * 2,
        out_shape=[jax.ShapeDtypeStruct((depth, length, half), F32)] * 2,
        compiler_params=_cparams("parallel", "parallel"),
        name="hy_spec",
    )(cmat, smat, p, q, ssq, nyq)


def _hy_fwd_kernel(c_ref, s_ref, u_ref, hre_ref, him_ref, yre_ref, yim_ref, *, n_points, n_batch):
    i = pl.program_id(0)
    ure = _dot(c_ref[...], u_ref[...])
    usn = _dot(s_ref[...], u_ref[...])
    hre = jnp.concatenate([hre_ref[...]] * n_batch, axis=1)
    him = jnp.concatenate([him_ref[...]] * n_batch, axis=1)
    row0 = (lax.broadcasted_iota(jnp.int32, ure.shape, 0) == 0) & (i == 0)
    yre = jnp.where(row0, ure * hre * (1.0 / n_points), (ure * hre + usn * him) * (2.0 / n_points))
    yim = jnp.where(row0, usn * him * (1.0 / n_points), (usn * hre - ure * him) * (2.0 / n_points))
    yre_ref[...] = yre.astype(BF16)
    yim_ref[...] = yim.astype(BF16)


def _hy_inv_kernel(c_ref, s_ref, yre_ref, yim_ref, u_ref, gate_ref, skip_ref, o_ref, *o16_ref, n_batch):
    i = pl.program_id(0)
    tm = c_ref.shape[0]
    conv_c = _dot(c_ref[...], yre_ref[...])
    conv_s = _dot(s_ref[...], yim_ref[...])
    trow = lax.broadcasted_iota(jnp.int32, conv_c.shape, 0) + i * tm
    alt = (1 - 2 * (trow & 1)).astype(F32)
    nyq = yim_ref[0:1, :].astype(F32)
    conv = conv_c + jnp.where(trow == 0, 0.0, conv_s) + alt * nyq
    skip = jnp.concatenate([skip_ref[...]] * n_batch, axis=1)
    y = gate_ref[...] * (conv + skip * u_ref[...])
    o_ref[...] = y
    if o16_ref:
        o16_ref[0][...] = y.astype(BF16)


def _hy_long_conv(cmat, smat, u16, u, gate, hre, him, skip, order, emit_bf16):
    length, cols = u.shape
    n_batch = cols // HY_CH
    tk = min(512, length)
    full = pl.BlockSpec((length, cols), lambda i: (0, 0))
    mat = pl.BlockSpec((tk, length), lambda i: (i, 0))
    tile = pl.BlockSpec((tk, cols), lambda i: (i, 0))
    spec = pl.BlockSpec((tk, HY_CH), lambda i: (i, order))
    yre, yim = pl.pallas_call(
        functools.partial(_hy_fwd_kernel, n_points=2 * length, n_batch=n_batch),
        grid=(length // tk,),
        in_specs=[mat, mat, full, spec, spec],
        out_specs=[tile, tile],
        out_shape=[jax.ShapeDtypeStruct((length, cols), BF16)] * 2,
        compiler_params=_cparams("parallel"),
        name="hy_fwd",
    )(cmat, smat, u16, hre, him)
    tm = min(256, length)
    mat = pl.BlockSpec((tm, length), lambda i: (i, 0))
    tile = pl.BlockSpec((tm, cols), lambda i: (i, 0))
    out_shape = [jax.ShapeDtypeStruct((length, cols), F32)]
    if emit_bf16:
        out_shape.append(jax.ShapeDtypeStruct((length, cols), BF16))
    return pl.pallas_call(
        functools.partial(_hy_inv_kernel, n_batch=n_batch),
        grid=(length // tm,),
        in_specs=[mat, mat, full, full, tile, tile, pl.BlockSpec((1, HY_CH), lambda i: (0, 0))],
        out_specs=[tile] * len(out_shape),
        out_shape=out_shape,
        compiler_params=_cparams("parallel"),
        name="hy_inv",
    )(cmat, smat, yre, yim, u, gate, skip[order][None, :])


def _hyena(z, conv_w, cmat, smat, hre, him, skip):
    v, v16, x1, x2 = _hy_prep(z, conv_w)
    y1, y16 = _hy_long_conv(cmat, smat, v16, v, x1, hre, him, skip, 0, True)
    (y,) = _hy_long_conv(cmat, smat, y16, y1, x2, hre, him, skip, 1, False)
    return y


FFT_N2 = 256
FFT_LANES = 8192


def _fft_consts(length):
    n = 2 * length
    n1 = n // FFT_N2
    nb = length // FFT_N2
    r = n1 // 2 + 1
    rp = -(-2 * r // SUBLANES) * SUBLANES
    rq = -(-r // SUBLANES) * SUBLANES
    k1 = np.arange(r)[:, None]
    ang1 = 2 * np.pi * ((k1 * np.arange(nb)[None, :]) % n1) / n1
    a_fwd = np.zeros((rp, nb))
    a_fwd[:r], a_fwd[r:2 * r] = np.cos(ang1), -np.sin(ang1)
    wgt = np.where((k1 == 0) | (k1 == n1 // 2), 1.0, 2.0) / n
    a_inv_re, a_inv_im = np.zeros((nb, rq)), np.zeros((nb, rq))
    a_inv_re[:, :r], a_inv_im[:, :r] = (np.cos(ang1) * wgt).T, (-np.sin(ang1) * wgt).T
    kk = np.arange(r)[:, None, None] + n1 * np.arange(FFT_N2)[None, :, None]
    ang2 = 2 * np.pi * ((kk * np.arange(FFT_N2)[None, None, :]) % n) / n
    gc, gs = np.cos(ang2), np.sin(ang2)
    f32 = lambda a: jnp.asarray(a, F32)
    b16 = lambda a: jnp.asarray(a, F32).astype(BF16)
    return dict(r=r, rp=rp, rq=rq, nb=nb, a_fwd=f32(a_fwd), a_inv_re=f32(a_inv_re), a_inv_im=f32(a_inv_im),
                gc=b16(gc), gs=b16(gs), gct=b16(gc.transpose(0, 2, 1)), gst=b16(gs.transpose(0, 2, 1)))


def _fft_s1f_kernel(a_ref, v_ref, o_ref):
    v = v_ref[0]
    o_ref[0] = _dot2_exact_lhs(a_ref[...], v).astype(BF16)


def _fft_stage1_fwd(a_fwd, v16):
    g, nb, m = v16.shape
    rp = a_fwd.shape[0]
    tl = FFT_LANES
    return pl.pallas_call(
        _fft_s1f_kernel,
        grid=(g, m // tl),
        in_specs=[pl.BlockSpec(a_fwd.shape, lambda l, i: (0, 0)),
                  pl.BlockSpec((1, nb, tl), lambda l, i: (l, 0, i))],
        out_specs=pl.BlockSpec((1, rp, tl), lambda l, i: (l, 0, i)),
        out_shape=jax.ShapeDtypeStruct((g, rp, m), BF16),
        compiler_params=_cparams("parallel", "parallel"),
        name="fft_s1_fwd",
    )(a_fwd, v16)


def _fft_s2f_kernel(ar_ref, ai_ref, gc_ref, gs_ref, *rest, spectrum, n_batch):
    ar, ai, gc, gs = ar_ref[0, 0], ai_ref[0, 0], gc_ref[0], gs_ref[0]
    xr = _dot(gc, ar) + _dot(gs, ai)
    xi = _dot(gc, ai) - _dot(gs, ar)
    if spectrum:
        ssq_ref, hr_ref, hi_ref = rest
        half = xr.shape[1] // 2
        scale = lax.rsqrt(ssq_ref[0] + RMS_EPS)
        hr_ref[0, 0] = (xr[:, :half] + xr[:, half:]) * scale
        hi_ref[0, 0] = (xi[:, :half] - xi[:, half:]) * scale
    else:
        hr_ref, hi_ref, yr_ref, yi_ref = rest
        hr = jnp.concatenate([hr_ref[0]] * n_batch, axis=1)
        hi = jnp.concatenate([hi_ref[0]] * n_batch, axis=1)
        yr_ref[0] = (xr * hr - xi * hi).astype(BF16)
        yi_ref[0] = (xr * hi + xi * hr).astype(BF16)


def _fft_s2i_kernel(yr_ref, yi_ref, gct_ref, gst_ref, br_ref, bi_ref, *, r):
    k = pl.program_id(0)

    @pl.when(k < r)
    def _():
        yr, yi, gct, gst = yr_ref[0], yi_ref[0], gct_ref[0], gst_ref[0]
        br_ref[0] = (_dot(gct, yr) - _dot(gst, yi)).astype(BF16)
        bi_ref[0] = (_dot(gct, yi) + _dot(gst, yr)).astype(BF16)

    @pl.when(k >= r)
    def _():
        br_ref[...] = jnp.zeros_like(br_ref)
        bi_ref[...] = jnp.zeros_like(bi_ref)


def _fft_s1i_kernel(are_ref, aim_ref, br_ref, bi_ref, u_ref, gate_ref, skip_ref, o_ref, *o16_ref):
    conv = _dot2_exact_lhs(are_ref[...], br_ref[...]) + _dot2_exact_lhs(aim_ref[...], bi_ref[...])
    y = gate_ref[...] * (conv + skip_ref[...] * u_ref[...])
    o_ref[...] = y
    if o16_ref:
        o16_ref[0][...] = y.astype(BF16)


def _hy_spec_fft(fc, taps, ssq):
    depth, length, cols = taps.shape
    r, rp, nb = fc["r"], fc["rp"], fc["nb"]
    half = cols // 2
    a = _fft_stage1_fwd(fc["a_fwd"], taps.reshape(depth, nb, FFT_N2 * cols)).reshape(depth, rp, FFT_N2, cols)
    res = lambda off: pl.BlockSpec((1, 1, FFT_N2, cols), lambda l, k: (l, k + off, 0, 0))
    g = pl.BlockSpec((1, FFT_N2, FFT_N2), lambda l, k: (k, 0, 0))
    out = pl.BlockSpec((1, 1, FFT_N2, half), lambda l, k: (l, k, 0, 0))
    return pl.pallas_call(
        functools.partial(_fft_s2f_kernel, spectrum=True, n_batch=1),
        grid=(depth, r),
        in_specs=[res(0), res(r), g, g, pl.BlockSpec((1, 1, half), lambda l, k: (l, 0, 0))],
        out_specs=[out, out],
        out_shape=[jax.ShapeDtypeStruct((depth, r, FFT_N2, half), F32)] * 2,
        compiler_params=_cparams("parallel", "parallel"),
        name="fft_spec",
    )(a, a, fc["gc"], fc["gs"], ssq)


def _hy_long_conv_fft(fc, u16, u, gate, hre, him, skip, order, emit_bf16):
    length, cols = u.shape
    n_batch = cols // HY_CH
    r, rp, rq, nb = fc["r"], fc["rp"], fc["rq"], fc["nb"]
    m = FFT_N2 * cols
    a = _fft_stage1_fwd(fc["a_fwd"], u16.reshape(1, nb, m)).reshape(1, rp, FFT_N2, cols)
    res = lambda off: pl.BlockSpec((1, 1, FFT_N2, cols), lambda k: (0, k + off, 0, 0))
    g = pl.BlockSpec((1, FFT_N2, FFT_N2), lambda k: (k, 0, 0))
    spec = pl.BlockSpec((1, FFT_N2, HY_CH), lambda k: (k, 0, order))
    blk = pl.BlockSpec((1, FFT_N2, cols), lambda k: (k, 0, 0))
    yr, yi = pl.pallas_call(
        functools.partial(_fft_s2f_kernel, spectrum=False, n_batch=n_batch),
        grid=(r,),
        in_specs=[res(0), res(r), g, g, spec, spec],
        out_specs=[blk, blk],
        out_shape=[jax.ShapeDtypeStruct((r, FFT_N2, cols), BF16)] * 2,
        compiler_params=_cparams("parallel"),
        name="fft_s2_fwd",
    )(a, a, fc["gc"], fc["gs"], hre, him)
    clamp = lambda k: jnp.minimum(k, r - 1)
    blk_in = pl.BlockSpec((1, FFT_N2, cols), lambda k: (clamp(k), 0, 0))
    g_in = pl.BlockSpec((1, FFT_N2, FFT_N2), lambda k: (clamp(k), 0, 0))
    br, bi = pl.pallas_call(
        functools.partial(_fft_s2i_kernel, r=r),
        grid=(rq,),
        in_specs=[blk_in, blk_in, g_in, g_in],
        out_specs=[blk, blk],
        out_shape=[jax.ShapeDtypeStruct((rq, FFT_N2, cols), BF16)] * 2,
        compiler_params=_cparams("parallel"),
        name="fft_s2_inv",
    )(yr, yi, fc["gct"], fc["gst"])
    tl = FFT_LANES
    flat = lambda n: pl.BlockSpec((n, tl), lambda i: (0, i))
    const = lambda shape: pl.BlockSpec(shape, lambda i: (0, 0))
    skip_row = jnp.tile(skip[order], tl // HY_CH)[None, :]
    out_shape = [jax.ShapeDtypeStruct((nb, m), F32)]
    if emit_bf16:
        out_shape.append(jax.ShapeDtypeStruct((nb, m), BF16))
    outs = pl.pallas_call(
        _fft_s1i_kernel,
        grid=(m // tl,),
        in_specs=[const((nb, rq)), const((nb, rq)), flat(rq), flat(rq), flat(nb), flat(nb), const((1, tl))],
        out_specs=[flat(nb)] * len(out_shape),
        out_shape=out_shape,
        compiler_params=_cparams("parallel"),
        name="fft_s1_inv",
    )(fc["a_inv_re"], fc["a_inv_im"], br.reshape(rq, m), bi.reshape(rq, m), u.reshape(nb, m),
      gate.reshape(nb, m), skip_row)
    return [o.reshape(length, cols) for o in outs]


def _hyena_fft(z, conv_w, fc, hre, him, skip):
    v, v16, x1, x2 = _hy_prep(z, conv_w)
    y1, y16 = _hy_long_conv_fft(fc, v16, v, x1, hre, him, skip, 0, True)
    (y,) = _hy_long_conv_fft(fc, y16, y1, x2, hre, him, skip, 1, False)
    return y


def _attn_kernel(sink_ref, q_ref, *rest, local, n_blocks):
    if local:
        kp_ref, kc_ref, kn_ref, vp_ref, vc_ref, vn_ref, kx_ref, vx_ref, o_ref = rest
        k_all = jnp.concatenate([kp_ref[0], kc_ref[0], kn_ref[0], kx_ref[0]], axis=0)
        v_all = jnp.concatenate([vp_ref[0], vc_ref[0], vn_ref[0], vx_ref[0]], axis=0)
    else:
        kx_ref, vx_ref, o_ref = rest
        k_all, v_all = kx_ref[0], vx_ref[0]
    blk = pl.program_id(1)
    q = q_ref[0]
    tq = q.shape[0]
    nk = k_all.shape[0]
    k16, v16 = k_all.astype(BF16), v_all.astype(BF16)
    lane = lax.broadcasted_iota(jnp.int32, (tq, LANES), 1)
    rows = SW_GROUP * tq
    r_idx = lax.broadcasted_iota(jnp.int32, (rows, nk), 0)
    if local:
        r = r_idx & (tq - 1)
        c = lax.broadcasted_iota(jnp.int32, (rows, nk), 1)
        rel = c - r - SW_BLOCK
        valid = (rel >= -SW_BLOCK) & (rel <= SW_BLOCK)
        valid = valid & ((c >= SW_BLOCK) | (blk > 0)) & ((c < 2 * SW_BLOCK) | (blk < n_blocks - 1))
        valid = valid | (c >= 3 * SW_BLOCK)
    out_heads = [None] * SW_Q_HEADS
    for j in range(SW_KV_HEADS):
        keep = (lane >= j * HEAD_DIM) & (lane < (j + 1) * HEAD_DIM)
        parts = []
        for g in range(SW_GROUP):
            hq = j * SW_GROUP + g
            chunk = q[:, (hq // 2) * LANES:(hq // 2 + 1) * LANES]
            if hq % 2 != j:
                chunk = pltpu.roll(chunk, HEAD_DIM, 1)
            parts.append(jnp.where(keep, chunk, 0.0))
        qz = jnp.concatenate(parts, axis=0).astype(BF16)
        s = _dot_nt(qz, k16)
        if local:
            s = jnp.where(valid, s, NEG_INF)
        sink = jnp.where(r_idx[:, 0:1] < tq, sink_ref[j * SW_GROUP],
                         jnp.where(r_idx[:, 0:1] < 2 * tq, sink_ref[j * SW_GROUP + 1], sink_ref[j * SW_GROUP + 2]))
        m = jnp.maximum(jnp.max(s, axis=-1, keepdims=True), sink)
        p = jnp.exp(s - m)
        denom = jnp.sum(p, axis=-1, keepdims=True) + jnp.exp(sink - m)
        o = _dot(p.astype(BF16), v16) / denom
        for g in range(SW_GROUP):
            hq = j * SW_GROUP + g
            og = o[g * tq:(g + 1) * tq]
            if hq % 2 != j:
                og = pltpu.roll(og, HEAD_DIM, 1)
            out_heads[hq] = og
    first_half = lane < HEAD_DIM
    o_ref[0] = jnp.concatenate(
        [jnp.where(first_half, out_heads[2 * c], out_heads[2 * c + 1]) for c in range(SW_Q_HEADS // 2)], axis=1)


def _attention(q, k, v, kx, vx, sink, local):
    bn, t, _ = q.shape
    nb = t // SW_BLOCK
    tx = kx.shape[1]
    qspec = pl.BlockSpec((1, SW_BLOCK, SW_WIDTH), lambda b, i: (b, i, 0))
    xspec = pl.BlockSpec((1, tx, SW_KV_WIDTH), lambda b, i: (b, 0, 0))
    in_specs = [pl.BlockSpec(memory_space=pltpu.SMEM), qspec]
    args = [sink, q]
    if local:
        prev = pl.BlockSpec((1, SW_BLOCK, SW_KV_WIDTH), lambda b, i: (b, jnp.maximum(i - 1, 0), 0))
        cur = pl.BlockSpec((1, SW_BLOCK, SW_KV_WIDTH), lambda b, i: (b, i, 0))
        nxt = pl.BlockSpec((1, SW_BLOCK, SW_KV_WIDTH), lambda b, i: (b, jnp.minimum(i + 1, nb - 1), 0))
        in_specs += [prev, cur, nxt, prev, cur, nxt]
        args += [k, k, k, v, v, v]
    in_specs += [xspec, xspec]
    args += [kx, vx]
    return pl.pallas_call(
        functools.partial(_attn_kernel, local=local, n_blocks=nb),
        grid=(bn, nb),
        in_specs=in_specs,
        out_specs=qspec,
        out_shape=jax.ShapeDtypeStruct((bn, t, SW_WIDTH), F32),
        compiler_params=_cparams("parallel", "parallel"),
        name="attn_local" if local else "attn_ctx",
    )(*args)


def _out_proj_kernel(of_ref, ob_ref, z_ref, gn_ref, gsum_ref, hy_ref, sw_ref, x_ref, gate_ref, w_ref, o_ref):
    o = of_ref[0] + ob_ref[0]
    mean = _dot2_exact_rhs(o * o, gsum_ref[...]) * (1.0 / HEAD_DIM)
    dn = o * lax.rsqrt(mean + RMS_EPS) * gn_ref[...] * _silu(z_ref[0])
    a, b = DN_WIDTH, DN_WIDTH + HY_CH
    mix = _dot(dn.astype(BF16), w_ref[0:a, :])
    mix += _dot(hy_ref[...].astype(BF16), w_ref[a:b, :])
    mix += _dot(sw_ref[0].astype(BF16), w_ref[b:, :])
    o_ref[0] = x_ref[0] + gate_ref[0] * mix


def _out_proj(o_f, o_b, z, gn, gsum, hy, sw, x, gate, w):
    bn, t, d = x.shape
    tm = min(512, t)
    seq = lambda n: pl.BlockSpec((1, tm, n), lambda b, i: (b, i, 0))
    const = lambda shape: pl.BlockSpec(shape, lambda b, i: (0,) * len(shape))
    return pl.pallas_call(
        _out_proj_kernel,
        grid=(bn, t // tm),
        in_specs=[seq(DN_WIDTH), seq(DN_WIDTH), seq(DN_WIDTH), const(gn.shape), const(gsum.shape),
                  pl.BlockSpec((tm, HY_CH), lambda b, i: (i, b)), seq(SW_WIDTH), seq(d),
                  pl.BlockSpec((1, 1, d), lambda b, i: (b, 0, 0)), const(w.shape)],
        out_specs=seq(d),
        out_shape=jax.ShapeDtypeStruct(x.shape, F32),
        compiler_params=_cparams("parallel", "parallel"),
        name="out_proj",
    )(o_f, o_b, z, gn, gsum, hy, sw, x, gate, w)


def _ffn_kernel(x_ref, g_ref, sh_ref, sc_ref, gate_ref, wg_ref, wu_ref, wd_ref, o_ref, h_ref, acc_ref):
    j = pl.program_id(2)

    @pl.when(j == 0)
    def _():
        h_ref[...] = _rms_mod(x_ref[0], g_ref[...], sh_ref[0], sc_ref[0]).astype(BF16)
        acc_ref[...] = jnp.zeros_like(acc_ref)

    h = h_ref[...]
    act = _silu(_dot(h, wg_ref[...])) * _dot(h, wu_ref[...])
    acc_ref[...] += _dot(act.astype(BF16), wd_ref[...])

    @pl.when(j == pl.num_programs(2) - 1)
    def _():
        o_ref[0] = x_ref[0] + gate_ref[0] * acc_ref[...]


def _ffn(x, g, shift, scale, gate, wg, wu, wd):
    bn, t, d = x.shape
    ff = wg.shape[1]
    tm = min(512, t)
    tf = ff // 2
    seq = pl.BlockSpec((1, tm, d), lambda b, i, j: (b, i, 0))
    mod = pl.BlockSpec((1, 1, d), lambda b, i, j: (b, 0, 0))
    return pl.pallas_call(
        _ffn_kernel,
        grid=(bn, t // tm, ff // tf),
        in_specs=[seq, pl.BlockSpec((1, d), lambda b, i, j: (0, 0)), mod, mod, mod,
                  pl.BlockSpec((d, tf), lambda b, i, j: (0, j)),
                  pl.BlockSpec((d, tf), lambda b, i, j: (0, j)),
                  pl.BlockSpec((tf, d), lambda b, i, j: (j, 0))],
        out_specs=seq,
        out_shape=jax.ShapeDtypeStruct(x.shape, F32),
        scratch_shapes=[pltpu.VMEM((tm, d), BF16), pltpu.VMEM((tm, d), F32)],
        compiler_params=_cparams("parallel", "parallel", "arbitrary"),
        name="ffn_dense",
    )(x, g, shift, scale, gate, wg, wu, wd)


MOE_TILE = 512
_INFO_E, _INFO_W, _INFO_RANK = 0, 2, 4


def _route_kernel(x_ref, g_ref, sh_ref, sc_ref, r_ref, h_ref, info_ref, cnt_ref, carry_ref):
    @pl.when((pl.program_id(0) == 0) & (pl.program_id(1) == 0))
    def _():
        carry_ref[...] = jnp.zeros_like(carry_ref)

    h = _rms_mod(x_ref[0], g_ref[...], sh_ref[0], sc_ref[0])
    h_ref[0] = h
    logits = _dot3(h, r_ref[...])
    tm = logits.shape[0]
    lane = lax.broadcasted_iota(jnp.int32, logits.shape, 1)
    lg = jnp.where(lane < N_EXPERTS, logits, NEG_INF)
    m1 = jnp.max(lg, axis=-1, keepdims=True)
    i1 = jnp.min(jnp.where(lg == m1, lane, LANES), axis=-1, keepdims=True)
    lg2 = jnp.where(lane == i1, NEG_INF, lg)
    m2 = jnp.max(lg2, axis=-1, keepdims=True)
    i2 = jnp.min(jnp.where(lg2 == m2, lane, LANES), axis=-1, keepdims=True)
    e2 = jnp.exp(m2 - m1)
    w1 = 1.0 / (1.0 + e2)
    w2 = e2 * w1
    chosen = jnp.where((lane == i1) | (lane == i2), 1.0, 0.0)
    rr = lax.broadcasted_iota(jnp.int32, (tm, tm), 0)
    cc = lax.broadcasted_iota(jnp.int32, (tm, tm), 1)
    earlier = jnp.where(rr > cc, 1.0, 0.0).astype(BF16)
    before = _dot(earlier, chosen.astype(BF16)) + carry_ref[...]
    rank1 = jnp.sum(jnp.where(lane == i1, before, 0.0), axis=-1, keepdims=True)
    rank2 = jnp.sum(jnp.where(lane == i2, before, 0.0), axis=-1, keepdims=True)
    info = jnp.zeros_like(logits)
    for pos, val in ((_INFO_E, i1.astype(F32)), (_INFO_E + 1, i2.astype(F32)), (_INFO_W, w1), (_INFO_W + 1, w2),
                     (_INFO_RANK, rank1), (_INFO_RANK + 1, rank2)):
        info = jnp.where(lane == pos, val, info)
    info_ref[0] = info
    carry_ref[...] += jnp.sum(chosen, axis=0, keepdims=True)
    cnt_ref[...] = carry_ref[...]


def _row_copies(src_row, dst_row, sem, tm, wait):
    def body(r, carry):
        for c in range(2):
            cp = pltpu.make_async_copy(src_row(r, c), dst_row(r, c), sem.at[c])
            if wait:
                cp.wait()
            else:
                cp.start()
        return carry
    lax.fori_loop(0, tm, body, 0, unroll=8)


def _scatter_kernel(dest_ref, h_ref, xs_in_ref, xs_ref, sem):
    del xs_in_ref
    tm = h_ref.shape[1]
    src = lambda r, c: h_ref.at[0, pl.ds(r, 1)]
    dst = lambda r, c: xs_ref.at[pl.ds(dest_ref[0, 0, c * tm + r], 1)]
    _row_copies(src, dst, sem, tm, wait=False)
    _row_copies(src, dst, sem, tm, wait=True)


def _group_ffn_kernel(te_ref, tb_ref, nv_ref, xs_ref, wg_ref, wu_ref, wd_ref, ys_ref, x16_ref, acc_ref):
    j = pl.program_id(0)
    f = pl.program_id(1)

    @pl.when(j < nv_ref[0])
    def _():
        @pl.when(f == 0)
        def _():
            x16_ref[...] = xs_ref[...].astype(BF16)
            acc_ref[...] = jnp.zeros_like(acc_ref)

        h = x16_ref[...]
        act = _silu(_dot(h, wg_ref[0])) * _dot(h, wu_ref[0])
        acc_ref[...] += _dot(act.astype(BF16), wd_ref[0])

        @pl.when(f == pl.num_programs(1) - 1)
        def _():
            ys_ref[...] = acc_ref[...]

    @pl.when(j >= nv_ref[0])
    def _():
        ys_ref[...] = jnp.zeros_like(ys_ref)


def _combine_kernel(dest_ref, x_ref, gate_ref, info_ref, ys_ref, o_ref, buf_ref, sem):
    tm = x_ref.shape[1]
    src = lambda r, c: ys_ref.at[pl.ds(dest_ref[0, 0, c * tm + r], 1)]
    dst = lambda r, c: buf_ref.at[c, pl.ds(r, 1)]
    _row_copies(src, dst, sem, tm, wait=False)
    _row_copies(src, dst, sem, tm, wait=True)
    info = info_ref[0]
    mix = info[:, _INFO_W:_INFO_W + 1] * buf_ref[0] + info[:, _INFO_W + 1:_INFO_W + 2] * buf_ref[1]
    o_ref[0] = x_ref[0] + gate_ref[0] * mix


def _moe(x, g, shift, scale, gate, router, wg, wu, wd):
    bn, t, d = x.shape
    ne, _, ff = wg.shape
    tm = MOE_TILE
    nt = t // tm
    n_tok = bn * t
    n_slots = 2 * n_tok // tm + ne
    seq = pl.BlockSpec((1, tm, d), lambda b, i: (b, i, 0))
    mod = pl.BlockSpec((1, 1, d), lambda b, i: (b, 0, 0))
    rec = pl.BlockSpec((1, tm, LANES), lambda b, i: (b, i, 0))
    h, info, counts = pl.pallas_call(
        _route_kernel,
        grid=(bn, nt),
        in_specs=[seq, pl.BlockSpec((1, d), lambda b, i: (0, 0)), mod, mod,
                  pl.BlockSpec(router.shape, lambda b, i: (0, 0))],
        out_specs=[seq, rec, pl.BlockSpec((1, LANES), lambda b, i: (0, 0))],
        out_shape=[jax.ShapeDtypeStruct(x.shape, F32), jax.ShapeDtypeStruct((bn, t, LANES), F32),
                   jax.ShapeDtypeStruct((1, LANES), F32)],
        scratch_shapes=[pltpu.VMEM((1, LANES), F32)],
        compiler_params=_cparams("arbitrary", "arbitrary"),
        name="moe_route",
    )(x, g, shift, scale, router)

    flat = info.reshape(n_tok, LANES)
    expert = flat[:, _INFO_E:_INFO_E + 2].astype(jnp.int32)
    rank = flat[:, _INFO_RANK:_INFO_RANK + 2].astype(jnp.int32)
    tiles_e = (counts[0, :ne].astype(jnp.int32) + tm - 1) // tm
    ends = jnp.cumsum(tiles_e)
    dest = (ends - tiles_e)[expert] * tm + rank
    dest = dest.reshape(bn * nt, tm, 2).transpose(0, 2, 1).reshape(bn * nt, 1, 2 * tm)
    n_valid = ends[-1]
    slot = jnp.minimum(jnp.arange(n_slots, dtype=jnp.int32), n_valid - 1)
    slot_expert = jnp.minimum(jnp.searchsorted(ends, slot, side="right"), ne - 1).astype(jnp.int32)

    dspec = pl.BlockSpec((1, 1, 2 * tm), lambda b, i: (b * nt + i, 0, 0), memory_space=pltpu.SMEM)
    anyspec = pl.BlockSpec(memory_space=pl.ANY)
    xs = pl.pallas_call(
        _scatter_kernel,
        grid=(bn, nt),
        in_specs=[dspec, seq, anyspec],
        out_specs=anyspec,
        out_shape=jax.ShapeDtypeStruct((n_slots * tm, d), F32),
        scratch_shapes=[pltpu.SemaphoreType.DMA((2,))],
        input_output_aliases={2: 0},
        compiler_params=_cparams("arbitrary", "arbitrary"),
        name="moe_scatter",
    )(dest, h, jnp.zeros((n_slots * tm, d), F32))

    nf = 2
    tf = ff // nf
    last = nf - 1
    fidx = lambda j, f, nv: jnp.where(j < nv[0], f, last)
    ys = pl.pallas_call(
        _group_ffn_kernel,
        grid_spec=pltpu.PrefetchScalarGridSpec(
            num_scalar_prefetch=3,
            grid=(n_slots, nf),
            in_specs=[pl.BlockSpec((tm, d), lambda j, f, te, tb, nv: (tb[j], 0)),
                      pl.BlockSpec((1, d, tf), lambda j, f, te, tb, nv: (te[j], 0, fidx(j, f, nv))),
                      pl.BlockSpec((1, d, tf), lambda j, f, te, tb, nv: (te[j], 0, fidx(j, f, nv))),
                      pl.BlockSpec((1, tf, d), lambda j, f, te, tb, nv: (te[j], fidx(j, f, nv), 0))],
            out_specs=pl.BlockSpec((tm, d), lambda j, f, te, tb, nv: (j, 0)),
            scratch_shapes=[pltpu.VMEM((tm, d), BF16), pltpu.VMEM((tm, d), F32)]),
        out_shape=jax.ShapeDtypeStruct((n_slots * tm, d), F32),
        compiler_params=_cparams("arbitrary", "arbitrary"),
        name="moe_group_ffn",
    )(slot_expert, slot, n_valid.reshape(1), xs, wg, wu, wd)

    return pl.pallas_call(
        _combine_kernel,
        grid=(bn, nt),
        in_specs=[dspec, seq, mod, rec, anyspec],
        out_specs=seq,
        out_shape=jax.ShapeDtypeStruct(x.shape, F32),
        scratch_shapes=[pltpu.VMEM((2, tm, d), F32), pltpu.SemaphoreType.DMA((2,))],
        compiler_params=_cparams("arbitrary", "arbitrary"),
        name="moe_combine",
    )(dest, x, gate, info, ys)


def _final_norm_kernel(x_ref, g_ref, o_ref):
    x = x_ref[0]
    o_ref[0] = x * lax.rsqrt(jnp.mean(x * x, axis=-1, keepdims=True) + RMS_EPS) * g_ref[...]


def _final_norm(x, g):
    bn, t, d = x.shape
    tm = min(1024, t)
    seq = pl.BlockSpec((1, tm, d), lambda b, i: (b, i, 0))
    return pl.pallas_call(
        _final_norm_kernel,
        grid=(bn, t // tm),
        in_specs=[seq, pl.BlockSpec((1, d), lambda b, i: (0, 0))],
        out_specs=seq,
        out_shape=jax.ShapeDtypeStruct(x.shape, F32),
        compiler_params=_cparams("parallel", "parallel"),
        name="final_norm",
    )(x, g)


def _head_sum_matrix():
    idx = np.arange(DN_WIDTH) // HEAD_DIM
    return jnp.asarray(idx[:, None] == idx[None, :], dtype=BF16)


def kernel(x, c, ctx, c_ctx, w_mod, b_mod, norm_mix_g, norm_ffn_g, w_in, dn_conv_w, dn_a_log, dn_dt_bias, dn_norm_g, hy_conv_w, hy_w1, hy_b1, hy_freq1, hy_w2, hy_b2, hy_freq2, hy_w3, hy_b3, hy_skip, sw_sink, w_out, ffn_w_gate, ffn_w_up, ffn_w_down, moe_router, moe_w_gate, moe_w_up, moe_w_down, final_norm_g):
    bn, seq_len, d = x.shape
    ctx_len = ctx.shape[1]
    depth = w_mod.shape[0]

    c_rows = jnp.concatenate([c, c_ctx[None, :], jnp.zeros((SUBLANES - bn - 1, d), F32)], axis=0)
    mods = _modulation(c_rows, w_mod, b_mod).reshape(depth, SUBLANES, N_MOD, d)

    rope_tabs = _rope_tables(seq_len)
    gsum = _head_sum_matrix()
    cm_c, sm_c = _dft_matrices(ctx_len)
    hy_params = (hy_w1, hy_b1, hy_freq1, hy_w2, hy_b2, hy_freq2, hy_w3, hy_b3)
    fft_x = _fft_consts(seq_len)
    taps_x = _hy_taps(seq_len, *hy_params)
    hre_x, him_x = _hy_spec_fft(fft_x, taps_x[4], taps_x[2])
    hre_c, him_c = _hy_spec_dense(cm_c, sm_c, *_hy_taps(ctx_len, *hy_params)[:4])
    router = jnp.pad(moe_router, ((0, 0), (0, 0), (0, LANES - N_EXPERTS)))
    zero_state = jnp.zeros((bn, N_DIR, DN_HEADS, HEAD_DIM, HEAD_DIM), F32)

    for layer in range(depth):
        last = layer == depth - 1
        mod_x = [mods[layer, :bn, m][:, None, :] for m in range(N_MOD)]
        mod_c = [jnp.broadcast_to(mods[layer, bn, m][None, None, :], (bn, 1, d)) for m in range(N_MOD)]
        g_mix = norm_mix_g[layer][None, :]
        g_ffn = norm_ffn_g[layer][None, :]
        w_in_l = _relayout_w_in(w_in[layer])
        w_out_l = w_out[layer].astype(BF16)
        gn = jnp.tile(dn_norm_g[layer], DN_HEADS)[None, :]

        qkv_x, z_x, hyp_x, swq_x, swk_x, swv_x, ab_x = _in_proj(x, g_mix, mod_x[0], mod_x[1], w_in_l, rope_tabs)
        qkv_c, z_c, hyp_c, swq_c, swk_c, swv_c, ab_c = _in_proj(ctx, g_mix, mod_c[0], mod_c[1], w_in_l, None)

        dn_in_c = _dn_prep(qkv_c, ab_c, dn_conv_w[layer], dn_a_log[layer], dn_dt_bias[layer], gsum)
        dn_in_x = _dn_prep(qkv_x, ab_x, dn_conv_w[layer], dn_a_log[layer], dn_dt_bias[layer], gsum)
        of_c, ob_c, state_c = _dn_scan(*dn_in_c, zero_state)
        of_x, ob_x, _ = _dn_scan(*dn_in_x, state_c)

        hy_x = _hyena_fft(hyp_x, hy_conv_w[layer], fft_x, hre_x[layer], him_x[layer], hy_skip[layer])
        sw_x = _attention(swq_x, swk_x, swv_x, swk_c, swv_c, sw_sink[layer], True)
        x = _out_proj(of_x, ob_x, z_x, gn, gsum, hy_x, sw_x, x, mod_x[2], w_out_l)

        if not last:
            hy_c = _hyena(hyp_c, hy_conv_w[layer], cm_c, sm_c, hre_c[layer], him_c[layer], hy_skip[layer])
            sw_c = _attention(swq_c, None, None, swk_c, swv_c, sw_sink[layer], False)
            ctx = _out_proj(of_c, ob_c, z_c, gn, gsum, hy_c, sw_c, ctx, mod_c[2], w_out_l)

        i = layer // 2
        streams = [(x, mod_x)] if last else [(x, mod_x), (ctx, mod_c)]
        outs = []
        for s, mod in streams:
            if layer % 2 == 0:
                outs.append(_ffn(s, g_ffn, mod[3], mod[4], mod[5], ffn_w_gate[i].astype(BF16),
                                 ffn_w_up[i].astype(BF16), ffn_w_down[i].astype(BF16)))
            else:
                shape = s.shape
                if mod is mod_c:
                    s = s.reshape(1, -1, d)
                    mod = [m[:1] for m in mod]
                outs.append(_moe(s, g_ffn, mod[3], mod[4], mod[5], router[i], moe_w_gate[i].astype(BF16),
                                 moe_w_up[i].astype(BF16), moe_w_down[i].astype(BF16)).reshape(shape))
        x = outs[0]
        if not last:
            ctx = outs[1]

    return _final_norm(x, final_norm_g[None, :])
```

```python
import functools
import math

import jax
import jax.numpy as jnp
import numpy as np
from jax import lax
from jax.experimental import pallas as pl
from jax.experimental.pallas import tpu as pltpu

F32 = jnp.float32
BF16 = jnp.bfloat16

D_MODEL = 1024
DEPTH = 4
GRID_W = 64
N_MOD = 6
HEAD_DIM = 64
N_DIR = 2
RMS_EPS = 1e-6

DN_HEADS = 6
DN_WIDTH = DN_HEADS * HEAD_DIM
DN_CHUNK = 64
DN_BLOCK = 8
DN_SUB = 4

SW_Q_HEADS = 6
SW_KV_HEADS = 2
SW_GROUP = SW_Q_HEADS // SW_KV_HEADS
SW_WIDTH = SW_Q_HEADS * HEAD_DIM
SW_KV_WIDTH = SW_KV_HEADS * HEAD_DIM
SW_BLOCK = 128
ROPE_THETA = 10000.0
NEG_INF = -1e30

HY_CH = D_MODEL - DN_WIDTH - SW_WIDTH
HY_ORDER = 2
HY_BANDS = 16
HY_EMB = 1 + 2 * HY_BANDS
HY_FFN = 64
HY_FAST_DECAY = 0.3
HY_SLOW_DECAY = 1.5
HY_TARGET = 1e-2

N_EXPERTS = 8

LANES = 128
SUBLANES = 8
VMEM_LIMIT = 56 * 1024 * 1024

_IN_SIZES = (3 * DN_WIDTH, DN_WIDTH, N_DIR * DN_HEADS, N_DIR * DN_HEADS, 3 * HY_CH, SW_WIDTH, SW_KV_WIDTH, SW_KV_WIDTH)
_IN_OFF = tuple(int(v) for v in np.cumsum((0,) + _IN_SIZES))
_SEGS = (3 * DN_WIDTH, DN_WIDTH, 3 * HY_CH, SW_WIDTH, SW_KV_WIDTH, SW_KV_WIDTH, LANES)
_SEG_Q, _SEG_K = 3, 4


def _cparams(*sem):
    return pltpu.CompilerParams(dimension_semantics=sem, vmem_limit_bytes=VMEM_LIMIT)


def _dot(a, b):
    return jnp.dot(a, b, preferred_element_type=F32)


def _dot_nt(a, b):
    return lax.dot_general(a, b, (((1,), (1,)), ((), ())), preferred_element_type=F32)


def _split(x):
    hi = x.astype(BF16)
    lo = (x - hi.astype(F32)).astype(BF16)
    return hi, lo


def _dot3(a, b):
    ah, al = _split(a)
    bh, bl = _split(b)
    return _dot(ah, bh) + (_dot(al, bh) + _dot(ah, bl))


def _dot2_exact_rhs(a, b_bf16):
    ah, al = _split(a)
    return _dot(ah, b_bf16) + _dot(al, b_bf16)


def _sigmoid(x):
    return 1.0 / (1.0 + jnp.exp(-x))


def _silu(x):
    return x * _sigmoid(x)


def _softplus(x):
    return jnp.maximum(x, 0.0) + jnp.log(1.0 + jnp.exp(-jnp.abs(x)))


def _rms_mod(x, g, shift, scale):
    y = x * lax.rsqrt(jnp.mean(x * x, axis=-1, keepdims=True) + RMS_EPS)
    return (y * g) * (1.0 + scale) + shift


def _mod_kernel(c_ref, w_ref, b_ref, o_ref):
    o_ref[0] = _dot3(_silu(c_ref[...]), w_ref[0]) + b_ref[0]


def _modulation(c_rows, w_mod, b_mod):
    depth, d, n = w_mod.shape
    rows = c_rows.shape[0]
    tn = 1024
    return pl.pallas_call(
        _mod_kernel,
        grid=(depth, n // tn),
        in_specs=[
            pl.BlockSpec((rows, d), lambda l, j: (0, 0)),
            pl.BlockSpec((1, d, tn), lambda l, j: (l, 0, j)),
            pl.BlockSpec((1, 1, tn), lambda l, j: (l, 0, j)),
        ],
        out_specs=pl.BlockSpec((1, rows, tn), lambda l, j: (l, 0, j)),
        out_shape=jax.ShapeDtypeStruct((depth, rows, n), F32),
        compiler_params=_cparams("parallel", "parallel"),
        name="adaln_mod",
    )(c_rows, w_mod, b_mod.reshape(depth, 1, n))


def _rope_apply(x, cos, sin_signed):
    lane = lax.broadcasted_iota(jnp.int32, x.shape, 1)
    first = (lane & 31) < 16
    partner = jnp.where(first, pltpu.roll(x, LANES - 16, 1), pltpu.roll(x, 16, 1))
    return x * cos + partner * sin_signed


def _in_proj_kernel(x_ref, g_ref, sh_ref, sc_ref, w_ref, *rest, rope):
    if rope:
        cos_ref, sin_ref = rest[:2]
        outs = rest[2:]
    else:
        outs = rest
    h = _rms_mod(x_ref[0], g_ref[...], sh_ref[0], sc_ref[0]).astype(BF16)
    off = 0
    for idx, (o_ref, n) in enumerate(zip(outs, _SEGS)):
        r = _dot(h, w_ref[:, off:off + n])
        if idx == _SEG_Q:
            r = r * (HEAD_DIM ** -0.5)
        if rope and idx in (_SEG_Q, _SEG_K):
            cos, sin = cos_ref[...], sin_ref[...]
            r = jnp.concatenate(
                [_rope_apply(r[:, c:c + LANES], cos, sin) for c in range(0, n, LANES)], axis=1)
        o_ref[0] = r
        off += n


def _in_proj(x, g, shift, scale, w, rope_tabs):
    bn, t, d = x.shape
    tm = min(512, t)
    rope = rope_tabs is not None
    in_specs = [
        pl.BlockSpec((1, tm, d), lambda b, i: (b, i, 0)),
        pl.BlockSpec((1, d), lambda b, i: (0, 0)),
        pl.BlockSpec((1, 1, d), lambda b, i: (b, 0, 0)),
        pl.BlockSpec((1, 1, d), lambda b, i: (b, 0, 0)),
        pl.BlockSpec(w.shape, lambda b, i: (0, 0)),
    ]
    args = [x, g, shift, scale, w]
    if rope:
        in_specs += [pl.BlockSpec((tm, LANES), lambda b, i: (i, 0))] * 2
        args += list(rope_tabs)
    return pl.pallas_call(
        functools.partial(_in_proj_kernel, rope=rope),
        grid=(bn, t // tm),
        in_specs=in_specs,
        out_specs=[pl.BlockSpec((1, tm, n), lambda b, i: (b, i, 0)) for n in _SEGS],
        out_shape=[jax.ShapeDtypeStruct((bn, t, n), F32) for n in _SEGS],
        compiler_params=_cparams("parallel", "parallel"),
        name="in_proj_rope" if rope else "in_proj",
    )(*args)


def _relayout_w_in(w_in):
    o = _IN_OFF
    d = w_in.shape[0]
    pad = jnp.zeros((d, LANES - 2 * N_DIR * DN_HEADS), w_in.dtype)
    cols = [w_in[:, o[0]:o[2]], w_in[:, o[4]:o[8]], w_in[:, o[2]:o[4]], pad]
    return jnp.concatenate(cols, axis=1).astype(BF16)


def _rope_tables(length):
    n = HEAD_DIM // 4
    inv = jnp.power(ROPE_THETA, -jnp.arange(n, dtype=F32) / n)
    t = jnp.arange(length)
    row = (t // GRID_W).astype(F32)[:, None] * inv[None, :]
    col = (t % GRID_W).astype(F32)[:, None] * inv[None, :]
    cos = jnp.concatenate([jnp.cos(row), jnp.cos(row), jnp.cos(col), jnp.cos(col)], axis=1)
    sin = jnp.concatenate([-jnp.sin(row), jnp.sin(row), -jnp.sin(col), jnp.sin(col)], axis=1)
    return jnp.tile(cos, (1, 2)), jnp.tile(sin, (1, 2))


def _conv3_rows(x, prev8, next8, w, first, last):
    tm = x.shape[0]
    row = lax.broadcasted_iota(jnp.int32, x.shape, 0)
    before = jnp.where(first, 0.0, prev8[SUBLANES - 1:SUBLANES, :])
    after = jnp.where(last, 0.0, next8[0:1, :])
    xm = jnp.where(row == 0, before, pltpu.roll(x, 1, 0))
    xp = jnp.where(row == tm - 1, after, pltpu.roll(x, tm - 1, 0))
    return xm * w[0:1, :] + x * w[1:2, :] + xp * w[2:3, :]


def _halo_specs(tm, t, width):
    nb8 = t // SUBLANES
    step = tm // SUBLANES
    return [
        pl.BlockSpec((1, tm, width), lambda b, i: (b, i, 0)),
        pl.BlockSpec((1, SUBLANES, width), lambda b, i: (b, jnp.maximum(i * step - 1, 0), 0)),
        pl.BlockSpec((1, SUBLANES, width), lambda b, i: (b, jnp.minimum((i + 1) * step, nb8 - 1), 0)),
    ]


def _dn_prep_kernel(x_ref, xp_ref, xn_ref, w_ref, ab_ref, al_ref, dt_ref, gsum_ref,
                    q_ref, k_ref, v_ref, g_ref, beta_ref):
    i = pl.program_id(1)
    y = _silu(_conv3_rows(x_ref[0], xp_ref[0], xn_ref[0], w_ref[...], i == 0, i == pl.num_programs(1) - 1))
    q, k, v = y[:, :DN_WIDTH], y[:, DN_WIDTH:2 * DN_WIDTH], y[:, 2 * DN_WIDTH:]
    gs = gsum_ref[...]
    q = q * lax.rsqrt(_dot2_exact_rhs(q * q, gs) + RMS_EPS) * (HEAD_DIM ** -0.5)
    k = k * lax.rsqrt(_dot2_exact_rhs(k * k, gs) + RMS_EPS)
    for h in range(DN_HEADS):
        sl = slice(h * HEAD_DIM, (h + 1) * HEAD_DIM)
        q_ref[0, h] = q[:, sl]
        k_ref[0, h] = k[:, sl]
        v_ref[0, h] = v[:, sl]
    ab = ab_ref[0]
    lane = lax.broadcasted_iota(jnp.int32, ab.shape, 1)
    nh = N_DIR * DN_HEADS
    g_ref[0] = jnp.where(lane < nh, -jnp.exp(al_ref[...]) * _softplus(ab + dt_ref[...]), 0.0)
    beta_ref[0] = jnp.where(lane < nh, _sigmoid(pltpu.roll(ab, LANES - nh, 1)), 0.0)


def _dn_prep(qkv, ab, conv_w, a_log, dt_bias, gsum):
    bn, t, width = qkv.shape
    tm = min(512, t)
    pad = LANES - N_DIR * DN_HEADS
    al = jnp.pad(a_log.reshape(1, -1), ((0, 0), (0, pad)))
    dt = jnp.pad(dt_bias.reshape(1, -1), ((0, 0), (0, pad)))
    head = pl.BlockSpec((1, DN_HEADS, tm, HEAD_DIM), lambda b, i: (b, 0, i, 0))
    row = pl.BlockSpec((1, tm, LANES), lambda b, i: (b, i, 0))
    const = lambda shape: pl.BlockSpec(shape, lambda b, i: (0,) * len(shape))
    head_shape = jax.ShapeDtypeStruct((bn, DN_HEADS, t, HEAD_DIM), F32)
    row_shape = jax.ShapeDtypeStruct((bn, t, LANES), F32)
    return pl.pallas_call(
        _dn_prep_kernel,
        grid=(bn, t // tm),
        in_specs=_halo_specs(tm, t, width) + [const(conv_w.shape), row, const(al.shape), const(dt.shape),
                                              const(gsum.shape)],
        out_specs=[head, head, head, row, row],
        out_shape=[head_shape, head_shape, head_shape, row_shape, row_shape],
        compiler_params=_cparams("parallel", "parallel"),
        name="dn_prep",
    )(qkv, qkv, qkv, conv_w, ab, al, dt, gsum)


def _cumsum_rows(x, reverse):
    c = x.shape[0]
    row = lax.broadcasted_iota(jnp.int32, x.shape, 0)
    s = 1
    while s < c:
        if reverse:
            x = x + jnp.where(row < c - s, pltpu.roll(x, c - s, 0), 0.0)
        else:
            x = x + jnp.where(row >= s, pltpu.roll(x, s, 0), 0.0)
        s *= 2
    return x


def _mm1(ps, cs):
    return [_dot(p.astype(BF16), c.astype(BF16)) for p, c in zip(ps, cs)]


def _mm3(ps, cs):
    out = []
    for p, c in zip(ps, cs):
        c16 = c.astype(BF16)
        c_lo = (c - c16.astype(F32)).astype(BF16)
        p_hi = p.astype(BF16).astype(F32)
        lhs = jnp.concatenate([p_hi, p - p_hi, p_hi], axis=1).astype(BF16)
        out.append(_dot(lhs, jnp.concatenate([c16, c16, c_lo], axis=0)))
    return out


def _dn_chunks(probs):
    c, d = DN_CHUNK, HEAD_DIM
    ii = lax.broadcasted_iota(jnp.int32, (c, c), 0)
    jj = lax.broadcasted_iota(jnp.int32, (c, c), 1)
    di = lax.broadcasted_iota(jnp.int32, (d, d), 0)
    dj = lax.broadcasted_iota(jnp.int32, (d, d), 1)
    n = len(probs)
    k16 = [pr["k"].astype(BF16) for pr in probs]
    kk = [_dot_nt(k16[i], k16[i]) for i in range(n)]
    qk = [_dot_nt(probs[i]["q"].astype(BF16), k16[i]) for i in range(n)]
    kt = [pr["k"].T for pr in probs]
    decay, e_cum, x, p = [], [], [], []
    for i, pr in enumerate(probs):
        incl = (ii >= jj) if pr["lower"] else (ii <= jj)
        strict = (ii > jj) if pr["lower"] else (ii < jj)
        dec = jnp.where(incl, jnp.exp(jnp.where(incl, pr["cum_col"] - pr["cum_row"], 0.0)), 0.0)
        ec = jnp.exp(pr["cum_col"])
        decay.append(dec)
        e_cum.append(ec)
        x.append(jnp.concatenate([pr["v"] * pr["beta_col"], pr["k"] * pr["beta_col"] * ec], axis=1))
        p.append(jnp.where(strict, -(kk[i] * pr["beta_col"] * dec), 0.0))
    same_blk = (ii // DN_BLOCK) == (jj // DN_BLOCK)
    eye = jnp.where(ii == jj, 1.0, 0.0)
    p0 = [jnp.where(same_blk, p[i], 0.0) for i in range(n)]
    a_off = [jnp.where(same_blk, 0.0, -p[i]) for i in range(n)]
    q = _mm1(p0, p0)
    xd = [eye + p0[i] for i in range(n)]
    lvl = 2
    while lvl < DN_BLOCK:
        last = 2 * lvl >= DN_BLOCK
        r = _mm1(q, xd if last else [jnp.concatenate([xd[i], q[i]], axis=1) for i in range(n)])
        xd = [xd[i] + r[i][:, :c] for i in range(n)]
        if not last:
            q = [r[i][:, c:] for i in range(n)]
        lvl *= 2
    r = _mm1(xd, [jnp.concatenate([x[i], a_off[i]], axis=1) for i in range(n)])
    x = [r[i][:, :2 * d] for i in range(n)]
    q = [-r[i][:, 2 * d:] for i in range(n)]
    lvl = 1
    while lvl < c // DN_BLOCK:
        last = 2 * lvl >= c // DN_BLOCK
        r = _mm3(q, x if last else [jnp.concatenate([x[i], q[i]], axis=1) for i in range(n)])
        x = [x[i] + r[i][:, :2 * d] for i in range(n)]
        if not last:
            q = [r[i][:, 2 * d:] for i in range(n)]
        lvl *= 2
    lhs = [jnp.concatenate([qk[i] * decay[i], kt[i] * jnp.exp(probs[i]["tot"] - probs[i]["cum_row"])], axis=0)
           for i in range(n)]
    r = [_dot(lhs[i].astype(BF16), x[i].astype(BF16)) for i in range(n)]
    lhs = [jnp.concatenate([jnp.where(di == dj, jnp.exp(probs[i]["tot"]), 0.0) - r[i][c:, d:],
                            probs[i]["q"] * e_cum[i] - r[i][:c, d:]], axis=0).astype(BF16) for i in range(n)]
    return [(lhs[i], r[i][c:, :d], r[i][:c, :d]) for i in range(n)]


def _dn_scan_kernel(qf_ref, kf_ref, vf_ref, gf_ref, bf_ref, qb_ref, kb_ref, vb_ref, gb_ref, bb_ref,
                    s0_ref, of_ref, ob_ref, sfin_ref, s_ref):
    t = pl.program_id(1)
    c, d = DN_CHUNK, HEAD_DIM

    @pl.when(t == 0)
    def _():
        s_ref[...] = s0_ref[0]

    probs = []
    for sub in range(DN_SUB):
        for d_idx, (q_ref, k_ref, v_ref, g_ref, b_ref) in enumerate(
                ((qf_ref, kf_ref, vf_ref, gf_ref, bf_ref), (qb_ref, kb_ref, vb_ref, gb_ref, bb_ref))):
            lower = d_idx == 0
            rows = pl.ds((sub if lower else DN_SUB - 1 - sub) * c, c)
            cum = _cumsum_rows(g_ref[0, rows, :], reverse=not lower)
            cum_t = cum.T
            beta = b_ref[0, rows, :]
            last = c - 1 if lower else 0
            for h in range(DN_HEADS):
                col = d_idx * DN_HEADS + h
                cum_row = cum_t[col:col + 1, :]
                probs.append(dict(q=q_ref[0, h, rows, :], k=k_ref[0, h, rows, :], v=v_ref[0, h, rows, :],
                                  lower=lower, cum_col=cum[:, col:col + 1], cum_row=cum_row,
                                  tot=cum_row[:, last:last + 1], beta_col=beta[:, col:col + 1]))
    local = _dn_chunks(probs)

    state = [s_ref[d_idx, h] for d_idx in range(N_DIR) for h in range(DN_HEADS)]
    for sub in range(DN_SUB):
        base = sub * N_DIR * DN_HEADS
        r = [_dot(local[base + i][0], state[i].astype(BF16)) for i in range(len(state))]
        state = [r[i][:d] + local[base + i][1] for i in range(len(state))]
        for d_idx, o_ref in enumerate((of_ref, ob_ref)):
            row0 = (sub if d_idx == 0 else DN_SUB - 1 - sub) * c
            o_ref[0, row0:row0 + c, :] = jnp.concatenate(
                [r[d_idx * DN_HEADS + h][d:] + local[base + d_idx * DN_HEADS + h][2] for h in range(DN_HEADS)],
                axis=1)
    for d_idx in range(N_DIR):
        for h in range(DN_HEADS):
            s_ref[d_idx, h] = state[d_idx * DN_HEADS + h]

    @pl.when(t == pl.num_programs(1) - 1)
    def _():
        sfin_ref[0] = s_ref[...]


def _dn_scan(q, k, v, g, beta, s0):
    bn, nh, t, d = q.shape
    rows = DN_SUB * DN_CHUNK
    n = t // rows
    head_f = pl.BlockSpec((1, nh, rows, d), lambda b, i: (b, 0, i, 0))
    head_b = pl.BlockSpec((1, nh, rows, d), lambda b, i: (b, 0, n - 1 - i, 0))
    row_f = pl.BlockSpec((1, rows, LANES), lambda b, i: (b, i, 0))
    row_b = pl.BlockSpec((1, rows, LANES), lambda b, i: (b, n - 1 - i, 0))
    state = pl.BlockSpec((1, N_DIR, nh, d, d), lambda b, i: (b, 0, 0, 0, 0))
    out_f = pl.BlockSpec((1, rows, nh * d), lambda b, i: (b, i, 0))
    out_b = pl.BlockSpec((1, rows, nh * d), lambda b, i: (b, n - 1 - i, 0))
    o_shape = jax.ShapeDtypeStruct((bn, t, nh * d), F32)
    return pl.pallas_call(
        _dn_scan_kernel,
        grid=(bn, n),
        in_specs=[head_f, head_f, head_f, row_f, row_f, head_b, head_b, head_b, row_b, row_b, state],
        out_specs=[out_f, out_b, state],
        out_shape=[o_shape, o_shape, jax.ShapeDtypeStruct(s0.shape, F32)],
        scratch_shapes=[pltpu.VMEM((N_DIR, nh, d, d), F32)],
        compiler_params=_cparams("parallel", "arbitrary"),
        name="dn_scan",
    )(q, k, v, g, beta, q, k, v, g, beta, s0)


def _hy_prep_kernel(x_ref, xp_ref, xn_ref, w_ref, v_ref, v16_ref, x1_ref, x2_ref):
    i = pl.program_id(1)
    y = _conv3_rows(x_ref[0], xp_ref[0], xn_ref[0], w_ref[...], i == 0, i == pl.num_programs(1) - 1)
    v = y[:, :HY_CH]
    v_ref[...] = v
    v16_ref[...] = v.astype(BF16)
    x1_ref[...] = y[:, HY_CH:2 * HY_CH]
    x2_ref[...] = y[:, 2 * HY_CH:]


def _hy_prep(z, conv_w):
    bn, t, width = z.shape
    tm = min(512, t)
    out = pl.BlockSpec((tm, HY_CH), lambda b, i: (i, b))
    f32 = jax.ShapeDtypeStruct((t, bn * HY_CH), F32)
    return pl.pallas_call(
        _hy_prep_kernel,
        grid=(bn, t // tm),
        in_specs=_halo_specs(tm, t, width) + [pl.BlockSpec(conv_w.shape, lambda b, i: (0, 0))],
        out_specs=[out, out, out, out],
        out_shape=[f32, jax.ShapeDtypeStruct((t, bn * HY_CH), BF16), f32, f32],
        compiler_params=_cparams("parallel", "parallel"),
        name="hy_prep",
    )(z, z, z, conv_w)


def _dft_matrices(length):
    n = 2 * length
    k = jnp.arange(length, dtype=jnp.int32)
    ang = ((k[:, None] * k[None, :]) % n).astype(F32) * (2.0 * math.pi / n)
    alt = (1 - 2 * (k % 2)).astype(F32)
    sin = jnp.where(k[:, None] == 0, alt[None, :], jnp.sin(ang))
    return jnp.cos(ang).astype(BF16), sin.astype(BF16)


def _hy_filter_kernel(f_ref, w1_ref, b1_ref, f1_ref, w2_ref, b2_ref, f2_ref, w3_ref, b3_ref, dl_ref,
                      p_ref, q_ref, ssq_ref, nyq_ref, taps_ref):
    i = pl.program_id(1)
    feats = f_ref[...]
    h = jnp.sin(f1_ref[0] * (_dot3(feats, w1_ref[0]) + b1_ref[0]))
    h = jnp.sin(f2_ref[0] * (_dot3(h, w2_ref[0]) + b2_ref[0]))
    h = _dot3(h, w3_ref[0]) + b3_ref[0]
    win = jnp.exp(-feats[:, 0:1] * dl_ref[...])
    half = HY_ORDER * HY_CH
    fwd = h[:, :half] * jnp.concatenate([win] * HY_ORDER, axis=1)
    bwd = h[:, half:] * jnp.concatenate([win] * HY_ORDER, axis=1)
    row = lax.broadcasted_iota(jnp.int32, bwd.shape, 0)
    bwd = jnp.where((row == 0) & (i == 0), 0.0, bwd)
    p = fwd + bwd
    p_ref[0] = p.astype(BF16)
    q_ref[0] = (bwd - fwd).astype(BF16)
    taps_ref[0] = jnp.concatenate([fwd, bwd], axis=1).astype(BF16)
    alt = (1 - 2 * (row & 1)).astype(F32)

    @pl.when(i == 0)
    def _():
        ssq_ref[0] = jnp.zeros_like(ssq_ref[0])
        nyq_ref[0] = jnp.zeros_like(nyq_ref[0])

    ssq_ref[0] += jnp.sum(fwd * fwd + bwd * bwd, axis=0, keepdims=True)
    nyq_ref[0] += jnp.sum(p * alt, axis=0, keepdims=True)


def _hy_spec_kernel(c_ref, s_ref, p_ref, q_ref, ssq_ref, nyq_ref, hre_ref, him_ref):
    i = pl.program_id(1)
    scale = lax.rsqrt(ssq_ref[0] + RMS_EPS)
    hre = _dot(c_ref[...], p_ref[0]) * scale
    him = _dot(s_ref[...], q_ref[0]) * scale
    row = lax.broadcasted_iota(jnp.int32, him.shape, 0)
    hre_ref[0] = hre
    him_ref[0] = jnp.where((row == 0) & (i == 0), nyq_ref[0] * scale, him)


def _hy_taps(length, w1, b1, f1, w2, b2, f2, w3, b3):
    depth = w1.shape[0]
    t = jnp.linspace(0.0, 1.0, length, dtype=F32)[:, None]
    omega = 2.0 * math.pi * jnp.arange(length, dtype=F32) / length
    bands = jnp.linspace(1e-4, HY_BANDS - 1, HY_BANDS, dtype=F32)
    ang = omega[:, None] * bands[None, :]
    feats = jnp.concatenate([t, jnp.cos(ang), -jnp.sin(ang), jnp.zeros((length, LANES - HY_EMB), F32)], axis=-1)
    w1p = jnp.pad(w1, ((0, 0), (0, LANES - HY_EMB), (0, 0)))
    max_decay = math.log(HY_TARGET) / HY_FAST_DECAY
    min_decay = math.log(HY_TARGET) / HY_SLOW_DECAY
    deltas = jnp.abs(jnp.linspace(min_decay, max_decay, HY_CH, dtype=F32))[None, :]
    half = HY_ORDER * HY_CH
    tm = min(512, length)
    vec = lambda a: a.reshape(depth, 1, -1)
    lay = lambda shape: pl.BlockSpec((1,) + shape, lambda l, i: (l,) + (0,) * len(shape))
    rows = lambda n: pl.BlockSpec((1, tm, n), lambda l, i: (l, i, 0))
    return pl.pallas_call(
        _hy_filter_kernel,
        grid=(depth, length // tm),
        in_specs=[pl.BlockSpec((tm, LANES), lambda l, i: (i, 0)),
                  lay((LANES, HY_FFN)), lay((1, HY_FFN)), lay((1, HY_FFN)),
                  lay((HY_FFN, HY_FFN)), lay((1, HY_FFN)), lay((1, HY_FFN)),
                  lay((HY_FFN, 2 * half)), lay((1, 2 * half)),
                  pl.BlockSpec((1, HY_CH), lambda l, i: (0, 0))],
        out_specs=[rows(half), rows(half), lay((1, half)), lay((1, half)), rows(2 * half)],
        out_shape=[jax.ShapeDtypeStruct((depth, length, half), BF16),
                   jax.ShapeDtypeStruct((depth, length, half), BF16),
                   jax.ShapeDtypeStruct((depth, 1, half), F32),
                   jax.ShapeDtypeStruct((depth, 1, half), F32),
                   jax.ShapeDtypeStruct((depth, length, 2 * half), BF16)],
        compiler_params=_cparams("parallel", "arbitrary"),
        name="hy_filter",
    )(feats, w1p, vec(b1), vec(f1), w2, vec(b2), vec(f2), w3, vec(b3), deltas)


def _hy_spec_dense(cmat, smat, p, q, ssq, nyq):
    depth, length, half = p.shape
    lay = lambda shape: pl.BlockSpec((1,) + shape, lambda l, i: (l,) + (0,) * len(shape))
    tk = min(512, length)
    return pl.pallas_call(
        _hy_spec_kernel,
        grid=(depth, length // tk),
        in_specs=[pl.BlockSpec((tk, length), lambda l, i: (i, 0)),
                  pl.BlockSpec((tk, length), lambda l, i: (i, 0)),
                  lay((length, half)), lay((length, half)), lay((1, half)), lay((1, half))],
        out_specs=[pl.BlockSpec((1, tk, half), lambda l, i: (l, i, 0))] * 2,
        out_shape=[jax.ShapeDtypeStruct((depth, length, half), F32)] * 2,
        compiler_params=_cparams("parallel", "parallel"),
        name="hy_spec",
    )(cmat, smat, p, q, ssq, nyq)


def _hy_fwd_kernel(c_ref, s_ref, u_ref, hre_ref, him_ref, yre_ref, yim_ref, *, n_points, n_batch):
    i = pl.program_id(0)
    ure = _dot(c_ref[...], u_ref[...])
    usn = _dot(s_ref[...], u_ref[...])
    hre = jnp.concatenate([hre_ref[...]] * n_batch, axis=1)
    him = jnp.concatenate([him_ref[...]] * n_batch, axis=1)
    row0 = (lax.broadcasted_iota(jnp.int32, ure.shape, 0) == 0) & (i == 0)
    yre = jnp.where(row0, ure * hre * (1.0 / n_points), (ure * hre + usn * him) * (2.0 / n_points))
    yim = jnp.where(row0, usn * him * (1.0 / n_points), (usn * hre - ure * him) * (2.0 / n_points))
    yre_ref[...] = yre.astype(BF16)
    yim_ref[...] = yim.astype(BF16)


def _hy_inv_kernel(c_ref, s_ref, yre_ref, yim_ref, u_ref, gate_ref, skip_ref, o_ref, *o16_ref, n_batch):
    i = pl.program_id(0)
    tm = c_ref.shape[0]
    conv_c = _dot(c_ref[...], yre_ref[...])
    conv_s = _dot(s_ref[...], yim_ref[...])
    trow = lax.broadcasted_iota(jnp.int32, conv_c.shape, 0) + i * tm
    alt = (1 - 2 * (trow & 1)).astype(F32)
    nyq = yim_ref[0:1, :].astype(F32)
    conv = conv_c + jnp.where(trow == 0, 0.0, conv_s) + alt * nyq
    skip = jnp.concatenate([skip_ref[...]] * n_batch, axis=1)
    y = gate_ref[...] * (conv + skip * u_ref[...])
    o_ref[...] = y
    if o16_ref:
        o16_ref[0][...] = y.astype(BF16)


def _hy_long_conv(cmat, smat, u16, u, gate, hre, him, skip, order, emit_bf16):
    length, cols = u.shape
    n_batch = cols // HY_CH
    tk = min(512, length)
    full = pl.BlockSpec((length, cols), lambda i: (0, 0))
    mat = pl.BlockSpec((tk, length), lambda i: (i, 0))
    tile = pl.BlockSpec((tk, cols), lambda i: (i, 0))
    spec = pl.BlockSpec((tk, HY_CH), lambda i: (i, order))
    yre, yim = pl.pallas_call(
        functools.partial(_hy_fwd_kernel, n_points=2 * length, n_batch=n_batch),
        grid=(length // tk,),
        in_specs=[mat, mat, full, spec, spec],
        out_specs=[tile, tile],
        out_shape=[jax.ShapeDtypeStruct((length, cols), BF16)] * 2,
        compiler_params=_cparams("parallel"),
        name="hy_fwd",
    )(cmat, smat, u16, hre, him)
    tm = min(256, length)
    mat = pl.BlockSpec((tm, length), lambda i: (i, 0))
    tile = pl.BlockSpec((tm, cols), lambda i: (i, 0))
    out_shape = [jax.ShapeDtypeStruct((length, cols), F32)]
    if emit_bf16:
        out_shape.append(jax.ShapeDtypeStruct((length, cols), BF16))
    return pl.pallas_call(
        functools.partial(_hy_inv_kernel, n_batch=n_batch),
        grid=(length // tm,),
        in_specs=[mat, mat, full, full, tile, tile, pl.BlockSpec((1, HY_CH), lambda i: (0, 0))],
        out_specs=[tile] * len(out_shape),
        out_shape=out_shape,
        compiler_params=_cparams("parallel"),
        name="hy_inv",
    )(cmat, smat, yre, yim, u, gate, skip[order][None, :])


def _hyena(z, conv_w, cmat, smat, hre, him, skip):
    v, v16, x1, x2 = _hy_prep(z, conv_w)
    y1, y16 = _hy_long_conv(cmat, smat, v16, v, x1, hre, him, skip, 0, True)
    (y,) = _hy_long_conv(cmat, smat, y16, y1, x2, hre, him, skip, 1, False)
    return y


FFT_N2 = 256
FFT_ROWS = 16


def _fft_consts(length):
    n = 2 * length
    n1 = n // FFT_N2
    nb = length // FFT_N2
    r = n1 // 2 + 1
    rp = -(-2 * r // SUBLANES) * SUBLANES
    kk = np.arange(r)[:, None, None] + n1 * np.arange(FFT_N2)[None, :, None]
    ang2 = 2 * np.pi * ((kk * np.arange(FFT_N2)[None, None, :]) % n) / n
    gc, gs = np.cos(ang2), np.sin(ang2)
    b16 = lambda a: jnp.asarray(a, F32).astype(BF16)
    return dict(r=r, rp=rp, n1=n1, nb=nb, gc=b16(gc), gs=b16(gs),
                gct=b16(gc.transpose(0, 2, 1)), gst=b16(gs.transpose(0, 2, 1)))


def _axpy(acc, coef, t):
    if abs(coef) < 1e-9:
        return acc
    term = t if abs(coef - 1.0) < 1e-9 else (-t if abs(coef + 1.0) < 1e-9 else coef * t)
    return term if acc is None else acc + term


def _nz(t, like):
    return jnp.zeros_like(like) if t is None else t


def _fft_s1f_rows_kernel(v_ref, o_ref, *, n1, nb, r):
    half = n1 // 2
    x = [v_ref[0, b].astype(F32) for b in range(nb)]
    done = set()
    for k in range(half // 2 + 1):
        p = half - k
        ce = co = se = so = None
        for b in range(nb):
            ang = 2.0 * math.pi * ((b * k) % n1) / n1
            if b % 2 == 0:
                ce, se = _axpy(ce, math.cos(ang), x[b]), _axpy(se, math.sin(ang), x[b])
            else:
                co, so = _axpy(co, math.cos(ang), x[b]), _axpy(so, math.sin(ang), x[b])
        ce, co, se, so = (_nz(t, x[0]) for t in (ce, co, se, so))
        o_ref[0, k] = (ce + co).astype(BF16)
        o_ref[0, r + k] = (-(se + so)).astype(BF16)
        done.update((k, r + k))
        if p != k:
            o_ref[0, p] = (ce - co).astype(BF16)
            o_ref[0, r + p] = (se - so).astype(BF16)
            done.update((p, r + p))
    for row in range(o_ref.shape[1]):
        if row not in done:
            o_ref[0, row] = jnp.zeros(o_ref.shape[2:], BF16)


def _fft_s1i_rows_kernel(br_ref, bi_ref, u_ref, gate_ref, skip_ref, o_ref, *o16_ref, n1, nb, n_batch):
    half = n1 // 2
    n = n1 * FFT_N2
    acc = [None] * nb
    for k in range(half // 2 + 1):
        p = half - k
        w = (1.0 if k == 0 else 2.0) / n
        rk, ik = br_ref[k].astype(F32), bi_ref[k].astype(F32)
        if p != k:
            rp_, ip_ = br_ref[p].astype(F32), bi_ref[p].astype(F32)
            r_even, r_odd, i_even, i_odd = rk + rp_, rk - rp_, ik - ip_, ik + ip_
        else:
            r_even = r_odd = rk
            i_even = i_odd = ik
        for b in range(nb):
            ang = 2.0 * math.pi * ((b * k) % n1) / n1
            re, im = (r_even, i_even) if b % 2 == 0 else (r_odd, i_odd)
            acc[b] = _axpy(_axpy(acc[b], w * math.cos(ang), re), -w * math.sin(ang), im)
    skip = jnp.concatenate([skip_ref[...]] * n_batch, axis=1)
    for b in range(nb):
        y = gate_ref[b] * (acc[b] + skip * u_ref[b])
        o_ref[b] = y
        if o16_ref:
            o16_ref[0][b] = y.astype(BF16)


def _fft_stage1_rows(fc, v):
    g, length, cols = v.shape
    r, rp, n1, nb = fc["r"], fc["rp"], fc["n1"], fc["nb"]
    tr = FFT_ROWS
    return pl.pallas_call(
        functools.partial(_fft_s1f_rows_kernel, n1=n1, nb=nb, r=r),
        grid=(g, FFT_N2 // tr),
        in_specs=[pl.BlockSpec((1, nb, tr, cols), lambda l, i: (l, 0, i, 0))],
        out_specs=pl.BlockSpec((1, rp, tr, cols), lambda l, i: (l, 0, i, 0)),
        out_shape=jax.ShapeDtypeStruct((g, rp, FFT_N2, cols), BF16),
        compiler_params=_cparams("parallel", "parallel"),
        name="fft_s1_fwd",
    )(v.reshape(g, nb, FFT_N2, cols))


def _fft_s2f_kernel(ar_ref, ai_ref, gc_ref, gs_ref, *rest, spectrum, n_batch):
    ar, ai, gc, gs = ar_ref[0, 0], ai_ref[0, 0], gc_ref[0], gs_ref[0]
    xr = _dot(gc, ar) + _dot(gs, ai)
    xi = _dot(gc, ai) - _dot(gs, ar)
    if spectrum:
        ssq_ref, hr_ref, hi_ref = rest
        half = xr.shape[1] // 2
        scale = lax.rsqrt(ssq_ref[0] + RMS_EPS)
        hr_ref[0, 0] = (xr[:, :half] + xr[:, half:]) * scale
        hi_ref[0, 0] = (xi[:, :half] - xi[:, half:]) * scale
    else:
        hr_ref, hi_ref, yr_ref, yi_ref = rest
        hr = jnp.concatenate([hr_ref[0]] * n_batch, axis=1)
        hi = jnp.concatenate([hi_ref[0]] * n_batch, axis=1)
        yr_ref[0] = (xr * hr - xi * hi).astype(BF16)
        yi_ref[0] = (xr * hi + xi * hr).astype(BF16)


def _fft_s2i_kernel(yr_ref, yi_ref, gct_ref, gst_ref, br_ref, bi_ref):
    yr, yi, gct, gst = yr_ref[0], yi_ref[0], gct_ref[0], gst_ref[0]
    br_ref[0] = (_dot(gct, yr) - _dot(gst, yi)).astype(BF16)
    bi_ref[0] = (_dot(gct, yi) + _dot(gst, yr)).astype(BF16)


def _hy_spec_fft(fc, taps, ssq):
    depth, length, cols = taps.shape
    r, rp, nb = fc["r"], fc["rp"], fc["nb"]
    half = cols // 2
    a = _fft_stage1_rows(fc, taps)
    res = lambda off: pl.BlockSpec((1, 1, FFT_N2, cols), lambda l, k: (l, k + off, 0, 0))
    g = pl.BlockSpec((1, FFT_N2, FFT_N2), lambda l, k: (k, 0, 0))
    out = pl.BlockSpec((1, 1, FFT_N2, half), lambda l, k: (l, k, 0, 0))
    return pl.pallas_call(
        functools.partial(_fft_s2f_kernel, spectrum=True, n_batch=1),
        grid=(depth, r),
        in_specs=[res(0), res(r), g, g, pl.BlockSpec((1, 1, half), lambda l, k: (l, 0, 0))],
        out_specs=[out, out],
        out_shape=[jax.ShapeDtypeStruct((depth, r, FFT_N2, half), F32)] * 2,
        compiler_params=_cparams("parallel", "parallel"),
        name="fft_spec",
    )(a, a, fc["gc"], fc["gs"], ssq)


def _hy_long_conv_fft(fc, u, gate, hre, him, skip, order):
    length, cols = u.shape
    n_batch = cols // HY_CH
    r, n1, nb = fc["r"], fc["n1"], fc["nb"]
    a = _fft_stage1_rows(fc, u[None])
    res = lambda off: pl.BlockSpec((1, 1, FFT_N2, cols), lambda k: (0, k + off, 0, 0))
    g = pl.BlockSpec((1, FFT_N2, FFT_N2), lambda k: (k, 0, 0))
    spec = pl.BlockSpec((1, FFT_N2, HY_CH), lambda k: (k, 0, order))
    blk = pl.BlockSpec((1, FFT_N2, cols), lambda k: (k, 0, 0))
    yr, yi = pl.pallas_call(
        functools.partial(_fft_s2f_kernel, spectrum=False, n_batch=n_batch),
        grid=(r,),
        in_specs=[res(0), res(r), g, g, spec, spec],
        out_specs=[blk, blk],
        out_shape=[jax.ShapeDtypeStruct((r, FFT_N2, cols), BF16)] * 2,
        compiler_params=_cparams("parallel"),
        name="fft_s2_fwd",
    )(a, a, fc["gc"], fc["gs"], hre, him)
    br, bi = pl.pallas_call(
        _fft_s2i_kernel,
        grid=(r,),
        in_specs=[blk, blk, g, g],
        out_specs=[blk, blk],
        out_shape=[jax.ShapeDtypeStruct((r, FFT_N2, cols), BF16)] * 2,
        compiler_params=_cparams("parallel"),
        name="fft_s2_inv",
    )(yr, yi, fc["gct"], fc["gst"])
    tr = FFT_ROWS
    rows = lambda n: pl.BlockSpec((n, tr, cols), lambda i: (0, i, 0))
    y = pl.pallas_call(
        functools.partial(_fft_s1i_rows_kernel, n1=n1, nb=nb, n_batch=n_batch),
        grid=(FFT_N2 // tr,),
        in_specs=[rows(r), rows(r), rows(nb), rows(nb), pl.BlockSpec((1, HY_CH), lambda i: (0, 0))],
        out_specs=rows(nb),
        out_shape=jax.ShapeDtypeStruct((nb, FFT_N2, cols), F32),
        compiler_params=_cparams("parallel"),
        name="fft_s1_inv",
    )(br, bi, u.reshape(nb, FFT_N2, cols), gate.reshape(nb, FFT_N2, cols), skip[order][None, :])
    return y.reshape(length, cols)


def _hyena_fft(z, conv_w, fc, hre, him, skip):
    v, _, x1, x2 = _hy_prep(z, conv_w)
    y1 = _hy_long_conv_fft(fc, v, x1, hre, him, skip, 0)
    return _hy_long_conv_fft(fc, y1, x2, hre, him, skip, 1)


def _attn_kernel(sink_ref, q_ref, *rest, local, n_blocks):
    if local:
        kp_ref, kc_ref, kn_ref, vp_ref, vc_ref, vn_ref, kx_ref, vx_ref, o_ref = rest
        k_all = jnp.concatenate([kp_ref[0], kc_ref[0], kn_ref[0], kx_ref[0]], axis=0)
        v_all = jnp.concatenate([vp_ref[0], vc_ref[0], vn_ref[0], vx_ref[0]], axis=0)
    else:
        kx_ref, vx_ref, o_ref = rest
        k_all, v_all = kx_ref[0], vx_ref[0]
    blk = pl.program_id(1)
    q = q_ref[0]
    tq = q.shape[0]
    nk = k_all.shape[0]
    k16, v16 = k_all.astype(BF16), v_all.astype(BF16)
    lane = lax.broadcasted_iota(jnp.int32, (tq, LANES), 1)
    rows = SW_GROUP * tq
    r_idx = lax.broadcasted_iota(jnp.int32, (rows, nk), 0)
    if local:
        r = r_idx & (tq - 1)
        c = lax.broadcasted_iota(jnp.int32, (rows, nk), 1)
        rel = c - r - SW_BLOCK
        valid = (rel >= -SW_BLOCK) & (rel <= SW_BLOCK)
        valid = valid & ((c >= SW_BLOCK) | (blk > 0)) & ((c < 2 * SW_BLOCK) | (blk < n_blocks - 1))
        valid = valid | (c >= 3 * SW_BLOCK)
    out_heads = [None] * SW_Q_HEADS
    for j in range(SW_KV_HEADS):
        keep = (lane >= j * HEAD_DIM) & (lane < (j + 1) * HEAD_DIM)
        parts = []
        for g in range(SW_GROUP):
            hq = j * SW_GROUP + g
            chunk = q[:, (hq // 2) * LANES:(hq // 2 + 1) * LANES]
            if hq % 2 != j:
                chunk = pltpu.roll(chunk, HEAD_DIM, 1)
            parts.append(jnp.where(keep, chunk, 0.0))
        qz = jnp.concatenate(parts, axis=0).astype(BF16)
        s = _dot_nt(qz, k16)
        if local:
            s = jnp.where(valid, s, NEG_INF)
        sink = jnp.where(r_idx[:, 0:1] < tq, sink_ref[j * SW_GROUP],
                         jnp.where(r_idx[:, 0:1] < 2 * tq, sink_ref[j * SW_GROUP + 1], sink_ref[j * SW_GROUP + 2]))
        m = jnp.maximum(jnp.max(s, axis=-1, keepdims=True), sink)
        p = jnp.exp(s - m)
        denom = jnp.sum(p, axis=-1, keepdims=True) + jnp.exp(sink - m)
        o = _dot(p.astype(BF16), v16) / denom
        for g in range(SW_GROUP):
            hq = j * SW_GROUP + g
            og = o[g * tq:(g + 1) * tq]
            if hq % 2 != j:
                og = pltpu.roll(og, HEAD_DIM, 1)
            out_heads[hq] = og
    first_half = lane < HEAD_DIM
    o_ref[0] = jnp.concatenate(
        [jnp.where(first_half, out_heads[2 * c], out_heads[2 * c + 1]) for c in range(SW_Q_HEADS // 2)], axis=1)


def _attention(q, k, v, kx, vx, sink, local):
    bn, t, _ = q.shape
    nb = t // SW_BLOCK
    tx = kx.shape[1]
    qspec = pl.BlockSpec((1, SW_BLOCK, SW_WIDTH), lambda b, i: (b, i, 0))
    xspec = pl.BlockSpec((1, tx, SW_KV_WIDTH), lambda b, i: (b, 0, 0))
    in_specs = [pl.BlockSpec(memory_space=pltpu.SMEM), qspec]
    args = [sink, q]
    if local:
        prev = pl.BlockSpec((1, SW_BLOCK, SW_KV_WIDTH), lambda b, i: (b, jnp.maximum(i - 1, 0), 0))
        cur = pl.BlockSpec((1, SW_BLOCK, SW_KV_WIDTH), lambda b, i: (b, i, 0))
        nxt = pl.BlockSpec((1, SW_BLOCK, SW_KV_WIDTH), lambda b, i: (b, jnp.minimum(i + 1, nb - 1), 0))
        in_specs += [prev, cur, nxt, prev, cur, nxt]
        args += [k, k, k, v, v, v]
    in_specs += [xspec, xspec]
    args += [kx, vx]
    return pl.pallas_call(
        functools.partial(_attn_kernel, local=local, n_blocks=nb),
        grid=(bn, nb),
        in_specs=in_specs,
        out_specs=qspec,
        out_shape=jax.ShapeDtypeStruct((bn, t, SW_WIDTH), F32),
        compiler_params=_cparams("parallel", "parallel"),
        name="attn_local" if local else "attn_ctx",
    )(*args)


def _out_proj_kernel(of_ref, ob_ref, z_ref, gn_ref, gsum_ref, hy_ref, sw_ref, x_ref, gate_ref, w_ref, o_ref):
    o = of_ref[0] + ob_ref[0]
    mean = _dot2_exact_rhs(o * o, gsum_ref[...]) * (1.0 / HEAD_DIM)
    dn = o * lax.rsqrt(mean + RMS_EPS) * gn_ref[...] * _silu(z_ref[0])
    a, b = DN_WIDTH, DN_WIDTH + HY_CH
    mix = _dot(dn.astype(BF16), w_ref[0:a, :])
    mix += _dot(hy_ref[...].astype(BF16), w_ref[a:b, :])
    mix += _dot(sw_ref[0].astype(BF16), w_ref[b:, :])
    o_ref[0] = x_ref[0] + gate_ref[0] * mix


def _out_proj(o_f, o_b, z, gn, gsum, hy, sw, x, gate, w):
    bn, t, d = x.shape
    tm = min(512, t)
    seq = lambda n: pl.BlockSpec((1, tm, n), lambda b, i: (b, i, 0))
    const = lambda shape: pl.BlockSpec(shape, lambda b, i: (0,) * len(shape))
    return pl.pallas_call(
        _out_proj_kernel,
        grid=(bn, t // tm),
        in_specs=[seq(DN_WIDTH), seq(DN_WIDTH), seq(DN_WIDTH), const(gn.shape), const(gsum.shape),
                  pl.BlockSpec((tm, HY_CH), lambda b, i: (i, b)), seq(SW_WIDTH), seq(d),
                  pl.BlockSpec((1, 1, d), lambda b, i: (b, 0, 0)), const(w.shape)],
        out_specs=seq(d),
        out_shape=jax.ShapeDtypeStruct(x.shape, F32),
        compiler_params=_cparams("parallel", "parallel"),
        name="out_proj",
    )(o_f, o_b, z, gn, gsum, hy, sw, x, gate, w)


def _ffn_kernel(x_ref, g_ref, sh_ref, sc_ref, gate_ref, wg_ref, wu_ref, wd_ref, o_ref, h_ref, acc_ref):
    j = pl.program_id(2)

    @pl.when(j == 0)
    def _():
        h_ref[...] = _rms_mod(x_ref[0], g_ref[...], sh_ref[0], sc_ref[0]).astype(BF16)
        acc_ref[...] = jnp.zeros_like(acc_ref)

    h = h_ref[...]
    act = _silu(_dot(h, wg_ref[...])) * _dot(h, wu_ref[...])
    acc_ref[...] += _dot(act.astype(BF16), wd_ref[...])

    @pl.when(j == pl.num_programs(2) - 1)
    def _():
        o_ref[0] = x_ref[0] + gate_ref[0] * acc_ref[...]


def _ffn(x, g, shift, scale, gate, wg, wu, wd):
    bn, t, d = x.shape
    ff = wg.shape[1]
    tm = min(512, t)
    tf = ff // 2
    seq = pl.BlockSpec((1, tm, d), lambda b, i, j: (b, i, 0))
    mod = pl.BlockSpec((1, 1, d), lambda b, i, j: (b, 0, 0))
    return pl.pallas_call(
        _ffn_kernel,
        grid=(bn, t // tm, ff // tf),
        in_specs=[seq, pl.BlockSpec((1, d), lambda b, i, j: (0, 0)), mod, mod, mod,
                  pl.BlockSpec((d, tf), lambda b, i, j: (0, j)),
                  pl.BlockSpec((d, tf), lambda b, i, j: (0, j)),
                  pl.BlockSpec((tf, d), lambda b, i, j: (j, 0))],
        out_specs=seq,
        out_shape=jax.ShapeDtypeStruct(x.shape, F32),
        scratch_shapes=[pltpu.VMEM((tm, d), BF16), pltpu.VMEM((tm, d), F32)],
        compiler_params=_cparams("parallel", "parallel", "arbitrary"),
        name="ffn_dense",
    )(x, g, shift, scale, gate, wg, wu, wd)


MOE_TILE = 512
_INFO_E, _INFO_W, _INFO_RANK = 0, 2, 4


def _route_kernel(x_ref, g_ref, sh_ref, sc_ref, r_ref, h_ref, info_ref, cnt_ref, carry_ref):
    @pl.when((pl.program_id(0) == 0) & (pl.program_id(1) == 0))
    def _():
        carry_ref[...] = jnp.zeros_like(carry_ref)

    h = _rms_mod(x_ref[0], g_ref[...], sh_ref[0], sc_ref[0])
    h_ref[0] = h
    logits = _dot3(h, r_ref[...])
    tm = logits.shape[0]
    lane = lax.broadcasted_iota(jnp.int32, logits.shape, 1)
    lg = jnp.where(lane < N_EXPERTS, logits, NEG_INF)
    m1 = jnp.max(lg, axis=-1, keepdims=True)
    i1 = jnp.min(jnp.where(lg == m1, lane, LANES), axis=-1, keepdims=True)
    lg2 = jnp.where(lane == i1, NEG_INF, lg)
    m2 = jnp.max(lg2, axis=-1, keepdims=True)
    i2 = jnp.min(jnp.where(lg2 == m2, lane, LANES), axis=-1, keepdims=True)
    e2 = jnp.exp(m2 - m1)
    w1 = 1.0 / (1.0 + e2)
    w2 = e2 * w1
    chosen = jnp.where((lane == i1) | (lane == i2), 1.0, 0.0)
    rr = lax.broadcasted_iota(jnp.int32, (tm, tm), 0)
    cc = lax.broadcasted_iota(jnp.int32, (tm, tm), 1)
    earlier = jnp.where(rr > cc, 1.0, 0.0).astype(BF16)
    before = _dot(earlier, chosen.astype(BF16)) + carry_ref[...]
    rank1 = jnp.sum(jnp.where(lane == i1, before, 0.0), axis=-1, keepdims=True)
    rank2 = jnp.sum(jnp.where(lane == i2, before, 0.0), axis=-1, keepdims=True)
    info = jnp.zeros_like(logits)
    for pos, val in ((_INFO_E, i1.astype(F32)), (_INFO_E + 1, i2.astype(F32)), (_INFO_W, w1), (_INFO_W + 1, w2),
                     (_INFO_RANK, rank1), (_INFO_RANK + 1, rank2)):
        info = jnp.where(lane == pos, val, info)
    info_ref[0] = info
    carry_ref[...] += jnp.sum(chosen, axis=0, keepdims=True)
    cnt_ref[...] = carry_ref[...]


def _row_copies(src_row, dst_row, sem, tm, wait):
    def body(r, carry):
        for c in range(2):
            cp = pltpu.make_async_copy(src_row(r, c), dst_row(r, c), sem.at[c])
            if wait:
                cp.wait()
            else:
                cp.start()
        return carry
    lax.fori_loop(0, tm, body, 0, unroll=8)


def _scatter_kernel(dest_ref, h_ref, xs_in_ref, xs_ref, sem):
    del xs_in_ref
    tm = h_ref.shape[1]
    src = lambda r, c: h_ref.at[0, pl.ds(r, 1)]
    dst = lambda r, c: xs_ref.at[pl.ds(dest_ref[0, 0, c * tm + r], 1)]
    _row_copies(src, dst, sem, tm, wait=False)
    _row_copies(src, dst, sem, tm, wait=True)


def _group_ffn_kernel(te_ref, tb_ref, nv_ref, xs_ref, wg_ref, wu_ref, wd_ref, ys_ref, x16_ref, acc_ref):
    j = pl.program_id(0)
    f = pl.program_id(1)

    @pl.when(j < nv_ref[0])
    def _():
        @pl.when(f == 0)
        def _():
            x16_ref[...] = xs_ref[...].astype(BF16)
            acc_ref[...] = jnp.zeros_like(acc_ref)

        h = x16_ref[...]
        act = _silu(_dot(h, wg_ref[0])) * _dot(h, wu_ref[0])
        acc_ref[...] += _dot(act.astype(BF16), wd_ref[0])

        @pl.when(f == pl.num_programs(1) - 1)
        def _():
            ys_ref[...] = acc_ref[...]

    @pl.when(j >= nv_ref[0])
    def _():
        ys_ref[...] = jnp.zeros_like(ys_ref)


def _combine_kernel(dest_ref, x_ref, gate_ref, info_ref, ys_ref, o_ref, buf_ref, sem):
    tm = x_ref.shape[1]
    src = lambda r, c: ys_ref.at[pl.ds(dest_ref[0, 0, c * tm + r], 1)]
    dst = lambda r, c: buf_ref.at[c, pl.ds(r, 1)]
    _row_copies(src, dst, sem, tm, wait=False)
    _row_copies(src, dst, sem, tm, wait=True)
    info = info_ref[0]
    mix = info[:, _INFO_W:_INFO_W + 1] * buf_ref[0] + info[:, _INFO_W + 1:_INFO_W + 2] * buf_ref[1]
    o_ref[0] = x_ref[0] + gate_ref[0] * mix


def _moe(x, g, shift, scale, gate, router, wg, wu, wd):
    bn, t, d = x.shape
    ne, _, ff = wg.shape
    tm = MOE_TILE
    nt = t // tm
    n_tok = bn * t
    n_slots = 2 * n_tok // tm + ne
    seq = pl.BlockSpec((1, tm, d), lambda b, i: (b, i, 0))
    mod = pl.BlockSpec((1, 1, d), lambda b, i: (b, 0, 0))
    rec = pl.BlockSpec((1, tm, LANES), lambda b, i: (b, i, 0))
    h, info, counts = pl.pallas_call(
        _route_kernel,
        grid=(bn, nt),
        in_specs=[seq, pl.BlockSpec((1, d), lambda b, i: (0, 0)), mod, mod,
                  pl.BlockSpec(router.shape, lambda b, i: (0, 0))],
        out_specs=[seq, rec, pl.BlockSpec((1, LANES), lambda b, i: (0, 0))],
        out_shape=[jax.ShapeDtypeStruct(x.shape, F32), jax.ShapeDtypeStruct((bn, t, LANES), F32),
                   jax.ShapeDtypeStruct((1, LANES), F32)],
        scratch_shapes=[pltpu.VMEM((1, LANES), F32)],
        compiler_params=_cparams("arbitrary", "arbitrary"),
        name="moe_route",
    )(x, g, shift, scale, router)

    flat = info.reshape(n_tok, LANES)
    expert = flat[:, _INFO_E:_INFO_E + 2].astype(jnp.int32)
    rank = flat[:, _INFO_RANK:_INFO_RANK + 2].astype(jnp.int32)
    tiles_e = (counts[0, :ne].astype(jnp.int32) + tm - 1) // tm
    ends = jnp.cumsum(tiles_e)
    dest = (ends - tiles_e)[expert] * tm + rank
    dest = dest.reshape(bn * nt, tm, 2).transpose(0, 2, 1).reshape(bn * nt, 1, 2 * tm)
    n_valid = ends[-1]
    slot = jnp.minimum(jnp.arange(n_slots, dtype=jnp.int32), n_valid - 1)
    slot_expert = jnp.minimum(jnp.searchsorted(ends, slot, side="right"), ne - 1).astype(jnp.int32)

    dspec = pl.BlockSpec((1, 1, 2 * tm), lambda b, i: (b * nt + i, 0, 0), memory_space=pltpu.SMEM)
    anyspec = pl.BlockSpec(memory_space=pl.ANY)
    xs = pl.pallas_call(
        _scatter_kernel,
        grid=(bn, nt),
        in_specs=[dspec, seq, anyspec],
        out_specs=anyspec,
        out_shape=jax.ShapeDtypeStruct((n_slots * tm, d), F32),
        scratch_shapes=[pltpu.SemaphoreType.DMA((2,))],
        input_output_aliases={2: 0},
        compiler_params=_cparams("arbitrary", "arbitrary"),
        name="moe_scatter",
    )(dest, h, jnp.zeros((n_slots * tm, d), F32))

    nf = 2
    tf = ff // nf
    last = nf - 1
    fidx = lambda j, f, nv: jnp.where(j < nv[0], f, last)
    ys = pl.pallas_call(
        _group_ffn_kernel,
        grid_spec=pltpu.PrefetchScalarGridSpec(
            num_scalar_prefetch=3,
            grid=(n_slots, nf),
            in_specs=[pl.BlockSpec((tm, d), lambda j, f, te, tb, nv: (tb[j], 0)),
                      pl.BlockSpec((1, d, tf), lambda j, f, te, tb, nv: (te[j], 0, fidx(j, f, nv))),
                      pl.BlockSpec((1, d, tf), lambda j, f, te, tb, nv: (te[j], 0, fidx(j, f, nv))),
                      pl.BlockSpec((1, tf, d), lambda j, f, te, tb, nv: (te[j], fidx(j, f, nv), 0))],
            out_specs=pl.BlockSpec((tm, d), lambda j, f, te, tb, nv: (j, 0)),
            scratch_shapes=[pltpu.VMEM((tm, d), BF16), pltpu.VMEM((tm, d), F32)]),
        out_shape=jax.ShapeDtypeStruct((n_slots * tm, d), F32),
        compiler_params=_cparams("arbitrary", "arbitrary"),
        name="moe_group_ffn",
    )(slot_expert, slot, n_valid.reshape(1), xs, wg, wu, wd)

    return pl.pallas_call(
        _combine_kernel,
        grid=(bn, nt),
        in_specs=[dspec, seq, mod, rec, anyspec],
        out_specs=seq,
        out_shape=jax.ShapeDtypeStruct(x.shape, F32),
        scratch_shapes=[pltpu.VMEM((2, tm, d), F32), pltpu.SemaphoreType.DMA((2,))],
        compiler_params=_cparams("arbitrary", "arbitrary"),
        name="moe_combine",
    )(dest, x, gate, info, ys)


def _final_norm_kernel(x_ref, g_ref, o_ref):
    x = x_ref[0]
    o_ref[0] = x * lax.rsqrt(jnp.mean(x * x, axis=-1, keepdims=True) + RMS_EPS) * g_ref[...]


def _final_norm(x, g):
    bn, t, d = x.shape
    tm = min(1024, t)
    seq = pl.BlockSpec((1, tm, d), lambda b, i: (b, i, 0))
    return pl.pallas_call(
        _final_norm_kernel,
        grid=(bn, t // tm),
        in_specs=[seq, pl.BlockSpec((1, d), lambda b, i: (0, 0))],
        out_specs=seq,
        out_shape=jax.ShapeDtypeStruct(x.shape, F32),
        compiler_params=_cparams("parallel", "parallel"),
        name="final_norm",
    )(x, g)


def _head_sum_matrix():
    idx = np.arange(DN_WIDTH) // HEAD_DIM
    return jnp.asarray(idx[:, None] == idx[None, :], dtype=BF16)


def kernel(x, c, ctx, c_ctx, w_mod, b_mod, norm_mix_g, norm_ffn_g, w_in, dn_conv_w, dn_a_log, dn_dt_bias, dn_norm_g, hy_conv_w, hy_w1, hy_b1, hy_freq1, hy_w2, hy_b2, hy_freq2, hy_w3, hy_b3, hy_skip, sw_sink, w_out, ffn_w_gate, ffn_w_up, ffn_w_down, moe_router, moe_w_gate, moe_w_up, moe_w_down, final_norm_g):
    bn, seq_len, d = x.shape
    ctx_len = ctx.shape[1]
    depth = w_mod.shape[0]

    c_rows = jnp.concatenate([c, c_ctx[None, :], jnp.zeros((SUBLANES - bn - 1, d), F32)], axis=0)
    mods = _modulation(c_rows, w_mod, b_mod).reshape(depth, SUBLANES, N_MOD, d)

    rope_tabs = _rope_tables(seq_len)
    gsum = _head_sum_matrix()
    cm_c, sm_c = _dft_matrices(ctx_len)
    hy_params = (hy_w1, hy_b1, hy_freq1, hy_w2, hy_b2, hy_freq2, hy_w3, hy_b3)
    fft_x = _fft_consts(seq_len)
    taps_x = _hy_taps(seq_len, *hy_params)
    hre_x, him_x = _hy_spec_fft(fft_x, taps_x[4], taps_x[2])
    hre_c, him_c = _hy_spec_dense(cm_c, sm_c, *_hy_taps(ctx_len, *hy_params)[:4])
    router = jnp.pad(moe_router, ((0, 0), (0, 0), (0, LANES - N_EXPERTS)))
    zero_state = jnp.zeros((bn, N_DIR, DN_HEADS, HEAD_DIM, HEAD_DIM), F32)

    for layer in range(depth):
        last = layer == depth - 1
        mod_x = [mods[layer, :bn, m][:, None, :] for m in range(N_MOD)]
        mod_c = [jnp.broadcast_to(mods[layer, bn, m][None, None, :], (bn, 1, d)) for m in range(N_MOD)]
        g_mix = norm_mix_g[layer][None, :]
        g_ffn = norm_ffn_g[layer][None, :]
        w_in_l = _relayout_w_in(w_in[layer])
        w_out_l = w_out[layer].astype(BF16)
        gn = jnp.tile(dn_norm_g[layer], DN_HEADS)[None, :]

        qkv_x, z_x, hyp_x, swq_x, swk_x, swv_x, ab_x = _in_proj(x, g_mix, mod_x[0], mod_x[1], w_in_l, rope_tabs)
        qkv_c, z_c, hyp_c, swq_c, swk_c, swv_c, ab_c = _in_proj(ctx, g_mix, mod_c[0], mod_c[1], w_in_l, None)

        dn_in_c = _dn_prep(qkv_c, ab_c, dn_conv_w[layer], dn_a_log[layer], dn_dt_bias[layer], gsum)
        dn_in_x = _dn_prep(qkv_x, ab_x, dn_conv_w[layer], dn_a_log[layer], dn_dt_bias[layer], gsum)
        of_c, ob_c, state_c = _dn_scan(*dn_in_c, zero_state)
        of_x, ob_x, _ = _dn_scan(*dn_in_x, state_c)

        hy_x = _hyena_fft(hyp_x, hy_conv_w[layer], fft_x, hre_x[layer], him_x[layer], hy_skip[layer])
        sw_x = _attention(swq_x, swk_x, swv_x, swk_c, swv_c, sw_sink[layer], True)
        x = _out_proj(of_x, ob_x, z_x, gn, gsum, hy_x, sw_x, x, mod_x[2], w_out_l)

        if not last:
            hy_c = _hyena(hyp_c, hy_conv_w[layer], cm_c, sm_c, hre_c[layer], him_c[layer], hy_skip[layer])
            sw_c = _attention(swq_c, None, None, swk_c, swv_c, sw_sink[layer], False)
            ctx = _out_proj(of_c, ob_c, z_c, gn, gsum, hy_c, sw_c, ctx, mod_c[2], w_out_l)

        i = layer // 2
        streams = [(x, mod_x)] if last else [(x, mod_x), (ctx, mod_c)]
        outs = []
        for s, mod in streams:
            if layer % 2 == 0:
                outs.append(_ffn(s, g_ffn, mod[3], mod[4], mod[5], ffn_w_gate[i].astype(BF16),
                                 ffn_w_up[i].astype(BF16), ffn_w_down[i].astype(BF16)))
            else:
                shape = s.shape
                if mod is mod_c:
                    s = s.reshape(1, -1, d)
                    mod = [m[:1] for m in mod]
                outs.append(_moe(s, g_ffn, mod[3], mod[4], mod[5], router[i], moe_w_gate[i].astype(BF16),
                                 moe_w_up[i].astype(BF16), moe_w_down[i].astype(BF16)).reshape(shape))
        x = outs[0]
        if not last:
            ctx = outs[1]

    return _final_norm(x, final_norm_g[None, :])
```

```python
import functools
import math

import jax
import jax.numpy as jnp
import numpy as np
from jax import lax
from jax.experimental import pallas as pl
from jax.experimental.pallas import tpu as pltpu

F32 = jnp.float32
BF16 = jnp.bfloat16

D_MODEL = 1024
DEPTH = 4
GRID_W = 64
N_MOD = 6
HEAD_DIM = 64
N_DIR = 2
RMS_EPS = 1e-6

DN_HEADS = 6
DN_WIDTH = DN_HEADS * HEAD_DIM
DN_CHUNK = 64
DN_BLOCK = 8
DN_SUB = 2

SW_Q_HEADS = 6
SW_KV_HEADS = 2
SW_GROUP = SW_Q_HEADS // SW_KV_HEADS
SW_WIDTH = SW_Q_HEADS * HEAD_DIM
SW_KV_WIDTH = SW_KV_HEADS * HEAD_DIM
SW_BLOCK = 128
ROPE_THETA = 10000.0
NEG_INF = -1e30

HY_CH = D_MODEL - DN_WIDTH - SW_WIDTH
HY_ORDER = 2
HY_BANDS = 16
HY_EMB = 1 + 2 * HY_BANDS
HY_FFN = 64
HY_FAST_DECAY = 0.3
HY_SLOW_DECAY = 1.5
HY_TARGET = 1e-2

N_EXPERTS = 8

LANES = 128
SUBLANES = 8
VMEM_LIMIT = 56 * 1024 * 1024

_IN_SIZES = (3 * DN_WIDTH, DN_WIDTH, N_DIR * DN_HEADS, N_DIR * DN_HEADS, 3 * HY_CH, SW_WIDTH, SW_KV_WIDTH, SW_KV_WIDTH)
_IN_OFF = tuple(int(v) for v in np.cumsum((0,) + _IN_SIZES))
_SEGS = (3 * DN_WIDTH, DN_WIDTH, 3 * HY_CH, SW_WIDTH, SW_KV_WIDTH, SW_KV_WIDTH, LANES)
_SEG_Q, _SEG_K = 3, 4


def _cparams(*sem):
    return pltpu.CompilerParams(dimension_semantics=sem, vmem_limit_bytes=VMEM_LIMIT)


def _dot(a, b):
    return jnp.dot(a, b, preferred_element_type=F32)


def _dot_nt(a, b):
    return lax.dot_general(a, b, (((1,), (1,)), ((), ())), preferred_element_type=F32)


def _split(x):
    hi = x.astype(BF16)
    lo = (x - hi.astype(F32)).astype(BF16)
    return hi, lo


def _dot3(a, b):
    ah, al = _split(a)
    bh, bl = _split(b)
    return _dot(ah, bh) + (_dot(al, bh) + _dot(ah, bl))


def _dot2_exact_rhs(a, b_bf16):
    ah, al = _split(a)
    return _dot(ah, b_bf16) + _dot(al, b_bf16)


def _sigmoid(x):
    return 1.0 / (1.0 + jnp.exp(-x))


def _silu(x):
    return x * _sigmoid(x)


def _softplus(x):
    return jnp.maximum(x, 0.0) + jnp.log(1.0 + jnp.exp(-jnp.abs(x)))


def _rms_mod(x, g, shift, scale):
    y = x * lax.rsqrt(jnp.mean(x * x, axis=-1, keepdims=True) + RMS_EPS)
    return (y * g) * (1.0 + scale) + shift


def _mod_kernel(c_ref, w_ref, b_ref, o_ref):
    o_ref[0] = _dot3(_silu(c_ref[...]), w_ref[0]) + b_ref[0]


def _modulation(c_rows, w_mod, b_mod):
    depth, d, n = w_mod.shape
    rows = c_rows.shape[0]
    tn = 1024
    return pl.pallas_call(
        _mod_kernel,
        grid=(depth, n // tn),
        in_specs=[
            pl.BlockSpec((rows, d), lambda l, j: (0, 0)),
            pl.BlockSpec((1, d, tn), lambda l, j: (l, 0, j)),
            pl.BlockSpec((1, 1, tn), lambda l, j: (l, 0, j)),
        ],
        out_specs=pl.BlockSpec((1, rows, tn), lambda l, j: (l, 0, j)),
        out_shape=jax.ShapeDtypeStruct((depth, rows, n), F32),
        compiler_params=_cparams("parallel", "parallel"),
        name="adaln_mod",
    )(c_rows, w_mod, b_mod.reshape(depth, 1, n))


def _rope_apply(x, cos, sin_signed):
    lane = lax.broadcasted_iota(jnp.int32, x.shape, 1)
    first = (lane & 31) < 16
    partner = jnp.where(first, pltpu.roll(x, LANES - 16, 1), pltpu.roll(x, 16, 1))
    return x * cos + partner * sin_signed


def _in_proj_kernel(x_ref, g_ref, sh_ref, sc_ref, w_ref, *rest, rope):
    if rope:
        cos_ref, sin_ref = rest[:2]
        outs = rest[2:]
    else:
        outs = rest
    h = _rms_mod(x_ref[0], g_ref[...], sh_ref[0], sc_ref[0]).astype(BF16)
    off = 0
    for idx, (o_ref, n) in enumerate(zip(outs, _SEGS)):
        r = _dot(h, w_ref[:, off:off + n])
        if idx == _SEG_Q:
            r = r * (HEAD_DIM ** -0.5)
        if rope and idx in (_SEG_Q, _SEG_K):
            cos, sin = cos_ref[...], sin_ref[...]
            r = jnp.concatenate(
                [_rope_apply(r[:, c:c + LANES], cos, sin) for c in range(0, n, LANES)], axis=1)
        o_ref[0] = r
        off += n


def _in_proj(x, g, shift, scale, w, rope_tabs):
    bn, t, d = x.shape
    tm = min(512, t)
    rope = rope_tabs is not None
    in_specs = [
        pl.BlockSpec((1, tm, d), lambda b, i: (b, i, 0)),
        pl.BlockSpec((1, d), lambda b, i: (0, 0)),
        pl.BlockSpec((1, 1, d), lambda b, i: (b, 0, 0)),
        pl.BlockSpec((1, 1, d), lambda b, i: (b, 0, 0)),
        pl.BlockSpec(w.shape, lambda b, i: (0, 0)),
    ]
    args = [x, g, shift, scale, w]
    if rope:
        in_specs += [pl.BlockSpec((tm, LANES), lambda b, i: (i, 0))] * 2
        args += list(rope_tabs)
    return pl.pallas_call(
        functools.partial(_in_proj_kernel, rope=rope),
        grid=(bn, t // tm),
        in_specs=in_specs,
        out_specs=[pl.BlockSpec((1, tm, n), lambda b, i: (b, i, 0)) for n in _SEGS],
        out_shape=[jax.ShapeDtypeStruct((bn, t, n), F32) for n in _SEGS],
        compiler_params=_cparams("parallel", "parallel"),
        name="in_proj_rope" if rope else "in_proj",
    )(*args)


def _relayout_w_in(w_in):
    o = _IN_OFF
    d = w_in.shape[0]
    pad = jnp.zeros((d, LANES - 2 * N_DIR * DN_HEADS), w_in.dtype)
    cols = [w_in[:, o[0]:o[2]], w_in[:, o[4]:o[8]], w_in[:, o[2]:o[4]], pad]
    return jnp.concatenate(cols, axis=1).astype(BF16)


def _rope_tables(length):
    n = HEAD_DIM // 4
    inv = jnp.power(ROPE_THETA, -jnp.arange(n, dtype=F32) / n)
    t = jnp.arange(length)
    row = (t // GRID_W).astype(F32)[:, None] * inv[None, :]
    col = (t % GRID_W).astype(F32)[:, None] * inv[None, :]
    cos = jnp.concatenate([jnp.cos(row), jnp.cos(row), jnp.cos(col), jnp.cos(col)], axis=1)
    sin = jnp.concatenate([-jnp.sin(row), jnp.sin(row), -jnp.sin(col), jnp.sin(col)], axis=1)
    return jnp.tile(cos, (1, 2)), jnp.tile(sin, (1, 2))


def _conv3_rows(x, prev8, next8, w, first, last):
    tm = x.shape[0]
    row = lax.broadcasted_iota(jnp.int32, x.shape, 0)
    before = jnp.where(first, 0.0, prev8[SUBLANES - 1:SUBLANES, :])
    after = jnp.where(last, 0.0, next8[0:1, :])
    xm = jnp.where(row == 0, before, pltpu.roll(x, 1, 0))
    xp = jnp.where(row == tm - 1, after, pltpu.roll(x, tm - 1, 0))
    return xm * w[0:1, :] + x * w[1:2, :] + xp * w[2:3, :]


def _halo_specs(tm, t, width):
    nb8 = t // SUBLANES
    step = tm // SUBLANES
    return [
        pl.BlockSpec((1, tm, width), lambda b, i: (b, i, 0)),
        pl.BlockSpec((1, SUBLANES, width), lambda b, i: (b, jnp.maximum(i * step - 1, 0), 0)),
        pl.BlockSpec((1, SUBLANES, width), lambda b, i: (b, jnp.minimum((i + 1) * step, nb8 - 1), 0)),
    ]


def _dn_prep_kernel(x_ref, xp_ref, xn_ref, w_ref, ab_ref, al_ref, dt_ref, gsum_ref,
                    q_ref, k_ref, v_ref, g_ref, beta_ref):
    i = pl.program_id(1)
    y = _silu(_conv3_rows(x_ref[0], xp_ref[0], xn_ref[0], w_ref[...], i == 0, i == pl.num_programs(1) - 1))
    q, k, v = y[:, :DN_WIDTH], y[:, DN_WIDTH:2 * DN_WIDTH], y[:, 2 * DN_WIDTH:]
    gs = gsum_ref[...]
    q = q * lax.rsqrt(_dot2_exact_rhs(q * q, gs) + RMS_EPS) * (HEAD_DIM ** -0.5)
    k = k * lax.rsqrt(_dot2_exact_rhs(k * k, gs) + RMS_EPS)
    for h in range(DN_HEADS):
        sl = slice(h * HEAD_DIM, (h + 1) * HEAD_DIM)
        q_ref[0, h] = q[:, sl]
        k_ref[0, h] = k[:, sl]
        v_ref[0, h] = v[:, sl]
    ab = ab_ref[0]
    lane = lax.broadcasted_iota(jnp.int32, ab.shape, 1)
    nh = N_DIR * DN_HEADS
    g_ref[0] = jnp.where(lane < nh, -jnp.exp(al_ref[...]) * _softplus(ab + dt_ref[...]), 0.0)
    beta_ref[0] = jnp.where(lane < nh, _sigmoid(pltpu.roll(ab, LANES - nh, 1)), 0.0)


def _dn_prep(qkv, ab, conv_w, a_log, dt_bias, gsum):
    bn, t, width = qkv.shape
    tm = min(512, t)
    pad = LANES - N_DIR * DN_HEADS
    al = jnp.pad(a_log.reshape(1, -1), ((0, 0), (0, pad)))
    dt = jnp.pad(dt_bias.reshape(1, -1), ((0, 0), (0, pad)))
    head = pl.BlockSpec((1, DN_HEADS, tm, HEAD_DIM), lambda b, i: (b, 0, i, 0))
    row = pl.BlockSpec((1, tm, LANES), lambda b, i: (b, i, 0))
    const = lambda shape: pl.BlockSpec(shape, lambda b, i: (0,) * len(shape))
    head_shape = jax.ShapeDtypeStruct((bn, DN_HEADS, t, HEAD_DIM), F32)
    row_shape = jax.ShapeDtypeStruct((bn, t, LANES), F32)
    return pl.pallas_call(
        _dn_prep_kernel,
        grid=(bn, t // tm),
        in_specs=_halo_specs(tm, t, width) + [const(conv_w.shape), row, const(al.shape), const(dt.shape),
                                              const(gsum.shape)],
        out_specs=[head, head, head, row, row],
        out_shape=[head_shape, head_shape, head_shape, row_shape, row_shape],
        compiler_params=_cparams("parallel", "parallel"),
        name="dn_prep",
    )(qkv, qkv, qkv, conv_w, ab, al, dt, gsum)


def _cumsum_rows(x, reverse):
    c = x.shape[0]
    row = lax.broadcasted_iota(jnp.int32, x.shape, 0)
    s = 1
    while s < c:
        if reverse:
            x = x + jnp.where(row < c - s, pltpu.roll(x, c - s, 0), 0.0)
        else:
            x = x + jnp.where(row >= s, pltpu.roll(x, s, 0), 0.0)
        s *= 2
    return x


def _mm1(ps, cs):
    return [_dot(p.astype(BF16), c.astype(BF16)) for p, c in zip(ps, cs)]


def _mm3(ps, cs):
    out = []
    for p, c in zip(ps, cs):
        c16 = c.astype(BF16)
        c_lo = (c - c16.astype(F32)).astype(BF16)
        p_hi = p.astype(BF16).astype(F32)
        lhs = jnp.concatenate([p_hi, p - p_hi, p_hi], axis=1).astype(BF16)
        out.append(_dot(lhs, jnp.concatenate([c16, c16, c_lo], axis=0)))
    return out


def _dn_chunks(probs):
    c, d = DN_CHUNK, HEAD_DIM
    ii = lax.broadcasted_iota(jnp.int32, (c, c), 0)
    jj = lax.broadcasted_iota(jnp.int32, (c, c), 1)
    di = lax.broadcasted_iota(jnp.int32, (d, d), 0)
    dj = lax.broadcasted_iota(jnp.int32, (d, d), 1)
    n = len(probs)
    k16 = [pr["k"].astype(BF16) for pr in probs]
    kk = [_dot_nt(k16[i], k16[i]) for i in range(n)]
    qk = [_dot_nt(probs[i]["q"].astype(BF16), k16[i]) for i in range(n)]
    kt = [pr["k"].T for pr in probs]
    decay, e_cum, x, p = [], [], [], []
    for i, pr in enumerate(probs):
        incl = (ii >= jj) if pr["lower"] else (ii <= jj)
        strict = (ii > jj) if pr["lower"] else (ii < jj)
        dec = jnp.where(incl, jnp.exp(jnp.where(incl, pr["cum_col"] - pr["cum_row"], 0.0)), 0.0)
        ec = jnp.exp(pr["cum_col"])
        decay.append(dec)
        e_cum.append(ec)
        x.append(jnp.concatenate([pr["v"] * pr["beta_col"], pr["k"] * pr["beta_col"] * ec], axis=1))
        p.append(jnp.where(strict, -(kk[i] * pr["beta_col"] * dec), 0.0))
    same_blk = (ii // DN_BLOCK) == (jj // DN_BLOCK)
    eye = jnp.where(ii == jj, 1.0, 0.0)
    p0 = [jnp.where(same_blk, p[i], 0.0) for i in range(n)]
    a_off = [jnp.where(same_blk, 0.0, -p[i]) for i in range(n)]
    q = _mm1(p0, p0)
    xd = [eye + p0[i] for i in range(n)]
    lvl = 2
    while lvl < DN_BLOCK:
        last = 2 * lvl >= DN_BLOCK
        r = _mm1(q, xd if last else [jnp.concatenate([xd[i], q[i]], axis=1) for i in range(n)])
        xd = [xd[i] + r[i][:, :c] for i in range(n)]
        if not last:
            q = [r[i][:, c:] for i in range(n)]
        lvl *= 2
    r = _mm1(xd, [jnp.concatenate([x[i], a_off[i]], axis=1) for i in range(n)])
    x = [r[i][:, :2 * d] for i in range(n)]
    q = [-r[i][:, 2 * d:] for i in range(n)]
    lvl = 1
    while lvl < c // DN_BLOCK:
        last = 2 * lvl >= c // DN_BLOCK
        r = _mm3(q, x if last else [jnp.concatenate([x[i], q[i]], axis=1) for i in range(n)])
        x = [x[i] + r[i][:, :2 * d] for i in range(n)]
        if not last:
            q = [r[i][:, 2 * d:] for i in range(n)]
        lvl *= 2
    lhs = [jnp.concatenate([qk[i] * decay[i], kt[i] * jnp.exp(probs[i]["tot"] - probs[i]["cum_row"])], axis=0)
           for i in range(n)]
    r = [_dot(lhs[i].astype(BF16), x[i].astype(BF16)) for i in range(n)]
    lhs = [jnp.concatenate([jnp.where(di == dj, jnp.exp(probs[i]["tot"]), 0.0) - r[i][c:, d:],
                            probs[i]["q"] * e_cum[i] - r[i][:c, d:]], axis=0).astype(BF16) for i in range(n)]
    return [(lhs[i], r[i][c:, :d], r[i][:c, :d]) for i in range(n)]


def _dn_scan_kernel(qf_ref, kf_ref, vf_ref, gf_ref, bf_ref, qb_ref, kb_ref, vb_ref, gb_ref, bb_ref,
                    s0_ref, of_ref, ob_ref, sfin_ref, s_ref):
    t = pl.program_id(1)
    c, d = DN_CHUNK, HEAD_DIM

    @pl.when(t == 0)
    def _():
        s_ref[...] = s0_ref[0]

    probs = []
    for sub in range(DN_SUB):
        for d_idx, (q_ref, k_ref, v_ref, g_ref, b_ref) in enumerate(
                ((qf_ref, kf_ref, vf_ref, gf_ref, bf_ref), (qb_ref, kb_ref, vb_ref, gb_ref, bb_ref))):
            lower = d_idx == 0
            rows = pl.ds((sub if lower else DN_SUB - 1 - sub) * c, c)
            cum = _cumsum_rows(g_ref[0, rows, :], reverse=not lower)
            cum_t = cum.T
            beta = b_ref[0, rows, :]
            last = c - 1 if lower else 0
            for h in range(DN_HEADS):
                col = d_idx * DN_HEADS + h
                cum_row = cum_t[col:col + 1, :]
                probs.append(dict(q=q_ref[0, h, rows, :], k=k_ref[0, h, rows, :], v=v_ref[0, h, rows, :],
                                  lower=lower, cum_col=cum[:, col:col + 1], cum_row=cum_row,
                                  tot=cum_row[:, last:last + 1], beta_col=beta[:, col:col + 1]))
    local = _dn_chunks(probs)

    state = [s_ref[d_idx, h] for d_idx in range(N_DIR) for h in range(DN_HEADS)]
    for sub in range(DN_SUB):
        base = sub * N_DIR * DN_HEADS
        r = [_dot(local[base + i][0], state[i].astype(BF16)) for i in range(len(state))]
        state = [r[i][:d] + local[base + i][1] for i in range(len(state))]
        for d_idx, o_ref in enumerate((of_ref, ob_ref)):
            row0 = (sub if d_idx == 0 else DN_SUB - 1 - sub) * c
            o_ref[0, row0:row0 + c, :] = jnp.concatenate(
                [r[d_idx * DN_HEADS + h][d:] + local[base + d_idx * DN_HEADS + h][2] for h in range(DN_HEADS)],
                axis=1)
    for d_idx in range(N_DIR):
        for h in range(DN_HEADS):
            s_ref[d_idx, h] = state[d_idx * DN_HEADS + h]

    @pl.when(t == pl.num_programs(1) - 1)
    def _():
        sfin_ref[0] = s_ref[...]


def _dn_scan(q, k, v, g, beta, s0):
    bn, nh, t, d = q.shape
    rows = DN_SUB * DN_CHUNK
    n = t // rows
    head_f = pl.BlockSpec((1, nh, rows, d), lambda b, i: (b, 0, i, 0))
    head_b = pl.BlockSpec((1, nh, rows, d), lambda b, i: (b, 0, n - 1 - i, 0))
    row_f = pl.BlockSpec((1, rows, LANES), lambda b, i: (b, i, 0))
    row_b = pl.BlockSpec((1, rows, LANES), lambda b, i: (b, n - 1 - i, 0))
    state = pl.BlockSpec((1, N_DIR, nh, d, d), lambda b, i: (b, 0, 0, 0, 0))
    out_f = pl.BlockSpec((1, rows, nh * d), lambda b, i: (b, i, 0))
    out_b = pl.BlockSpec((1, rows, nh * d), lambda b, i: (b, n - 1 - i, 0))
    o_shape = jax.ShapeDtypeStruct((bn, t, nh * d), F32)
    return pl.pallas_call(
        _dn_scan_kernel,
        grid=(bn, n),
        in_specs=[head_f, head_f, head_f, row_f, row_f, head_b, head_b, head_b, row_b, row_b, state],
        out_specs=[out_f, out_b, state],
        out_shape=[o_shape, o_shape, jax.ShapeDtypeStruct(s0.shape, F32)],
        scratch_shapes=[pltpu.VMEM((N_DIR, nh, d, d), F32)],
        compiler_params=_cparams("parallel", "arbitrary"),
        name="dn_scan",
    )(q, k, v, g, beta, q, k, v, g, beta, s0)


def _hy_prep_kernel(x_ref, xp_ref, xn_ref, w_ref, v_ref, v16_ref, x1_ref, x2_ref):
    i = pl.program_id(1)
    y = _conv3_rows(x_ref[0], xp_ref[0], xn_ref[0], w_ref[...], i == 0, i == pl.num_programs(1) - 1)
    v = y[:, :HY_CH]
    v_ref[...] = v
    v16_ref[...] = v.astype(BF16)
    x1_ref[...] = y[:, HY_CH:2 * HY_CH]
    x2_ref[...] = y[:, 2 * HY_CH:]


def _hy_prep(z, conv_w):
    bn, t, width = z.shape
    tm = min(512, t)
    out = pl.BlockSpec((tm, HY_CH), lambda b, i: (i, b))
    f32 = jax.ShapeDtypeStruct((t, bn * HY_CH), F32)
    return pl.pallas_call(
        _hy_prep_kernel,
        grid=(bn, t // tm),
        in_specs=_halo_specs(tm, t, width) + [pl.BlockSpec(conv_w.shape, lambda b, i: (0, 0))],
        out_specs=[out, out, out, out],
        out_shape=[f32, jax.ShapeDtypeStruct((t, bn * HY_CH), BF16), f32, f32],
        compiler_params=_cparams("parallel", "parallel"),
        name="hy_prep",
    )(z, z, z, conv_w)


def _dft_matrices(length):
    n = 2 * length
    k = jnp.arange(length, dtype=jnp.int32)
    ang = ((k[:, None] * k[None, :]) % n).astype(F32) * (2.0 * math.pi / n)
    alt = (1 - 2 * (k % 2)).astype(F32)
    sin = jnp.where(k[:, None] == 0, alt[None, :], jnp.sin(ang))
    return jnp.cos(ang).astype(BF16), sin.astype(BF16)


def _hy_filter_kernel(f_ref, w1_ref, b1_ref, f1_ref, w2_ref, b2_ref, f2_ref, w3_ref, b3_ref, dl_ref,
                      p_ref, q_ref, ssq_ref, nyq_ref, taps_ref):
    i = pl.program_id(1)
    feats = f_ref[...]
    h = jnp.sin(f1_ref[0] * (_dot3(feats, w1_ref[0]) + b1_ref[0]))
    h = jnp.sin(f2_ref[0] * (_dot3(h, w2_ref[0]) + b2_ref[0]))
    h = _dot3(h, w3_ref[0]) + b3_ref[0]
    win = jnp.exp(-feats[:, 0:1] * dl_ref[...])
    half = HY_ORDER * HY_CH
    fwd = h[:, :half] * jnp.concatenate([win] * HY_ORDER, axis=1)
    bwd = h[:, half:] * jnp.concatenate([win] * HY_ORDER, axis=1)
    row = lax.broadcasted_iota(jnp.int32, bwd.shape, 0)
    bwd = jnp.where((row == 0) & (i == 0), 0.0, bwd)
    p = fwd + bwd
    p_ref[0] = p.astype(BF16)
    q_ref[0] = (bwd - fwd).astype(BF16)
    taps_ref[0] = jnp.concatenate([fwd, bwd], axis=1).astype(BF16)
    alt = (1 - 2 * (row & 1)).astype(F32)

    @pl.when(i == 0)
    def _():
        ssq_ref[0] = jnp.zeros_like(ssq_ref[0])
        nyq_ref[0] = jnp.zeros_like(nyq_ref[0])

    ssq_ref[0] += jnp.sum(fwd * fwd + bwd * bwd, axis=0, keepdims=True)
    nyq_ref[0] += jnp.sum(p * alt, axis=0, keepdims=True)


def _hy_spec_kernel(c_ref, s_ref, p_ref, q_ref, ssq_ref, nyq_ref, hre_ref, him_ref):
    i = pl.program_id(1)
    scale = lax.rsqrt(ssq_ref[0] + RMS_EPS)
    hre = _dot(c_ref[...], p_ref[0]) * scale
    him = _dot(s_ref[...], q_ref[0]) * scale
    row = lax.broadcasted_iota(jnp.int32, him.shape, 0)
    hre_ref[0] = hre
    him_ref[0] = jnp.where((row == 0) & (i == 0), nyq_ref[0] * scale, him)


def _hy_taps(length, w1, b1, f1, w2, b2, f2, w3, b3):
    depth = w1.shape[0]
    t = jnp.linspace(0.0, 1.0, length, dtype=F32)[:, None]
    omega = 2.0 * math.pi * jnp.arange(length, dtype=F32) / length
    bands = jnp.linspace(1e-4, HY_BANDS - 1, HY_BANDS, dtype=F32)
    ang = omega[:, None] * bands[None, :]
    feats = jnp.concatenate([t, jnp.cos(ang), -jnp.sin(ang), jnp.zeros((length, LANES - HY_EMB), F32)], axis=-1)
    w1p = jnp.pad(w1, ((0, 0), (0, LANES - HY_EMB), (0, 0)))
    max_decay = math.log(HY_TARGET) / HY_FAST_DECAY
    min_decay = math.log(HY_TARGET) / HY_SLOW_DECAY
    deltas = jnp.abs(jnp.linspace(min_decay, max_decay, HY_CH, dtype=F32))[None, :]
    half = HY_ORDER * HY_CH
    tm = min(512, length)
    vec = lambda a: a.reshape(depth, 1, -1)
    lay = lambda shape: pl.BlockSpec((1,) + shape, lambda l, i: (l,) + (0,) * len(shape))
    rows = lambda n: pl.BlockSpec((1, tm, n), lambda l, i: (l, i, 0))
    return pl.pallas_call(
        _hy_filter_kernel,
        grid=(depth, length // tm),
        in_specs=[pl.BlockSpec((tm, LANES), lambda l, i: (i, 0)),
                  lay((LANES, HY_FFN)), lay((1, HY_FFN)), lay((1, HY_FFN)),
                  lay((HY_FFN, HY_FFN)), lay((1, HY_FFN)), lay((1, HY_FFN)),
                  lay((HY_FFN, 2 * half)), lay((1, 2 * half)),
                  pl.BlockSpec((1, HY_CH), lambda l, i: (0, 0))],
        out_specs=[rows(half), rows(half), lay((1, half)), lay((1, half)), rows(2 * half)],
        out_shape=[jax.ShapeDtypeStruct((depth, length, half), BF16),
                   jax.ShapeDtypeStruct((depth, length, half), BF16),
                   jax.ShapeDtypeStruct((depth, 1, half), F32),
                   jax.ShapeDtypeStruct((depth, 1, half), F32),
                   jax.ShapeDtypeStruct((depth, length, 2 * half), BF16)],
        compiler_params=_cparams("parallel", "arbitrary"),
        name="hy_filter",
    )(feats, w1p, vec(b1), vec(f1), w2, vec(b2), vec(f2), w3, vec(b3), deltas)


def _hy_spec_dense(cmat, smat, p, q, ssq, nyq):
    depth, length, half = p.shape
    lay = lambda shape: pl.BlockSpec((1,) + shape, lambda l, i: (l,) + (0,) * len(shape))
    tk = min(512, length)
    return pl.pallas_call(
        _hy_spec_kernel,
        grid=(depth, length // tk),
        in_specs=[pl.BlockSpec((tk, length), lambda l, i: (i, 0)),
                  pl.BlockSpec((tk, length), lambda l, i: (i, 0)),
                  lay((length, half)), lay((length, half)), lay((1, half)), lay((1, half))],
        out_specs=[pl.BlockSpec((1, tk, half), lambda l, i: (l, i, 0))] * 2,
        out_shape=[jax.ShapeDtypeStruct((depth, length, half), F32)] * 2,
        compiler_params=_cparams("parallel", "parallel"),
        name="hy_spec",
    )(cmat, smat, p, q, ssq, nyq)


def _hy_fwd_kernel(c_ref, s_ref, u_ref, hre_ref, him_ref, yre_ref, yim_ref, *, n_points, n_batch):
    i = pl.program_id(0)
    ure = _dot(c_ref[...], u_ref[...])
    usn = _dot(s_ref[...], u_ref[...])
    hre = jnp.concatenate([hre_ref[...]] * n_batch, axis=1)
    him = jnp.concatenate([him_ref[...]] * n_batch, axis=1)
    row0 = (lax.broadcasted_iota(jnp.int32, ure.shape, 0) == 0) & (i == 0)
    yre = jnp.where(row0, ure * hre * (1.0 / n_points), (ure * hre + usn * him) * (2.0 / n_points))
    yim = jnp.where(row0, usn * him * (1.0 / n_points), (usn * hre - ure * him) * (2.0 / n_points))
    yre_ref[...] = yre.astype(BF16)
    yim_ref[...] = yim.astype(BF16)


def _hy_inv_kernel(c_ref, s_ref, yre_ref, yim_ref, u_ref, gate_ref, skip_ref, o_ref, *o16_ref, n_batch):
    i = pl.program_id(0)
    tm = c_ref.shape[0]
    conv_c = _dot(c_ref[...], yre_ref[...])
    conv_s = _dot(s_ref[...], yim_ref[...])
    trow = lax.broadcasted_iota(jnp.int32, conv_c.shape, 0) + i * tm
    alt = (1 - 2 * (trow & 1)).astype(F32)
    nyq = yim_ref[0:1, :].astype(F32)
    conv = conv_c + jnp.where(trow == 0, 0.0, conv_s) + alt * nyq
    skip = jnp.concatenate([skip_ref[...]] * n_batch, axis=1)
    y = gate_ref[...] * (conv + skip * u_ref[...])
    o_ref[...] = y
    if o16_ref:
        o16_ref[0][...] = y.astype(BF16)


def _hy_long_conv(cmat, smat, u16, u, gate, hre, him, skip, order, emit_bf16):
    length, cols = u.shape
    n_batch = cols // HY_CH
    tk = min(512, length)
    full = pl.BlockSpec((length, cols), lambda i: (0, 0))
    mat = pl.BlockSpec((tk, length), lambda i: (i, 0))
    tile = pl.BlockSpec((tk, cols), lambda i: (i, 0))
    spec = pl.BlockSpec((tk, HY_CH), lambda i: (i, order))
    yre, yim = pl.pallas_call(
        functools.partial(_hy_fwd_kernel, n_points=2 * length, n_batch=n_batch),
        grid=(length // tk,),
        in_specs=[mat, mat, full, spec, spec],
        out_specs=[tile, tile],
        out_shape=[jax.ShapeDtypeStruct((length, cols), BF16)] * 2,
        compiler_params=_cparams("parallel"),
        name="hy_fwd",
    )(cmat, smat, u16, hre, him)
    tm = min(256, length)
    mat = pl.BlockSpec((tm, length), lambda i: (i, 0))
    tile = pl.BlockSpec((tm, cols), lambda i: (i, 0))
    out_shape = [jax.ShapeDtypeStruct((length, cols), F32)]
    if emit_bf16:
        out_shape.append(jax.ShapeDtypeStruct((length, cols), BF16))
    return pl.pallas_call(
        functools.partial(_hy_inv_kernel, n_batch=n_batch),
        grid=(length // tm,),
        in_specs=[mat, mat, full, full, tile, tile, pl.BlockSpec((1, HY_CH), lambda i: (0, 0))],
        out_specs=[tile] * len(out_shape),
        out_shape=out_shape,
        compiler_params=_cparams("parallel"),
        name="hy_inv",
    )(cmat, smat, yre, yim, u, gate, skip[order][None, :])


def _hyena(z, conv_w, cmat, smat, hre, him, skip):
    v, v16, x1, x2 = _hy_prep(z, conv_w)
    y1, y16 = _hy_long_conv(cmat, smat, v16, v, x1, hre, him, skip, 0, True)
    (y,) = _hy_long_conv(cmat, smat, y16, y1, x2, hre, him, skip, 1, False)
    return y


FFT_N2 = 256
FFT_ROWS = 16


def _fft_consts(length):
    n = 2 * length
    n1 = n // FFT_N2
    nb = length // FFT_N2
    r = n1 // 2 + 1
    rp = -(-2 * r // SUBLANES) * SUBLANES
    kk = np.arange(r)[:, None, None] + n1 * np.arange(FFT_N2)[None, :, None]
    ang2 = 2 * np.pi * ((kk * np.arange(FFT_N2)[None, None, :]) % n) / n
    gc, gs = np.cos(ang2), np.sin(ang2)
    b16 = lambda a: jnp.asarray(a, F32).astype(BF16)
    return dict(r=r, rp=rp, n1=n1, nb=nb, gc=b16(gc), gs=b16(gs),
                gct=b16(gc.transpose(0, 2, 1)), gst=b16(gs.transpose(0, 2, 1)))


def _axpy(acc, coef, t):
    if abs(coef) < 1e-9:
        return acc
    term = t if abs(coef - 1.0) < 1e-9 else (-t if abs(coef + 1.0) < 1e-9 else coef * t)
    return term if acc is None else acc + term


def _nz(t, like):
    return jnp.zeros_like(like) if t is None else t


def _fft_s1f_rows_kernel(v_ref, o_ref, *, n1, nb, r):
    half = n1 // 2
    x = [v_ref[0, b].astype(F32) for b in range(nb)]
    done = set()
    for k in range(half // 2 + 1):
        p = half - k
        ce = co = se = so = None
        for b in range(nb):
            ang = 2.0 * math.pi * ((b * k) % n1) / n1
            if b % 2 == 0:
                ce, se = _axpy(ce, math.cos(ang), x[b]), _axpy(se, math.sin(ang), x[b])
            else:
                co, so = _axpy(co, math.cos(ang), x[b]), _axpy(so, math.sin(ang), x[b])
        ce, co, se, so = (_nz(t, x[0]) for t in (ce, co, se, so))
        o_ref[0, k] = (ce + co).astype(BF16)
        o_ref[0, r + k] = (-(se + so)).astype(BF16)
        done.update((k, r + k))
        if p != k:
            o_ref[0, p] = (ce - co).astype(BF16)
            o_ref[0, r + p] = (se - so).astype(BF16)
            done.update((p, r + p))
    for row in range(o_ref.shape[1]):
        if row not in done:
            o_ref[0, row] = jnp.zeros(o_ref.shape[2:], BF16)


def _fft_s1i_rows_kernel(br_ref, bi_ref, u_ref, gate_ref, skip_ref, o_ref, *o16_ref, n1, nb, n_batch):
    half = n1 // 2
    n = n1 * FFT_N2
    acc = [None] * nb
    for k in range(half // 2 + 1):
        p = half - k
        w = (1.0 if k == 0 else 2.0) / n
        rk, ik = br_ref[k].astype(F32), bi_ref[k].astype(F32)
        if p != k:
            rp_, ip_ = br_ref[p].astype(F32), bi_ref[p].astype(F32)
            r_even, r_odd, i_even, i_odd = rk + rp_, rk - rp_, ik - ip_, ik + ip_
        else:
            r_even = r_odd = rk
            i_even = i_odd = ik
        for b in range(nb):
            ang = 2.0 * math.pi * ((b * k) % n1) / n1
            re, im = (r_even, i_even) if b % 2 == 0 else (r_odd, i_odd)
            acc[b] = _axpy(_axpy(acc[b], w * math.cos(ang), re), -w * math.sin(ang), im)
    skip = jnp.concatenate([skip_ref[...]] * n_batch, axis=1)
    for b in range(nb):
        y = gate_ref[b] * (acc[b] + skip * u_ref[b])
        o_ref[b] = y
        if o16_ref:
            o16_ref[0][b] = y.astype(BF16)


def _fft_stage1_rows(fc, v):
    g, length, cols = v.shape
    r, rp, n1, nb = fc["r"], fc["rp"], fc["n1"], fc["nb"]
    tr = FFT_ROWS
    return pl.pallas_call(
        functools.partial(_fft_s1f_rows_kernel, n1=n1, nb=nb, r=r),
        grid=(g, FFT_N2 // tr),
        in_specs=[pl.BlockSpec((1, nb, tr, cols), lambda l, i: (l, 0, i, 0))],
        out_specs=pl.BlockSpec((1, rp, tr, cols), lambda l, i: (l, 0, i, 0)),
        out_shape=jax.ShapeDtypeStruct((g, rp, FFT_N2, cols), BF16),
        compiler_params=_cparams("parallel", "parallel"),
        name="fft_s1_fwd",
    )(v.reshape(g, nb, FFT_N2, cols))


def _fft_s2f_kernel(ar_ref, ai_ref, gc_ref, gs_ref, *rest, spectrum, n_batch):
    ar, ai, gc, gs = ar_ref[0, 0], ai_ref[0, 0], gc_ref[0], gs_ref[0]
    xr = _dot(gc, ar) + _dot(gs, ai)
    xi = _dot(gc, ai) - _dot(gs, ar)
    if spectrum:
        ssq_ref, hr_ref, hi_ref = rest
        half = xr.shape[1] // 2
        scale = lax.rsqrt(ssq_ref[0] + RMS_EPS)
        hr_ref[0, 0] = (xr[:, :half] + xr[:, half:]) * scale
        hi_ref[0, 0] = (xi[:, :half] - xi[:, half:]) * scale
    else:
        hr_ref, hi_ref, yr_ref, yi_ref = rest
        hr = jnp.concatenate([hr_ref[0]] * n_batch, axis=1)
        hi = jnp.concatenate([hi_ref[0]] * n_batch, axis=1)
        yr_ref[0] = (xr * hr - xi * hi).astype(BF16)
        yi_ref[0] = (xr * hi + xi * hr).astype(BF16)


def _fft_s2i_kernel(yr_ref, yi_ref, gct_ref, gst_ref, br_ref, bi_ref):
    yr, yi, gct, gst = yr_ref[0], yi_ref[0], gct_ref[0], gst_ref[0]
    br_ref[0] = (_dot(gct, yr) - _dot(gst, yi)).astype(BF16)
    bi_ref[0] = (_dot(gct, yi) + _dot(gst, yr)).astype(BF16)


def _hy_spec_fft(fc, taps, ssq):
    depth, length, cols = taps.shape
    r, rp, nb = fc["r"], fc["rp"], fc["nb"]
    half = cols // 2
    a = _fft_stage1_rows(fc, taps)
    res = lambda off: pl.BlockSpec((1, 1, FFT_N2, cols), lambda l, k: (l, k + off, 0, 0))
    g = pl.BlockSpec((1, FFT_N2, FFT_N2), lambda l, k: (k, 0, 0))
    out = pl.BlockSpec((1, 1, FFT_N2, half), lambda l, k: (l, k, 0, 0))
    return pl.pallas_call(
        functools.partial(_fft_s2f_kernel, spectrum=True, n_batch=1),
        grid=(depth, r),
        in_specs=[res(0), res(r), g, g, pl.BlockSpec((1, 1, half), lambda l, k: (l, 0, 0))],
        out_specs=[out, out],
        out_shape=[jax.ShapeDtypeStruct((depth, r, FFT_N2, half), F32)] * 2,
        compiler_params=_cparams("parallel", "parallel"),
        name="fft_spec",
    )(a, a, fc["gc"], fc["gs"], ssq)


def _hy_long_conv_fft(fc, u, gate, hre, him, skip, order):
    length, cols = u.shape
    n_batch = cols // HY_CH
    r, n1, nb = fc["r"], fc["n1"], fc["nb"]
    a = _fft_stage1_rows(fc, u[None])
    res = lambda off: pl.BlockSpec((1, 1, FFT_N2, cols), lambda k: (0, k + off, 0, 0))
    g = pl.BlockSpec((1, FFT_N2, FFT_N2), lambda k: (k, 0, 0))
    spec = pl.BlockSpec((1, FFT_N2, HY_CH), lambda k: (k, 0, order))
    blk = pl.BlockSpec((1, FFT_N2, cols), lambda k: (k, 0, 0))
    yr, yi = pl.pallas_call(
        functools.partial(_fft_s2f_kernel, spectrum=False, n_batch=n_batch),
        grid=(r,),
        in_specs=[res(0), res(r), g, g, spec, spec],
        out_specs=[blk, blk],
        out_shape=[jax.ShapeDtypeStruct((r, FFT_N2, cols), BF16)] * 2,
        compiler_params=_cparams("parallel"),
        name="fft_s2_fwd",
    )(a, a, fc["gc"], fc["gs"], hre, him)
    br, bi = pl.pallas_call(
        _fft_s2i_kernel,
        grid=(r,),
        in_specs=[blk, blk, g, g],
        out_specs=[blk, blk],
        out_shape=[jax.ShapeDtypeStruct((r, FFT_N2, cols), BF16)] * 2,
        compiler_params=_cparams("parallel"),
        name="fft_s2_inv",
    )(yr, yi, fc["gct"], fc["gst"])
    tr = FFT_ROWS
    rows = lambda n: pl.BlockSpec((n, tr, cols), lambda i: (0, i, 0))
    y = pl.pallas_call(
        functools.partial(_fft_s1i_rows_kernel, n1=n1, nb=nb, n_batch=n_batch),
        grid=(FFT_N2 // tr,),
        in_specs=[rows(r), rows(r), rows(nb), rows(nb), pl.BlockSpec((1, HY_CH), lambda i: (0, 0))],
        out_specs=rows(nb),
        out_shape=jax.ShapeDtypeStruct((nb, FFT_N2, cols), F32),
        compiler_params=_cparams("parallel"),
        name="fft_s1_inv",
    )(br, bi, u.reshape(nb, FFT_N2, cols), gate.reshape(nb, FFT_N2, cols), skip[order][None, :])
    return y.reshape(length, cols)


def _hyena_fft(z, conv_w, fc, hre, him, skip):
    v, _, x1, x2 = _hy_prep(z, conv_w)
    y1 = _hy_long_conv_fft(fc, v, x1, hre, him, skip, 0)
    return _hy_long_conv_fft(fc, y1, x2, hre, him, skip, 1)


def _attn_kernel(sink_ref, q_ref, *rest, local, n_blocks):
    if local:
        kp_ref, kc_ref, kn_ref, vp_ref, vc_ref, vn_ref, kx_ref, vx_ref, o_ref = rest
        k_all = jnp.concatenate([kp_ref[0], kc_ref[0], kn_ref[0], kx_ref[0]], axis=0)
        v_all = jnp.concatenate([vp_ref[0], vc_ref[0], vn_ref[0], vx_ref[0]], axis=0)
    else:
        kx_ref, vx_ref, o_ref = rest
        k_all, v_all = kx_ref[0], vx_ref[0]
    blk = pl.program_id(1)
    q = q_ref[0]
    tq = q.shape[0]
    nk = k_all.shape[0]
    k16, v16 = k_all.astype(BF16), v_all.astype(BF16)
    lane = lax.broadcasted_iota(jnp.int32, (tq, LANES), 1)
    rows = SW_GROUP * tq
    r_idx = lax.broadcasted_iota(jnp.int32, (rows, nk), 0)
    if local:
        r = r_idx & (tq - 1)
        c = lax.broadcasted_iota(jnp.int32, (rows, nk), 1)
        rel = c - r - SW_BLOCK
        valid = (rel >= -SW_BLOCK) & (rel <= SW_BLOCK)
        valid = valid & ((c >= SW_BLOCK) | (blk > 0)) & ((c < 2 * SW_BLOCK) | (blk < n_blocks - 1))
        valid = valid | (c >= 3 * SW_BLOCK)
    out_heads = [None] * SW_Q_HEADS
    for j in range(SW_KV_HEADS):
        keep = (lane >= j * HEAD_DIM) & (lane < (j + 1) * HEAD_DIM)
        parts = []
        for g in range(SW_GROUP):
            hq = j * SW_GROUP + g
            chunk = q[:, (hq // 2) * LANES:(hq // 2 + 1) * LANES]
            if hq % 2 != j:
                chunk = pltpu.roll(chunk, HEAD_DIM, 1)
            parts.append(jnp.where(keep, chunk, 0.0))
        qz = jnp.concatenate(parts, axis=0).astype(BF16)
        s = _dot_nt(qz, k16)
        if local:
            s = jnp.where(valid, s, NEG_INF)
        sink = jnp.where(r_idx[:, 0:1] < tq, sink_ref[j * SW_GROUP],
                         jnp.where(r_idx[:, 0:1] < 2 * tq, sink_ref[j * SW_GROUP + 1], sink_ref[j * SW_GROUP + 2]))
        m = jnp.maximum(jnp.max(s, axis=-1, keepdims=True), sink)
        p = jnp.exp(s - m)
        denom = jnp.sum(p, axis=-1, keepdims=True) + jnp.exp(sink - m)
        o = _dot(p.astype(BF16), v16) / denom
        for g in range(SW_GROUP):
            hq = j * SW_GROUP + g
            og = o[g * tq:(g + 1) * tq]
            if hq % 2 != j:
                og = pltpu.roll(og, HEAD_DIM, 1)
            out_heads[hq] = og
    first_half = lane < HEAD_DIM
    o_ref[0] = jnp.concatenate(
        [jnp.where(first_half, out_heads[2 * c], out_heads[2 * c + 1]) for c in range(SW_Q_HEADS // 2)], axis=1)


def _attention(q, k, v, kx, vx, sink, local):
    bn, t, _ = q.shape
    nb = t // SW_BLOCK
    tx = kx.shape[1]
    qspec = pl.BlockSpec((1, SW_BLOCK, SW_WIDTH), lambda b, i: (b, i, 0))
    xspec = pl.BlockSpec((1, tx, SW_KV_WIDTH), lambda b, i: (b, 0, 0))
    in_specs = [pl.BlockSpec(memory_space=pltpu.SMEM), qspec]
    args = [sink, q]
    if local:
        prev = pl.BlockSpec((1, SW_BLOCK, SW_KV_WIDTH), lambda b, i: (b, jnp.maximum(i - 1, 0), 0))
        cur = pl.BlockSpec((1, SW_BLOCK, SW_KV_WIDTH), lambda b, i: (b, i, 0))
        nxt = pl.BlockSpec((1, SW_BLOCK, SW_KV_WIDTH), lambda b, i: (b, jnp.minimum(i + 1, nb - 1), 0))
        in_specs += [prev, cur, nxt, prev, cur, nxt]
        args += [k, k, k, v, v, v]
    in_specs += [xspec, xspec]
    args += [kx, vx]
    return pl.pallas_call(
        functools.partial(_attn_kernel, local=local, n_blocks=nb),
        grid=(bn, nb),
        in_specs=in_specs,
        out_specs=qspec,
        out_shape=jax.ShapeDtypeStruct((bn, t, SW_WIDTH), F32),
        compiler_params=_cparams("parallel", "parallel"),
        name="attn_local" if local else "attn_ctx",
    )(*args)


def _out_proj_kernel(of_ref, ob_ref, z_ref, gn_ref, gsum_ref, hy_ref, sw_ref, x_ref, gate_ref, w_ref, o_ref):
    o = of_ref[0] + ob_ref[0]
    mean = _dot2_exact_rhs(o * o, gsum_ref[...]) * (1.0 / HEAD_DIM)
    dn = o * lax.rsqrt(mean + RMS_EPS) * gn_ref[...] * _silu(z_ref[0])
    a, b = DN_WIDTH, DN_WIDTH + HY_CH
    mix = _dot(dn.astype(BF16), w_ref[0:a, :])
    mix += _dot(hy_ref[...].astype(BF16), w_ref[a:b, :])
    mix += _dot(sw_ref[0].astype(BF16), w_ref[b:, :])
    o_ref[0] = x_ref[0] + gate_ref[0] * mix


def _out_proj(o_f, o_b, z, gn, gsum, hy, sw, x, gate, w):
    bn, t, d = x.shape
    tm = min(512, t)
    seq = lambda n: pl.BlockSpec((1, tm, n), lambda b, i: (b, i, 0))
    const = lambda shape: pl.BlockSpec(shape, lambda b, i: (0,) * len(shape))
    return pl.pallas_call(
        _out_proj_kernel,
        grid=(bn, t // tm),
        in_specs=[seq(DN_WIDTH), seq(DN_WIDTH), seq(DN_WIDTH), const(gn.shape), const(gsum.shape),
                  pl.BlockSpec((tm, HY_CH), lambda b, i: (i, b)), seq(SW_WIDTH), seq(d),
                  pl.BlockSpec((1, 1, d), lambda b, i: (b, 0, 0)), const(w.shape)],
        out_specs=seq(d),
        out_shape=jax.ShapeDtypeStruct(x.shape, F32),
        compiler_params=_cparams("parallel", "parallel"),
        name="out_proj",
    )(o_f, o_b, z, gn, gsum, hy, sw, x, gate, w)


def _ffn_kernel(x_ref, g_ref, sh_ref, sc_ref, gate_ref, wg_ref, wu_ref, wd_ref, o_ref, h_ref, acc_ref):
    j = pl.program_id(2)

    @pl.when(j == 0)
    def _():
        h_ref[...] = _rms_mod(x_ref[0], g_ref[...], sh_ref[0], sc_ref[0]).astype(BF16)
        acc_ref[...] = jnp.zeros_like(acc_ref)

    h = h_ref[...]
    act = _silu(_dot(h, wg_ref[...])) * _dot(h, wu_ref[...])
    acc_ref[...] += _dot(act.astype(BF16), wd_ref[...])

    @pl.when(j == pl.num_programs(2) - 1)
    def _():
        o_ref[0] = x_ref[0] + gate_ref[0] * acc_ref[...]


def _ffn(x, g, shift, scale, gate, wg, wu, wd):
    bn, t, d = x.shape
    ff = wg.shape[1]
    tm = min(512, t)
    tf = ff // 2
    seq = pl.BlockSpec((1, tm, d), lambda b, i, j: (b, i, 0))
    mod = pl.BlockSpec((1, 1, d), lambda b, i, j: (b, 0, 0))
    return pl.pallas_call(
        _ffn_kernel,
        grid=(bn, t // tm, ff // tf),
        in_specs=[seq, pl.BlockSpec((1, d), lambda b, i, j: (0, 0)), mod, mod, mod,
                  pl.BlockSpec((d, tf), lambda b, i, j: (0, j)),
                  pl.BlockSpec((d, tf), lambda b, i, j: (0, j)),
                  pl.BlockSpec((tf, d), lambda b, i, j: (j, 0))],
        out_specs=seq,
        out_shape=jax.ShapeDtypeStruct(x.shape, F32),
        scratch_shapes=[pltpu.VMEM((tm, d), BF16), pltpu.VMEM((tm, d), F32)],
        compiler_params=_cparams("parallel", "parallel", "arbitrary"),
        name="ffn_dense",
    )(x, g, shift, scale, gate, wg, wu, wd)


MOE_TILE = 512
_INFO_E, _INFO_W, _INFO_RANK = 0, 2, 4


def _route_kernel(x_ref, g_ref, sh_ref, sc_ref, r_ref, h_ref, info_ref, cnt_ref, carry_ref):
    @pl.when((pl.program_id(0) == 0) & (pl.program_id(1) == 0))
    def _():
        carry_ref[...] = jnp.zeros_like(carry_ref)

    h = _rms_mod(x_ref[0], g_ref[...], sh_ref[0], sc_ref[0])
    h_ref[0] = h
    logits = _dot3(h, r_ref[...])
    tm = logits.shape[0]
    lane = lax.broadcasted_iota(jnp.int32, logits.shape, 1)
    lg = jnp.where(lane < N_EXPERTS, logits, NEG_INF)
    m1 = jnp.max(lg, axis=-1, keepdims=True)
    i1 = jnp.min(jnp.where(lg == m1, lane, LANES), axis=-1, keepdims=True)
    lg2 = jnp.where(lane == i1, NEG_INF, lg)
    m2 = jnp.max(lg2, axis=-1, keepdims=True)
    i2 = jnp.min(jnp.where(lg2 == m2, lane, LANES), axis=-1, keepdims=True)
    e2 = jnp.exp(m2 - m1)
    w1 = 1.0 / (1.0 + e2)
    w2 = e2 * w1
    chosen = jnp.where((lane == i1) | (lane == i2), 1.0, 0.0)
    rr = lax.broadcasted_iota(jnp.int32, (tm, tm), 0)
    cc = lax.broadcasted_iota(jnp.int32, (tm, tm), 1)
    earlier = jnp.where(rr > cc, 1.0, 0.0).astype(BF16)
    before = _dot(earlier, chosen.astype(BF16)) + carry_ref[...]
    rank1 = jnp.sum(jnp.where(lane == i1, before, 0.0), axis=-1, keepdims=True)
    rank2 = jnp.sum(jnp.where(lane == i2, before, 0.0), axis=-1, keepdims=True)
    info = jnp.zeros_like(logits)
    for pos, val in ((_INFO_E, i1.astype(F32)), (_INFO_E + 1, i2.astype(F32)), (_INFO_W, w1), (_INFO_W + 1, w2),
                     (_INFO_RANK, rank1), (_INFO_RANK + 1, rank2)):
        info = jnp.where(lane == pos, val, info)
    info_ref[0] = info
    carry_ref[...] += jnp.sum(chosen, axis=0, keepdims=True)
    cnt_ref[...] = carry_ref[...]


def _row_copies(src_row, dst_row, sem, tm, wait):
    def body(r, carry):
        for c in range(2):
            cp = pltpu.make_async_copy(src_row(r, c), dst_row(r, c), sem.at[c])
            if wait:
                cp.wait()
            else:
                cp.start()
        return carry
    lax.fori_loop(0, tm, body, 0, unroll=8)


def _scatter_kernel(dest_ref, h_ref, xs_in_ref, xs_ref, sem):
    del xs_in_ref
    tm = h_ref.shape[1]
    src = lambda r, c: h_ref.at[0, pl.ds(r, 1)]
    dst = lambda r, c: xs_ref.at[pl.ds(dest_ref[0, 0, c * tm + r], 1)]
    _row_copies(src, dst, sem, tm, wait=False)
    _row_copies(src, dst, sem, tm, wait=True)


def _group_ffn_kernel(te_ref, tb_ref, nv_ref, xs_ref, wg_ref, wu_ref, wd_ref, ys_ref, x16_ref, acc_ref):
    j = pl.program_id(0)
    f = pl.program_id(1)

    @pl.when(j < nv_ref[0])
    def _():
        @pl.when(f == 0)
        def _():
            x16_ref[...] = xs_ref[...].astype(BF16)
            acc_ref[...] = jnp.zeros_like(acc_ref)

        h = x16_ref[...]
        act = _silu(_dot(h, wg_ref[0])) * _dot(h, wu_ref[0])
        acc_ref[...] += _dot(act.astype(BF16), wd_ref[0])

        @pl.when(f == pl.num_programs(1) - 1)
        def _():
            ys_ref[...] = acc_ref[...]

    @pl.when(j >= nv_ref[0])
    def _():
        ys_ref[...] = jnp.zeros_like(ys_ref)


def _combine_kernel(dest_ref, x_ref, gate_ref, info_ref, ys_ref, o_ref, buf_ref, sem):
    tm = x_ref.shape[1]
    src = lambda r, c: ys_ref.at[pl.ds(dest_ref[0, 0, c * tm + r], 1)]
    dst = lambda r, c: buf_ref.at[c, pl.ds(r, 1)]
    _row_copies(src, dst, sem, tm, wait=False)
    _row_copies(src, dst, sem, tm, wait=True)
    info = info_ref[0]
    mix = info[:, _INFO_W:_INFO_W + 1] * buf_ref[0] + info[:, _INFO_W + 1:_INFO_W + 2] * buf_ref[1]
    o_ref[0] = x_ref[0] + gate_ref[0] * mix


def _moe(x, g, shift, scale, gate, router, wg, wu, wd):
    bn, t, d = x.shape
    ne, _, ff = wg.shape
    tm = MOE_TILE
    nt = t // tm
    n_tok = bn * t
    n_slots = 2 * n_tok // tm + ne
    seq = pl.BlockSpec((1, tm, d), lambda b, i: (b, i, 0))
    mod = pl.BlockSpec((1, 1, d), lambda b, i: (b, 0, 0))
    rec = pl.BlockSpec((1, tm, LANES), lambda b, i: (b, i, 0))
    h, info, counts = pl.pallas_call(
        _route_kernel,
        grid=(bn, nt),
        in_specs=[seq, pl.BlockSpec((1, d), lambda b, i: (0, 0)), mod, mod,
                  pl.BlockSpec(router.shape, lambda b, i: (0, 0))],
        out_specs=[seq, rec, pl.BlockSpec((1, LANES), lambda b, i: (0, 0))],
        out_shape=[jax.ShapeDtypeStruct(x.shape, F32), jax.ShapeDtypeStruct((bn, t, LANES), F32),
                   jax.ShapeDtypeStruct((1, LANES), F32)],
        scratch_shapes=[pltpu.VMEM((1, LANES), F32)],
        compiler_params=_cparams("arbitrary", "arbitrary"),
        name="moe_route",
    )(x, g, shift, scale, router)

    flat = info.reshape(n_tok, LANES)
    expert = flat[:, _INFO_E:_INFO_E + 2].astype(jnp.int32)
    rank = flat[:, _INFO_RANK:_INFO_RANK + 2].astype(jnp.int32)
    tiles_e = (counts[0, :ne].astype(jnp.int32) + tm - 1) // tm
    ends = jnp.cumsum(tiles_e)
    dest = (ends - tiles_e)[expert] * tm + rank
    dest = dest.reshape(bn * nt, tm, 2).transpose(0, 2, 1).reshape(bn * nt, 1, 2 * tm)
    n_valid = ends[-1]
    slot = jnp.minimum(jnp.arange(n_slots, dtype=jnp.int32), n_valid - 1)
    slot_expert = jnp.minimum(jnp.searchsorted(ends, slot, side="right"), ne - 1).astype(jnp.int32)

    dspec = pl.BlockSpec((1, 1, 2 * tm), lambda b, i: (b * nt + i, 0, 0), memory_space=pltpu.SMEM)
    anyspec = pl.BlockSpec(memory_space=pl.ANY)
    xs = pl.pallas_call(
        _scatter_kernel,
        grid=(bn, nt),
        in_specs=[dspec, seq, anyspec],
        out_specs=anyspec,
        out_shape=jax.ShapeDtypeStruct((n_slots * tm, d), F32),
        scratch_shapes=[pltpu.SemaphoreType.DMA((2,))],
        input_output_aliases={2: 0},
        compiler_params=_cparams("arbitrary", "arbitrary"),
        name="moe_scatter",
    )(dest, h, jnp.zeros((n_slots * tm, d), F32))

    nf = 2
    tf = ff // nf
    last = nf - 1
    fidx = lambda j, f, nv: jnp.where(j < nv[0], f, last)
    ys = pl.pallas_call(
        _group_ffn_kernel,
        grid_spec=pltpu.PrefetchScalarGridSpec(
            num_scalar_prefetch=3,
            grid=(n_slots, nf),
            in_specs=[pl.BlockSpec((tm, d), lambda j, f, te, tb, nv: (tb[j], 0)),
                      pl.BlockSpec((1, d, tf), lambda j, f, te, tb, nv: (te[j], 0, fidx(j, f, nv))),
                      pl.BlockSpec((1, d, tf), lambda j, f, te, tb, nv: (te[j], 0, fidx(j, f, nv))),
                      pl.BlockSpec((1, tf, d), lambda j, f, te, tb, nv: (te[j], fidx(j, f, nv), 0))],
            out_specs=pl.BlockSpec((tm, d), lambda j, f, te, tb, nv: (j, 0)),
            scratch_shapes=[pltpu.VMEM((tm, d), BF16), pltpu.VMEM((tm, d), F32)]),
        out_shape=jax.ShapeDtypeStruct((n_slots * tm, d), F32),
        compiler_params=_cparams("arbitrary", "arbitrary"),
        name="moe_group_ffn",
    )(slot_expert, slot, n_valid.reshape(1), xs, wg, wu, wd)

    return pl.pallas_call(
        _combine_kernel,
        grid=(bn, nt),
        in_specs=[dspec, seq, mod, rec, anyspec],
        out_specs=seq,
        out_shape=jax.ShapeDtypeStruct(x.shape, F32),
        scratch_shapes=[pltpu.VMEM((2, tm, d), F32), pltpu.SemaphoreType.DMA((2,))],
        compiler_params=_cparams("arbitrary", "arbitrary"),
        name="moe_combine",
    )(dest, x, gate, info, ys)


def _final_norm_kernel(x_ref, g_ref, o_ref):
    x = x_ref[0]
    o_ref[0] = x * lax.rsqrt(jnp.mean(x * x, axis=-1, keepdims=True) + RMS_EPS) * g_ref[...]


def _final_norm(x, g):
    bn, t, d = x.shape
    tm = min(1024, t)
    seq = pl.BlockSpec((1, tm, d), lambda b, i: (b, i, 0))
    return pl.pallas_call(
        _final_norm_kernel,
        grid=(bn, t // tm),
        in_specs=[seq, pl.BlockSpec((1, d), lambda b, i: (0, 0))],
        out_specs=seq,
        out_shape=jax.ShapeDtypeStruct(x.shape, F32),
        compiler_params=_cparams("parallel", "parallel"),
        name="final_norm",
    )(x, g)


def _head_sum_matrix():
    idx = np.arange(DN_WIDTH) // HEAD_DIM
    return jnp.asarray(idx[:, None] == idx[None, :], dtype=BF16)


def kernel(x, c, ctx, c_ctx, w_mod, b_mod, norm_mix_g, norm_ffn_g, w_in, dn_conv_w, dn_a_log, dn_dt_bias, dn_norm_g, hy_conv_w, hy_w1, hy_b1, hy_freq1, hy_w2, hy_b2, hy_freq2, hy_w3, hy_b3, hy_skip, sw_sink, w_out, ffn_w_gate, ffn_w_up, ffn_w_down, moe_router, moe_w_gate, moe_w_up, moe_w_down, final_norm_g):
    bn, seq_len, d = x.shape
    ctx_len = ctx.shape[1]
    depth = w_mod.shape[0]

    c_rows = jnp.concatenate([c, c_ctx[None, :], jnp.zeros((SUBLANES - bn - 1, d), F32)], axis=0)
    mods = _modulation(c_rows, w_mod, b_mod).reshape(depth, SUBLANES, N_MOD, d)

    rope_tabs = _rope_tables(seq_len)
    gsum = _head_sum_matrix()
    cm_c, sm_c = _dft_matrices(ctx_len)
    hy_params = (hy_w1, hy_b1, hy_freq1, hy_w2, hy_b2, hy_freq2, hy_w3, hy_b3)
    fft_x = _fft_consts(seq_len)
    taps_x = _hy_taps(seq_len, *hy_params)
    hre_x, him_x = _hy_spec_fft(fft_x, taps_x[4], taps_x[2])
    hre_c, him_c = _hy_spec_dense(cm_c, sm_c, *_hy_taps(ctx_len, *hy_params)[:4])
    router = jnp.pad(moe_router, ((0, 0), (0, 0), (0, LANES - N_EXPERTS)))
    zero_state = jnp.zeros((bn, N_DIR, DN_HEADS, HEAD_DIM, HEAD_DIM), F32)

    for layer in range(depth):
        last = layer == depth - 1
        mod_x = [mods[layer, :bn, m][:, None, :] for m in range(N_MOD)]
        mod_c = [jnp.broadcast_to(mods[layer, bn, m][None, None, :], (bn, 1, d)) for m in range(N_MOD)]
        g_mix = norm_mix_g[layer][None, :]
        g_ffn = norm_ffn_g[layer][None, :]
        w_in_l = _relayout_w_in(w_in[layer])
        w_out_l = w_out[layer].astype(BF16)
        gn = jnp.tile(dn_norm_g[layer], DN_HEADS)[None, :]

        qkv_x, z_x, hyp_x, swq_x, swk_x, swv_x, ab_x = _in_proj(x, g_mix, mod_x[0], mod_x[1], w_in_l, rope_tabs)
        qkv_c, z_c, hyp_c, swq_c, swk_c, swv_c, ab_c = _in_proj(ctx, g_mix, mod_c[0], mod_c[1], w_in_l, None)

        dn_in_c = _dn_prep(qkv_c, ab_c, dn_conv_w[layer], dn_a_log[layer], dn_dt_bias[layer], gsum)
        dn_in_x = _dn_prep(qkv_x, ab_x, dn_conv_w[layer], dn_a_log[layer], dn_dt_bias[layer], gsum)
        of_c, ob_c, state_c = _dn_scan(*dn_in_c, zero_state)
        of_x, ob_x, _ = _dn_scan(*dn_in_x, state_c)

        hy_x = _hyena_fft(hyp_x, hy_conv_w[layer], fft_x, hre_x[layer], him_x[layer], hy_skip[layer])
        sw_x = _attention(swq_x, swk_x, swv_x, swk_c, swv_c, sw_sink[layer], True)
        x = _out_proj(of_x, ob_x, z_x, gn, gsum, hy_x, sw_x, x, mod_x[2], w_out_l)

        if not last:
            hy_c = _hyena(hyp_c, hy_conv_w[layer], cm_c, sm_c, hre_c[layer], him_c[layer], hy_skip[layer])
            sw_c = _attention(swq_c, None, None, swk_c, swv_c, sw_sink[layer], False)
            ctx = _out_proj(of_c, ob_c, z_c, gn, gsum, hy_c, sw_c, ctx, mod_c[2], w_out_l)

        i = layer // 2
        streams = [(x, mod_x)] if last else [(x, mod_x), (ctx, mod_c)]
        outs = []
        for s, mod in streams:
            if layer % 2 == 0:
                outs.append(_ffn(s, g_ffn, mod[3], mod[4], mod[5], ffn_w_gate[i].astype(BF16),
                                 ffn_w_up[i].astype(BF16), ffn_w_down[i].astype(BF16)))
            else:
                shape = s.shape
                if mod is mod_c:
                    s = s.reshape(1, -1, d)
                    mod = [m[:1] for m in mod]
                outs.append(_moe(s, g_ffn, mod[3], mod[4], mod[5], router[i], moe_w_gate[i].astype(BF16),
                                 moe_w_up[i].astype(BF16), moe_w_down[i].astype(BF16)).reshape(shape))
        x = outs[0]
        if not last:
            ctx = outs[1]

    return _final_norm(x, final_norm_g[None, :])
```

```python
import functools
import math

import jax
import jax.numpy as jnp
import numpy as np
from jax import lax
from jax.experimental import pallas as pl
from jax.experimental.pallas import tpu as pltpu

F32 = jnp.float32
BF16 = jnp.bfloat16

D_MODEL = 1024
DEPTH = 4
GRID_W = 64
N_MOD = 6
HEAD_DIM = 64
N_DIR = 2
RMS_EPS = 1e-6

DN_HEADS = 6
DN_WIDTH = DN_HEADS * HEAD_DIM
DN_CHUNK = 64
DN_BLOCK = 8
DN_SUB = 2

SW_Q_HEADS = 6
SW_KV_HEADS = 2
SW_GROUP = SW_Q_HEADS // SW_KV_HEADS
SW_WIDTH = SW_Q_HEADS * HEAD_DIM
SW_KV_WIDTH = SW_KV_HEADS * HEAD_DIM
SW_BLOCK = 128
ROPE_THETA = 10000.0
NEG_INF = -1e30

HY_CH = D_MODEL - DN_WIDTH - SW_WIDTH
HY_ORDER = 2
HY_BANDS = 16
HY_EMB = 1 + 2 * HY_BANDS
HY_FFN = 64
HY_FAST_DECAY = 0.3
HY_SLOW_DECAY = 1.5
HY_TARGET = 1e-2

N_EXPERTS = 8

LANES = 128
SUBLANES = 8
VMEM_LIMIT = 56 * 1024 * 1024

_IN_SIZES = (3 * DN_WIDTH, DN_WIDTH, N_DIR * DN_HEADS, N_DIR * DN_HEADS, 3 * HY_CH, SW_WIDTH, SW_KV_WIDTH, SW_KV_WIDTH)
_IN_OFF = tuple(int(v) for v in np.cumsum((0,) + _IN_SIZES))
_SEGS = (3 * DN_WIDTH, DN_WIDTH, 3 * HY_CH, SW_WIDTH, SW_KV_WIDTH, SW_KV_WIDTH, LANES)
_SEG_Q, _SEG_K = 3, 4


def _cparams(*sem):
    return pltpu.CompilerParams(dimension_semantics=sem, vmem_limit_bytes=VMEM_LIMIT)


def _dot(a, b):
    return jnp.dot(a, b, preferred_element_type=F32)


def _dot_nt(a, b):
    return lax.dot_general(a, b, (((1,), (1,)), ((), ())), preferred_element_type=F32)


def _split(x):
    hi = x.astype(BF16)
    lo = (x - hi.astype(F32)).astype(BF16)
    return hi, lo


def _dot3(a, b):
    ah, al = _split(a)
    bh, bl = _split(b)
    return _dot(ah, bh) + (_dot(al, bh) + _dot(ah, bl))


def _dot2_exact_rhs(a, b_bf16):
    ah, al = _split(a)
    return _dot(ah, b_bf16) + _dot(al, b_bf16)


def _sigmoid(x):
    return 1.0 / (1.0 + jnp.exp(-x))


def _silu(x):
    return x * _sigmoid(x)


def _softplus(x):
    return jnp.maximum(x, 0.0) + jnp.log(1.0 + jnp.exp(-jnp.abs(x)))


def _rms_mod(x, g, shift, scale):
    y = x * lax.rsqrt(jnp.mean(x * x, axis=-1, keepdims=True) + RMS_EPS)
    return (y * g) * (1.0 + scale) + shift


def _mod_kernel(c_ref, w_ref, b_ref, o_ref):
    o_ref[0] = _dot3(_silu(c_ref[...]), w_ref[0]) + b_ref[0]


def _modulation(c_rows, w_mod, b_mod):
    depth, d, n = w_mod.shape
    rows = c_rows.shape[0]
    tn = 1024
    return pl.pallas_call(
        _mod_kernel,
        grid=(depth, n // tn),
        in_specs=[
            pl.BlockSpec((rows, d), lambda l, j: (0, 0)),
            pl.BlockSpec((1, d, tn), lambda l, j: (l, 0, j)),
            pl.BlockSpec((1, 1, tn), lambda l, j: (l, 0, j)),
        ],
        out_specs=pl.BlockSpec((1, rows, tn), lambda l, j: (l, 0, j)),
        out_shape=jax.ShapeDtypeStruct((depth, rows, n), F32),
        compiler_params=_cparams("parallel", "parallel"),
        name="adaln_mod",
    )(c_rows, w_mod, b_mod.reshape(depth, 1, n))


def _rope_apply(x, cos, sin_signed):
    lane = lax.broadcasted_iota(jnp.int32, x.shape, 1)
    first = (lane & 31) < 16
    partner = jnp.where(first, pltpu.roll(x, LANES - 16, 1), pltpu.roll(x, 16, 1))
    return x * cos + partner * sin_signed


def _in_proj_kernel(x_ref, g_ref, sh_ref, sc_ref, w_ref, *rest, rope):
    if rope:
        cos_ref, sin_ref = rest[:2]
        outs = rest[2:]
    else:
        outs = rest
    h = _rms_mod(x_ref[0], g_ref[...], sh_ref[0], sc_ref[0]).astype(BF16)
    off = 0
    for idx, (o_ref, n) in enumerate(zip(outs, _SEGS)):
        r = _dot(h, w_ref[:, off:off + n])
        if idx == _SEG_Q:
            r = r * (HEAD_DIM ** -0.5)
        if rope and idx in (_SEG_Q, _SEG_K):
            cos, sin = cos_ref[...], sin_ref[...]
            r = jnp.concatenate(
                [_rope_apply(r[:, c:c + LANES], cos, sin) for c in range(0, n, LANES)], axis=1)
        o_ref[0] = r
        off += n


def _in_proj(x, g, shift, scale, w, rope_tabs):
    bn, t, d = x.shape
    tm = min(512, t)
    rope = rope_tabs is not None
    in_specs = [
        pl.BlockSpec((1, tm, d), lambda b, i: (b, i, 0)),
        pl.BlockSpec((1, d), lambda b, i: (0, 0)),
        pl.BlockSpec((1, 1, d), lambda b, i: (b, 0, 0)),
        pl.BlockSpec((1, 1, d), lambda b, i: (b, 0, 0)),
        pl.BlockSpec(w.shape, lambda b, i: (0, 0)),
    ]
    args = [x, g, shift, scale, w]
    if rope:
        in_specs += [pl.BlockSpec((tm, LANES), lambda b, i: (i, 0))] * 2
        args += list(rope_tabs)
    return pl.pallas_call(
        functools.partial(_in_proj_kernel, rope=rope),
        grid=(bn, t // tm),
        in_specs=in_specs,
        out_specs=[pl.BlockSpec((1, tm, n), lambda b, i: (b, i, 0)) for n in _SEGS],
        out_shape=[jax.ShapeDtypeStruct((bn, t, n), F32) for n in _SEGS],
        compiler_params=_cparams("parallel", "parallel"),
        name="in_proj_rope" if rope else "in_proj",
    )(*args)


def _relayout_w_in(w_in):
    o = _IN_OFF
    d = w_in.shape[0]
    pad = jnp.zeros((d, LANES - 2 * N_DIR * DN_HEADS), w_in.dtype)
    cols = [w_in[:, o[0]:o[2]], w_in[:, o[4]:o[8]], w_in[:, o[2]:o[4]], pad]
    return jnp.concatenate(cols, axis=1).astype(BF16)


def _rope_tables(length):
    n = HEAD_DIM // 4
    inv = jnp.power(ROPE_THETA, -jnp.arange(n, dtype=F32) / n)
    t = jnp.arange(length)
    row = (t // GRID_W).astype(F32)[:, None] * inv[None, :]
    col = (t % GRID_W).astype(F32)[:, None] * inv[None, :]
    cos = jnp.concatenate([jnp.cos(row), jnp.cos(row), jnp.cos(col), jnp.cos(col)], axis=1)
    sin = jnp.concatenate([-jnp.sin(row), jnp.sin(row), -jnp.sin(col), jnp.sin(col)], axis=1)
    return jnp.tile(cos, (1, 2)), jnp.tile(sin, (1, 2))


def _conv3_rows(x, prev8, next8, w, first, last):
    tm = x.shape[0]
    row = lax.broadcasted_iota(jnp.int32, x.shape, 0)
    before = jnp.where(first, 0.0, prev8[SUBLANES - 1:SUBLANES, :])
    after = jnp.where(last, 0.0, next8[0:1, :])
    xm = jnp.where(row == 0, before, pltpu.roll(x, 1, 0))
    xp = jnp.where(row == tm - 1, after, pltpu.roll(x, tm - 1, 0))
    return xm * w[0:1, :] + x * w[1:2, :] + xp * w[2:3, :]


def _halo_specs(tm, t, width):
    nb8 = t // SUBLANES
    step = tm // SUBLANES
    return [
        pl.BlockSpec((1, tm, width), lambda b, i: (b, i, 0)),
        pl.BlockSpec((1, SUBLANES, width), lambda b, i: (b, jnp.maximum(i * step - 1, 0), 0)),
        pl.BlockSpec((1, SUBLANES, width), lambda b, i: (b, jnp.minimum((i + 1) * step, nb8 - 1), 0)),
    ]


def _dn_prep_kernel(x_ref, xp_ref, xn_ref, w_ref, ab_ref, al_ref, dt_ref, gsum_ref,
                    q_ref, k_ref, v_ref, g_ref, beta_ref):
    i = pl.program_id(1)
    y = _silu(_conv3_rows(x_ref[0], xp_ref[0], xn_ref[0], w_ref[...], i == 0, i == pl.num_programs(1) - 1))
    q, k, v = y[:, :DN_WIDTH], y[:, DN_WIDTH:2 * DN_WIDTH], y[:, 2 * DN_WIDTH:]
    gs = gsum_ref[...]
    q = q * lax.rsqrt(_dot2_exact_rhs(q * q, gs) + RMS_EPS) * (HEAD_DIM ** -0.5)
    k = k * lax.rsqrt(_dot2_exact_rhs(k * k, gs) + RMS_EPS)
    for h in range(DN_HEADS):
        sl = slice(h * HEAD_DIM, (h + 1) * HEAD_DIM)
        q_ref[0, h] = q[:, sl]
        k_ref[0, h] = k[:, sl]
        v_ref[0, h] = v[:, sl]
    ab = ab_ref[0]
    lane = lax.broadcasted_iota(jnp.int32, ab.shape, 1)
    nh = N_DIR * DN_HEADS
    g_ref[0] = jnp.where(lane < nh, -jnp.exp(al_ref[...]) * _softplus(ab + dt_ref[...]), 0.0)
    beta_ref[0] = jnp.where(lane < nh, _sigmoid(pltpu.roll(ab, LANES - nh, 1)), 0.0)


def _dn_prep(qkv, ab, conv_w, a_log, dt_bias, gsum):
    bn, t, width = qkv.shape
    tm = min(512, t)
    pad = LANES - N_DIR * DN_HEADS
    al = jnp.pad(a_log.reshape(1, -1), ((0, 0), (0, pad)))
    dt = jnp.pad(dt_bias.reshape(1, -1), ((0, 0), (0, pad)))
    head = pl.BlockSpec((1, DN_HEADS, tm, HEAD_DIM), lambda b, i: (b, 0, i, 0))
    row = pl.BlockSpec((1, tm, LANES), lambda b, i: (b, i, 0))
    const = lambda shape: pl.BlockSpec(shape, lambda b, i: (0,) * len(shape))
    head_shape = jax.ShapeDtypeStruct((bn, DN_HEADS, t, HEAD_DIM), F32)
    row_shape = jax.ShapeDtypeStruct((bn, t, LANES), F32)
    return pl.pallas_call(
        _dn_prep_kernel,
        grid=(bn, t // tm),
        in_specs=_halo_specs(tm, t, width) + [const(conv_w.shape), row, const(al.shape), const(dt.shape),
                                              const(gsum.shape)],
        out_specs=[head, head, head, row, row],
        out_shape=[head_shape, head_shape, head_shape, row_shape, row_shape],
        compiler_params=_cparams("parallel", "parallel"),
        name="dn_prep",
    )(qkv, qkv, qkv, conv_w, ab, al, dt, gsum)


def _cumsum_rows(x, reverse):
    c = x.shape[0]
    row = lax.broadcasted_iota(jnp.int32, x.shape, 0)
    s = 1
    while s < c:
        if reverse:
            x = x + jnp.where(row < c - s, pltpu.roll(x, c - s, 0), 0.0)
        else:
            x = x + jnp.where(row >= s, pltpu.roll(x, s, 0), 0.0)
        s *= 2
    return x


def _mm1(ps, cs):
    return [_dot(p.astype(BF16), c.astype(BF16)) for p, c in zip(ps, cs)]


def _mm3(ps, cs):
    out = []
    for p, c in zip(ps, cs):
        c16 = c.astype(BF16)
        c_lo = (c - c16.astype(F32)).astype(BF16)
        p_hi = p.astype(BF16).astype(F32)
        lhs = jnp.concatenate([p_hi, p - p_hi, p_hi], axis=1).astype(BF16)
        out.append(_dot(lhs, jnp.concatenate([c16, c16, c_lo], axis=0)))
    return out


def _dn_chunks(probs):
    c, d = DN_CHUNK, HEAD_DIM
    ii = lax.broadcasted_iota(jnp.int32, (c, c), 0)
    jj = lax.broadcasted_iota(jnp.int32, (c, c), 1)
    di = lax.broadcasted_iota(jnp.int32, (d, d), 0)
    dj = lax.broadcasted_iota(jnp.int32, (d, d), 1)
    n = len(probs)
    k16 = [pr["k"].astype(BF16) for pr in probs]
    kk = [_dot_nt(k16[i], k16[i]) for i in range(n)]
    qk = [_dot_nt(probs[i]["q"].astype(BF16), k16[i]) for i in range(n)]
    kt = [pr["k"].T for pr in probs]
    decay, e_cum, x, p = [], [], [], []
    for i, pr in enumerate(probs):
        incl = (ii >= jj) if pr["lower"] else (ii <= jj)
        strict = (ii > jj) if pr["lower"] else (ii < jj)
        dec = jnp.where(incl, jnp.exp(jnp.where(incl, pr["cum_col"] - pr["cum_row"], 0.0)), 0.0)
        ec = jnp.exp(pr["cum_col"])
        decay.append(dec)
        e_cum.append(ec)
        x.append(jnp.concatenate([pr["v"] * pr["beta_col"], pr["k"] * pr["beta_col"] * ec], axis=1))
        p.append(jnp.where(strict, -(kk[i] * pr["beta_col"] * dec), 0.0))
    same_blk = (ii // DN_BLOCK) == (jj // DN_BLOCK)
    eye = jnp.where(ii == jj, 1.0, 0.0)
    p0 = [jnp.where(same_blk, p[i], 0.0) for i in range(n)]
    a_off = [jnp.where(same_blk, 0.0, -p[i]) for i in range(n)]
    q = _mm1(p0, p0)
    xd = [eye + p0[i] for i in range(n)]
    lvl = 2
    while lvl < DN_BLOCK:
        last = 2 * lvl >= DN_BLOCK
        r = _mm1(q, xd if last else [jnp.concatenate([xd[i], q[i]], axis=1) for i in range(n)])
        xd = [xd[i] + r[i][:, :c] for i in range(n)]
        if not last:
            q = [r[i][:, c:] for i in range(n)]
        lvl *= 2
    r = _mm1(xd, [jnp.concatenate([x[i], a_off[i]], axis=1) for i in range(n)])
    x = [r[i][:, :2 * d] for i in range(n)]
    q = [-r[i][:, 2 * d:] for i in range(n)]
    lvl = 1
    while lvl < c // DN_BLOCK:
        last = 2 * lvl >= c // DN_BLOCK
        r = _mm3(q, x if last else [jnp.concatenate([x[i], q[i]], axis=1) for i in range(n)])
        x = [x[i] + r[i][:, :2 * d] for i in range(n)]
        if not last:
            q = [r[i][:, 2 * d:] for i in range(n)]
        lvl *= 2
    lhs = [jnp.concatenate([qk[i] * decay[i], kt[i] * jnp.exp(probs[i]["tot"] - probs[i]["cum_row"])], axis=0)
           for i in range(n)]
    r = [_dot(lhs[i].astype(BF16), x[i].astype(BF16)) for i in range(n)]
    lhs = [jnp.concatenate([jnp.where(di == dj, jnp.exp(probs[i]["tot"]), 0.0) - r[i][c:, d:],
                            probs[i]["q"] * e_cum[i] - r[i][:c, d:]], axis=0).astype(BF16) for i in range(n)]
    return [(lhs[i], r[i][c:, :d], r[i][:c, :d]) for i in range(n)]


def _dn_scan_kernel(qf_ref, kf_ref, vf_ref, gf_ref, bf_ref, qb_ref, kb_ref, vb_ref, gb_ref, bb_ref,
                    s0_ref, of_ref, ob_ref, sfin_ref, s_ref):
    t = pl.program_id(1)
    c, d = DN_CHUNK, HEAD_DIM

    @pl.when(t == 0)
    def _():
        s_ref[...] = s0_ref[0]

    probs = []
    for sub in range(DN_SUB):
        for d_idx, (q_ref, k_ref, v_ref, g_ref, b_ref) in enumerate(
                ((qf_ref, kf_ref, vf_ref, gf_ref, bf_ref), (qb_ref, kb_ref, vb_ref, gb_ref, bb_ref))):
            lower = d_idx == 0
            rows = pl.ds((sub if lower else DN_SUB - 1 - sub) * c, c)
            cum = _cumsum_rows(g_ref[0, rows, :], reverse=not lower)
            cum_t = cum.T
            beta = b_ref[0, rows, :]
            last = c - 1 if lower else 0
            for h in range(DN_HEADS):
                col = d_idx * DN_HEADS + h
                cum_row = cum_t[col:col + 1, :]
                probs.append(dict(q=q_ref[0, h, rows, :], k=k_ref[0, h, rows, :], v=v_ref[0, h, rows, :],
                                  lower=lower, cum_col=cum[:, col:col + 1], cum_row=cum_row,
                                  tot=cum_row[:, last:last + 1], beta_col=beta[:, col:col + 1]))
    local = _dn_chunks(probs)

    state = [s_ref[d_idx, h] for d_idx in range(N_DIR) for h in range(DN_HEADS)]
    for sub in range(DN_SUB):
        base = sub * N_DIR * DN_HEADS
        r = [_dot(local[base + i][0], state[i].astype(BF16)) for i in range(len(state))]
        state = [r[i][:d] + local[base + i][1] for i in range(len(state))]
        for d_idx, o_ref in enumerate((of_ref, ob_ref)):
            row0 = (sub if d_idx == 0 else DN_SUB - 1 - sub) * c
            o_ref[0, row0:row0 + c, :] = jnp.concatenate(
                [r[d_idx * DN_HEADS + h][d:] + local[base + d_idx * DN_HEADS + h][2] for h in range(DN_HEADS)],
                axis=1)
    for d_idx in range(N_DIR):
        for h in range(DN_HEADS):
            s_ref[d_idx, h] = state[d_idx * DN_HEADS + h]

    @pl.when(t == pl.num_programs(1) - 1)
    def _():
        sfin_ref[0] = s_ref[...]


def _dn_scan(q, k, v, g, beta, s0):
    bn, nh, t, d = q.shape
    rows = DN_SUB * DN_CHUNK
    n = t // rows
    head_f = pl.BlockSpec((1, nh, rows, d), lambda b, i: (b, 0, i, 0))
    head_b = pl.BlockSpec((1, nh, rows, d), lambda b, i: (b, 0, n - 1 - i, 0))
    row_f = pl.BlockSpec((1, rows, LANES), lambda b, i: (b, i, 0))
    row_b = pl.BlockSpec((1, rows, LANES), lambda b, i: (b, n - 1 - i, 0))
    state = pl.BlockSpec((1, N_DIR, nh, d, d), lambda b, i: (b, 0, 0, 0, 0))
    out_f = pl.BlockSpec((1, rows, nh * d), lambda b, i: (b, i, 0))
    out_b = pl.BlockSpec((1, rows, nh * d), lambda b, i: (b, n - 1 - i, 0))
    o_shape = jax.ShapeDtypeStruct((bn, t, nh * d), F32)
    return pl.pallas_call(
        _dn_scan_kernel,
        grid=(bn, n),
        in_specs=[head_f, head_f, head_f, row_f, row_f, head_b, head_b, head_b, row_b, row_b, state],
        out_specs=[out_f, out_b, state],
        out_shape=[o_shape, o_shape, jax.ShapeDtypeStruct(s0.shape, F32)],
        scratch_shapes=[pltpu.VMEM((N_DIR, nh, d, d), F32)],
        compiler_params=_cparams("parallel", "arbitrary"),
        name="dn_scan",
    )(q, k, v, g, beta, q, k, v, g, beta, s0)


def _hy_prep_kernel(x_ref, xp_ref, xn_ref, w_ref, v_ref, v16_ref, x1_ref, x2_ref):
    i = pl.program_id(1)
    y = _conv3_rows(x_ref[0], xp_ref[0], xn_ref[0], w_ref[...], i == 0, i == pl.num_programs(1) - 1)
    v = y[:, :HY_CH]
    v_ref[...] = v
    v16_ref[...] = v.astype(BF16)
    x1_ref[...] = y[:, HY_CH:2 * HY_CH]
    x2_ref[...] = y[:, 2 * HY_CH:]


def _hy_prep(z, conv_w):
    bn, t, width = z.shape
    tm = min(512, t)
    out = pl.BlockSpec((tm, HY_CH), lambda b, i: (i, b))
    f32 = jax.ShapeDtypeStruct((t, bn * HY_CH), F32)
    return pl.pallas_call(
        _hy_prep_kernel,
        grid=(bn, t // tm),
        in_specs=_halo_specs(tm, t, width) + [pl.BlockSpec(conv_w.shape, lambda b, i: (0, 0))],
        out_specs=[out, out, out, out],
        out_shape=[f32, jax.ShapeDtypeStruct((t, bn * HY_CH), BF16), f32, f32],
        compiler_params=_cparams("parallel", "parallel"),
        name="hy_prep",
    )(z, z, z, conv_w)


def _dft_matrices(length):
    n = 2 * length
    k = jnp.arange(length, dtype=jnp.int32)
    ang = ((k[:, None] * k[None, :]) % n).astype(F32) * (2.0 * math.pi / n)
    alt = (1 - 2 * (k % 2)).astype(F32)
    sin = jnp.where(k[:, None] == 0, alt[None, :], jnp.sin(ang))
    return jnp.cos(ang).astype(BF16), sin.astype(BF16)


def _hy_filter_kernel(f_ref, w1_ref, b1_ref, f1_ref, w2_ref, b2_ref, f2_ref, w3_ref, b3_ref, dl_ref,
                      p_ref, q_ref, ssq_ref, nyq_ref, taps_ref):
    i = pl.program_id(1)
    feats = f_ref[...]
    h = jnp.sin(f1_ref[0] * (_dot3(feats, w1_ref[0]) + b1_ref[0]))
    h = jnp.sin(f2_ref[0] * (_dot3(h, w2_ref[0]) + b2_ref[0]))
    h = _dot3(h, w3_ref[0]) + b3_ref[0]
    win = jnp.exp(-feats[:, 0:1] * dl_ref[...])
    half = HY_ORDER * HY_CH
    fwd = h[:, :half] * jnp.concatenate([win] * HY_ORDER, axis=1)
    bwd = h[:, half:] * jnp.concatenate([win] * HY_ORDER, axis=1)
    row = lax.broadcasted_iota(jnp.int32, bwd.shape, 0)
    bwd = jnp.where((row == 0) & (i == 0), 0.0, bwd)
    p = fwd + bwd
    p_ref[0] = p.astype(BF16)
    q_ref[0] = (bwd - fwd).astype(BF16)
    taps_ref[0] = jnp.concatenate([fwd, bwd], axis=1).astype(BF16)
    alt = (1 - 2 * (row & 1)).astype(F32)

    @pl.when(i == 0)
    def _():
        ssq_ref[0] = jnp.zeros_like(ssq_ref[0])
        nyq_ref[0] = jnp.zeros_like(nyq_ref[0])

    ssq_ref[0] += jnp.sum(fwd * fwd + bwd * bwd, axis=0, keepdims=True)
    nyq_ref[0] += jnp.sum(p * alt, axis=0, keepdims=True)


def _hy_spec_kernel(c_ref, s_ref, p_ref, q_ref, ssq_ref, nyq_ref, hre_ref, him_ref):
    i = pl.program_id(1)
    scale = lax.rsqrt(ssq_ref[0] + RMS_EPS)
    hre = _dot(c_ref[...], p_ref[0]) * scale
    him = _dot(s_ref[...], q_ref[0]) * scale
    row = lax.broadcasted_iota(jnp.int32, him.shape, 0)
    hre_ref[0] = hre
    him_ref[0] = jnp.where((row == 0) & (i == 0), nyq_ref[0] * scale, him)


def _hy_taps(length, w1, b1, f1, w2, b2, f2, w3, b3):
    depth = w1.shape[0]
    t = jnp.linspace(0.0, 1.0, length, dtype=F32)[:, None]
    omega = 2.0 * math.pi * jnp.arange(length, dtype=F32) / length
    bands = jnp.linspace(1e-4, HY_BANDS - 1, HY_BANDS, dtype=F32)
    ang = omega[:, None] * bands[None, :]
    feats = jnp.concatenate([t, jnp.cos(ang), -jnp.sin(ang), jnp.zeros((length, LANES - HY_EMB), F32)], axis=-1)
    w1p = jnp.pad(w1, ((0, 0), (0, LANES - HY_EMB), (0, 0)))
    max_decay = math.log(HY_TARGET) / HY_FAST_DECAY
    min_decay = math.log(HY_TARGET) / HY_SLOW_DECAY
    deltas = jnp.abs(jnp.linspace(min_decay, max_decay, HY_CH, dtype=F32))[None, :]
    half = HY_ORDER * HY_CH
    tm = min(512, length)
    vec = lambda a: a.reshape(depth, 1, -1)
    lay = lambda shape: pl.BlockSpec((1,) + shape, lambda l, i: (l,) + (0,) * len(shape))
    rows = lambda n: pl.BlockSpec((1, tm, n), lambda l, i: (l, i, 0))
    return pl.pallas_call(
        _hy_filter_kernel,
        grid=(depth, length // tm),
        in_specs=[pl.BlockSpec((tm, LANES), lambda l, i: (i, 0)),
                  lay((LANES, HY_FFN)), lay((1, HY_FFN)), lay((1, HY_FFN)),
                  lay((HY_FFN, HY_FFN)), lay((1, HY_FFN)), lay((1, HY_FFN)),
                  lay((HY_FFN, 2 * half)), lay((1, 2 * half)),
                  pl.BlockSpec((1, HY_CH), lambda l, i: (0, 0))],
        out_specs=[rows(half), rows(half), lay((1, half)), lay((1, half)), rows(2 * half)],
        out_shape=[jax.ShapeDtypeStruct((depth, length, half), BF16),
                   jax.ShapeDtypeStruct((depth, length, half), BF16),
                   jax.ShapeDtypeStruct((depth, 1, half), F32),
                   jax.ShapeDtypeStruct((depth, 1, half), F32),
                   jax.ShapeDtypeStruct((depth, length, 2 * half), BF16)],
        compiler_params=_cparams("parallel", "arbitrary"),
        name="hy_filter",
    )(feats, w1p, vec(b1), vec(f1), w2, vec(b2), vec(f2), w3, vec(b3), deltas)


def _hy_spec_dense(cmat, smat, p, q, ssq, nyq):
    depth, length, half = p.shape
    lay = lambda shape: pl.BlockSpec((1,) + shape, lambda l, i: (l,) + (0,) * len(shape))
    tk = min(512, length)
    return pl.pallas_call(
        _hy_spec_kernel,
        grid=(depth, length // tk),
        in_specs=[pl.BlockSpec((tk, length), lambda l, i: (i, 0)),
                  pl.BlockSpec((tk, length), lambda l, i: (i, 0)),
                  lay((length, half)), lay((length, half)), lay((1, half)), lay((1, half))],
        out_specs=[pl.BlockSpec((1, tk, half), lambda l, i: (l, i, 0))] * 2,
        out_shape=[jax.ShapeDtypeStruct((depth, length, half), F32)] * 2,
        compiler_params=_cparams("parallel", "parallel"),
        name="hy_spec",
    )(cmat, smat, p, q, ssq, nyq)


def _hy_fwd_kernel(c_ref, s_ref, u_ref, hre_ref, him_ref, yre_ref, yim_ref, *, n_points, n_batch):
    i = pl.program_id(0)
    ure = _dot(c_ref[...], u_ref[...])
    usn = _dot(s_ref[...], u_ref[...])
    hre = jnp.concatenate([hre_ref[...]] * n_batch, axis=1)
    him = jnp.concatenate([him_ref[...]] * n_batch, axis=1)
    row0 = (lax.broadcasted_iota(jnp.int32, ure.shape, 0) == 0) & (i == 0)
    yre = jnp.where(row0, ure * hre * (1.0 / n_points), (ure * hre + usn * him) * (2.0 / n_points))
    yim = jnp.where(row0, usn * him * (1.0 / n_points), (usn * hre - ure * him) * (2.0 / n_points))
    yre_ref[...] = yre.astype(BF16)
    yim_ref[...] = yim.astype(BF16)


def _hy_inv_kernel(c_ref, s_ref, yre_ref, yim_ref, u_ref, gate_ref, skip_ref, o_ref, *o16_ref, n_batch):
    i = pl.program_id(0)
    tm = c_ref.shape[0]
    conv_c = _dot(c_ref[...], yre_ref[...])
    conv_s = _dot(s_ref[...], yim_ref[...])
    trow = lax.broadcasted_iota(jnp.int32, conv_c.shape, 0) + i * tm
    alt = (1 - 2 * (trow & 1)).astype(F32)
    nyq = yim_ref[0:1, :].astype(F32)
    conv = conv_c + jnp.where(trow == 0, 0.0, conv_s) + alt * nyq
    skip = jnp.concatenate([skip_ref[...]] * n_batch, axis=1)
    y = gate_ref[...] * (conv + skip * u_ref[...])
    o_ref[...] = y
    if o16_ref:
        o16_ref[0][...] = y.astype(BF16)


def _hy_long_conv(cmat, smat, u16, u, gate, hre, him, skip, order, emit_bf16):
    length, cols = u.shape
    n_batch = cols // HY_CH
    tk = min(512, length)
    full = pl.BlockSpec((length, cols), lambda i: (0, 0))
    mat = pl.BlockSpec((tk, length), lambda i: (i, 0))
    tile = pl.BlockSpec((tk, cols), lambda i: (i, 0))
    spec = pl.BlockSpec((tk, HY_CH), lambda i: (i, order))
    yre, yim = pl.pallas_call(
        functools.partial(_hy_fwd_kernel, n_points=2 * length, n_batch=n_batch),
        grid=(length // tk,),
        in_specs=[mat, mat, full, spec, spec],
        out_specs=[tile, tile],
        out_shape=[jax.ShapeDtypeStruct((length, cols), BF16)] * 2,
        compiler_params=_cparams("parallel"),
        name="hy_fwd",
    )(cmat, smat, u16, hre, him)
    tm = min(256, length)
    mat = pl.BlockSpec((tm, length), lambda i: (i, 0))
    tile = pl.BlockSpec((tm, cols), lambda i: (i, 0))
    out_shape = [jax.ShapeDtypeStruct((length, cols), F32)]
    if emit_bf16:
        out_shape.append(jax.ShapeDtypeStruct((length, cols), BF16))
    return pl.pallas_call(
        functools.partial(_hy_inv_kernel, n_batch=n_batch),
        grid=(length // tm,),
        in_specs=[mat, mat, full, full, tile, tile, pl.BlockSpec((1, HY_CH), lambda i: (0, 0))],
        out_specs=[tile] * len(out_shape),
        out_shape=out_shape,
        compiler_params=_cparams("parallel"),
        name="hy_inv",
    )(cmat, smat, yre, yim, u, gate, skip[order][None, :])


def _hyena(z, conv_w, cmat, smat, hre, him, skip):
    v, v16, x1, x2 = _hy_prep(z, conv_w)
    y1, y16 = _hy_long_conv(cmat, smat, v16, v, x1, hre, him, skip, 0, True)
    (y,) = _hy_long_conv(cmat, smat, y16, y1, x2, hre, him, skip, 1, False)
    return y


FFT_N2 = 256
FFT_ROWS = 16


def _fft_consts(length):
    n = 2 * length
    n1 = n // FFT_N2
    nb = length // FFT_N2
    r = n1 // 2 + 1
    rp = -(-2 * r // SUBLANES) * SUBLANES
    kk = np.arange(r)[:, None, None] + n1 * np.arange(FFT_N2)[None, :, None]
    ang2 = 2 * np.pi * ((kk * np.arange(FFT_N2)[None, None, :]) % n) / n
    gc, gs = np.cos(ang2), np.sin(ang2)
    b16 = lambda a: jnp.asarray(a, F32).astype(BF16)
    return dict(r=r, rp=rp, n1=n1, nb=nb, gc=b16(gc), gs=b16(gs),
                gct=b16(gc.transpose(0, 2, 1)), gst=b16(gs.transpose(0, 2, 1)))


def _axpy(acc, coef, t):
    if abs(coef) < 1e-9:
        return acc
    term = t if abs(coef - 1.0) < 1e-9 else (-t if abs(coef + 1.0) < 1e-9 else coef * t)
    return term if acc is None else acc + term


def _nz(t, like):
    return jnp.zeros_like(like) if t is None else t


def _fft_s1f_rows_kernel(v_ref, o_ref, *, n1, nb, r):
    half = n1 // 2
    x = [v_ref[0, b].astype(F32) for b in range(nb)]
    done = set()
    for k in range(half // 2 + 1):
        p = half - k
        ce = co = se = so = None
        for b in range(nb):
            ang = 2.0 * math.pi * ((b * k) % n1) / n1
            if b % 2 == 0:
                ce, se = _axpy(ce, math.cos(ang), x[b]), _axpy(se, math.sin(ang), x[b])
            else:
                co, so = _axpy(co, math.cos(ang), x[b]), _axpy(so, math.sin(ang), x[b])
        ce, co, se, so = (_nz(t, x[0]) for t in (ce, co, se, so))
        o_ref[0, k] = (ce + co).astype(BF16)
        o_ref[0, r + k] = (-(se + so)).astype(BF16)
        done.update((k, r + k))
        if p != k:
            o_ref[0, p] = (ce - co).astype(BF16)
            o_ref[0, r + p] = (se - so).astype(BF16)
            done.update((p, r + p))
    for row in range(o_ref.shape[1]):
        if row not in done:
            o_ref[0, row] = jnp.zeros(o_ref.shape[2:], BF16)


def _fft_s1i_rows_kernel(br_ref, bi_ref, u_ref, gate_ref, skip_ref, o_ref, *o16_ref, n1, nb, n_batch):
    half = n1 // 2
    n = n1 * FFT_N2
    acc = [None] * nb
    for k in range(half // 2 + 1):
        p = half - k
        w = (1.0 if k == 0 else 2.0) / n
        rk, ik = br_ref[k].astype(F32), bi_ref[k].astype(F32)
        if p != k:
            rp_, ip_ = br_ref[p].astype(F32), bi_ref[p].astype(F32)
            r_even, r_odd, i_even, i_odd = rk + rp_, rk - rp_, ik - ip_, ik + ip_
        else:
            r_even = r_odd = rk
            i_even = i_odd = ik
        for b in range(nb):
            ang = 2.0 * math.pi * ((b * k) % n1) / n1
            re, im = (r_even, i_even) if b % 2 == 0 else (r_odd, i_odd)
            acc[b] = _axpy(_axpy(acc[b], w * math.cos(ang), re), -w * math.sin(ang), im)
    skip = jnp.concatenate([skip_ref[...]] * n_batch, axis=1)
    for b in range(nb):
        y = gate_ref[b] * (acc[b] + skip * u_ref[b])
        o_ref[b] = y
        if o16_ref:
            o16_ref[0][b] = y.astype(BF16)


def _fft_stage1_rows(fc, v):
    g, length, cols = v.shape
    r, rp, n1, nb = fc["r"], fc["rp"], fc["n1"], fc["nb"]
    tr = FFT_ROWS
    return pl.pallas_call(
        functools.partial(_fft_s1f_rows_kernel, n1=n1, nb=nb, r=r),
        grid=(g, FFT_N2 // tr),
        in_specs=[pl.BlockSpec((1, nb, tr, cols), lambda l, i: (l, 0, i, 0))],
        out_specs=pl.BlockSpec((1, rp, tr, cols), lambda l, i: (l, 0, i, 0)),
        out_shape=jax.ShapeDtypeStruct((g, rp, FFT_N2, cols), BF16),
        compiler_params=_cparams("parallel", "parallel"),
        name="fft_s1_fwd",
    )(v.reshape(g, nb, FFT_N2, cols))


def _fft_s2f_kernel(ar_ref, ai_ref, gc_ref, gs_ref, *rest, spectrum, n_batch):
    ar, ai, gc, gs = ar_ref[0, 0], ai_ref[0, 0], gc_ref[0], gs_ref[0]
    xr = _dot(gc, ar) + _dot(gs, ai)
    xi = _dot(gc, ai) - _dot(gs, ar)
    if spectrum:
        ssq_ref, hr_ref, hi_ref = rest
        half = xr.shape[1] // 2
        scale = lax.rsqrt(ssq_ref[0] + RMS_EPS)
        hr_ref[0, 0] = (xr[:, :half] + xr[:, half:]) * scale
        hi_ref[0, 0] = (xi[:, :half] - xi[:, half:]) * scale
    else:
        hr_ref, hi_ref, yr_ref, yi_ref = rest
        hr = jnp.concatenate([hr_ref[0]] * n_batch, axis=1)
        hi = jnp.concatenate([hi_ref[0]] * n_batch, axis=1)
        yr_ref[0] = (xr * hr - xi * hi).astype(BF16)
        yi_ref[0] = (xr * hi + xi * hr).astype(BF16)


def _fft_s2i_kernel(yr_ref, yi_ref, gct_ref, gst_ref, br_ref, bi_ref):
    yr, yi, gct, gst = yr_ref[0], yi_ref[0], gct_ref[0], gst_ref[0]
    br_ref[0] = (_dot(gct, yr) - _dot(gst, yi)).astype(BF16)
    bi_ref[0] = (_dot(gct, yi) + _dot(gst, yr)).astype(BF16)


def _hy_spec_fft(fc, taps, ssq):
    depth, length, cols = taps.shape
    r, rp, nb = fc["r"], fc["rp"], fc["nb"]
    half = cols // 2
    a = _fft_stage1_rows(fc, taps)
    res = lambda off: pl.BlockSpec((1, 1, FFT_N2, cols), lambda l, k: (l, k + off, 0, 0))
    g = pl.BlockSpec((1, FFT_N2, FFT_N2), lambda l, k: (k, 0, 0))
    out = pl.BlockSpec((1, 1, FFT_N2, half), lambda l, k: (l, k, 0, 0))
    return pl.pallas_call(
        functools.partial(_fft_s2f_kernel, spectrum=True, n_batch=1),
        grid=(depth, r),
        in_specs=[res(0), res(r), g, g, pl.BlockSpec((1, 1, half), lambda l, k: (l, 0, 0))],
        out_specs=[out, out],
        out_shape=[jax.ShapeDtypeStruct((depth, r, FFT_N2, half), F32)] * 2,
        compiler_params=_cparams("parallel", "parallel"),
        name="fft_spec",
    )(a, a, fc["gc"], fc["gs"], ssq)


def _hy_long_conv_fft(fc, u, gate, hre, him, skip, order):
    length, cols = u.shape
    n_batch = cols // HY_CH
    r, n1, nb = fc["r"], fc["n1"], fc["nb"]
    a = _fft_stage1_rows(fc, u[None])
    res = lambda off: pl.BlockSpec((1, 1, FFT_N2, cols), lambda k: (0, k + off, 0, 0))
    g = pl.BlockSpec((1, FFT_N2, FFT_N2), lambda k: (k, 0, 0))
    spec = pl.BlockSpec((1, FFT_N2, HY_CH), lambda k: (k, 0, order))
    blk = pl.BlockSpec((1, FFT_N2, cols), lambda k: (k, 0, 0))
    yr, yi = pl.pallas_call(
        functools.partial(_fft_s2f_kernel, spectrum=False, n_batch=n_batch),
        grid=(r,),
        in_specs=[res(0), res(r), g, g, spec, spec],
        out_specs=[blk, blk],
        out_shape=[jax.ShapeDtypeStruct((r, FFT_N2, cols), BF16)] * 2,
        compiler_params=_cparams("parallel"),
        name="fft_s2_fwd",
    )(a, a, fc["gc"], fc["gs"], hre, him)
    br, bi = pl.pallas_call(
        _fft_s2i_kernel,
        grid=(r,),
        in_specs=[blk, blk, g, g],
        out_specs=[blk, blk],
        out_shape=[jax.ShapeDtypeStruct((r, FFT_N2, cols), BF16)] * 2,
        compiler_params=_cparams("parallel"),
        name="fft_s2_inv",
    )(yr, yi, fc["gct"], fc["gst"])
    tr = FFT_ROWS
    rows = lambda n: pl.BlockSpec((n, tr, cols), lambda i: (0, i, 0))
    y = pl.pallas_call(
        functools.partial(_fft_s1i_rows_kernel, n1=n1, nb=nb, n_batch=n_batch),
        grid=(FFT_N2 // tr,),
        in_specs=[rows(r), rows(r), rows(nb), rows(nb), pl.BlockSpec((1, HY_CH), lambda i: (0, 0))],
        out_specs=rows(nb),
        out_shape=jax.ShapeDtypeStruct((nb, FFT_N2, cols), F32),
        compiler_params=_cparams("parallel"),
        name="fft_s1_inv",
    )(br, bi, u.reshape(nb, FFT_N2, cols), gate.reshape(nb, FFT_N2, cols), skip[order][None, :])
    return y.reshape(length, cols)


def _hyena_fft(z, conv_w, fc, hre, him, skip):
    v, _, x1, x2 = _hy_prep(z, conv_w)
    y1 = _hy_long_conv_fft(fc, v, x1, hre, him, skip, 0)
    return _hy_long_conv_fft(fc, y1, x2, hre, him, skip, 1)


def _attn_kernel(sink_ref, q_ref, *rest, local, n_blocks):
    if local:
        kp_ref, kc_ref, kn_ref, vp_ref, vc_ref, vn_ref, kx_ref, vx_ref, o_ref = rest
        k_all = jnp.concatenate([kp_ref[0], kc_ref[0], kn_ref[0], kx_ref[0]], axis=0)
        v_all = jnp.concatenate([vp_ref[0], vc_ref[0], vn_ref[0], vx_ref[0]], axis=0)
    else:
        kx_ref, vx_ref, o_ref = rest
        k_all, v_all = kx_ref[0], vx_ref[0]
    blk = pl.program_id(1)
    q = q_ref[0]
    tq = q.shape[0]
    nk = k_all.shape[0]
    k16, v16 = k_all.astype(BF16), v_all.astype(BF16)
    lane = lax.broadcasted_iota(jnp.int32, (tq, LANES), 1)
    rows = SW_GROUP * tq
    r_idx = lax.broadcasted_iota(jnp.int32, (rows, nk), 0)
    if local:
        r = r_idx & (tq - 1)
        c = lax.broadcasted_iota(jnp.int32, (rows, nk), 1)
        rel = c - r - SW_BLOCK
        valid = (rel >= -SW_BLOCK) & (rel <= SW_BLOCK)
        valid = valid & ((c >= SW_BLOCK) | (blk > 0)) & ((c < 2 * SW_BLOCK) | (blk < n_blocks - 1))
        valid = valid | (c >= 3 * SW_BLOCK)
    out_heads = [None] * SW_Q_HEADS
    for j in range(SW_KV_HEADS):
        keep = (lane >= j * HEAD_DIM) & (lane < (j + 1) * HEAD_DIM)
        parts = []
        for g in range(SW_GROUP):
            hq = j * SW_GROUP + g
            chunk = q[:, (hq // 2) * LANES:(hq // 2 + 1) * LANES]
            if hq % 2 != j:
                chunk = pltpu.roll(chunk, HEAD_DIM, 1)
            parts.append(jnp.where(keep, chunk, 0.0))
        qz = jnp.concatenate(parts, axis=0).astype(BF16)
        s = _dot_nt(qz, k16)
        if local:
            s = jnp.where(valid, s, NEG_INF)
        sink = jnp.where(r_idx[:, 0:1] < tq, sink_ref[j * SW_GROUP],
                         jnp.where(r_idx[:, 0:1] < 2 * tq, sink_ref[j * SW_GROUP + 1], sink_ref[j * SW_GROUP + 2]))
        m = jnp.maximum(jnp.max(s, axis=-1, keepdims=True), sink)
        p = jnp.exp(s - m)
        denom = jnp.sum(p, axis=-1, keepdims=True) + jnp.exp(sink - m)
        o = _dot(p.astype(BF16), v16) / denom
        for g in range(SW_GROUP):
            hq = j * SW_GROUP + g
            og = o[g * tq:(g + 1) * tq]
            if hq % 2 != j:
                og = pltpu.roll(og, HEAD_DIM, 1)
            out_heads[hq] = og
    first_half = lane < HEAD_DIM
    o_ref[0] = jnp.concatenate(
        [jnp.where(first_half, out_heads[2 * c], out_heads[2 * c + 1]) for c in range(SW_Q_HEADS // 2)], axis=1)


def _attention(q, k, v, kx, vx, sink, local):
    bn, t, _ = q.shape
    nb = t // SW_BLOCK
    tx = kx.shape[1]
    qspec = pl.BlockSpec((1, SW_BLOCK, SW_WIDTH), lambda b, i: (b, i, 0))
    xspec = pl.BlockSpec((1, tx, SW_KV_WIDTH), lambda b, i: (b, 0, 0))
    in_specs = [pl.BlockSpec(memory_space=pltpu.SMEM), qspec]
    args = [sink, q]
    if local:
        prev = pl.BlockSpec((1, SW_BLOCK, SW_KV_WIDTH), lambda b, i: (b, jnp.maximum(i - 1, 0), 0))
        cur = pl.BlockSpec((1, SW_BLOCK, SW_KV_WIDTH), lambda b, i: (b, i, 0))
        nxt = pl.BlockSpec((1, SW_BLOCK, SW_KV_WIDTH), lambda b, i: (b, jnp.minimum(i + 1, nb - 1), 0))
        in_specs += [prev, cur, nxt, prev, cur, nxt]
        args += [k, k, k, v, v, v]
    in_specs += [xspec, xspec]
    args += [kx, vx]
    return pl.pallas_call(
        functools.partial(_attn_kernel, local=local, n_blocks=nb),
        grid=(bn, nb),
        in_specs=in_specs,
        out_specs=qspec,
        out_shape=jax.ShapeDtypeStruct((bn, t, SW_WIDTH), F32),
        compiler_params=_cparams("parallel", "parallel"),
        name="attn_local" if local else "attn_ctx",
    )(*args)


def _out_proj_kernel(of_ref, ob_ref, z_ref, gn_ref, gsum_ref, hy_ref, sw_ref, x_ref, gate_ref, w_ref, o_ref):
    o = of_ref[0] + ob_ref[0]
    mean = _dot2_exact_rhs(o * o, gsum_ref[...]) * (1.0 / HEAD_DIM)
    dn = o * lax.rsqrt(mean + RMS_EPS) * gn_ref[...] * _silu(z_ref[0])
    a, b = DN_WIDTH, DN_WIDTH + HY_CH
    mix = _dot(dn.astype(BF16), w_ref[0:a, :])
    mix += _dot(hy_ref[...].astype(BF16), w_ref[a:b, :])
    mix += _dot(sw_ref[0].astype(BF16), w_ref[b:, :])
    o_ref[0] = x_ref[0] + gate_ref[0] * mix


def _out_proj(o_f, o_b, z, gn, gsum, hy, sw, x, gate, w):
    bn, t, d = x.shape
    tm = min(512, t)
    seq = lambda n: pl.BlockSpec((1, tm, n), lambda b, i: (b, i, 0))
    const = lambda shape: pl.BlockSpec(shape, lambda b, i: (0,) * len(shape))
    return pl.pallas_call(
        _out_proj_kernel,
        grid=(bn, t // tm),
        in_specs=[seq(DN_WIDTH), seq(DN_WIDTH), seq(DN_WIDTH), const(gn.shape), const(gsum.shape),
                  pl.BlockSpec((tm, HY_CH), lambda b, i: (i, b)), seq(SW_WIDTH), seq(d),
                  pl.BlockSpec((1, 1, d), lambda b, i: (b, 0, 0)), const(w.shape)],
        out_specs=seq(d),
        out_shape=jax.ShapeDtypeStruct(x.shape, F32),
        compiler_params=_cparams("parallel", "parallel"),
        name="out_proj",
    )(o_f, o_b, z, gn, gsum, hy, sw, x, gate, w)


def _ffn_kernel(x_ref, g_ref, sh_ref, sc_ref, gate_ref, wg_ref, wu_ref, wd_ref, o_ref, h_ref, acc_ref):
    j = pl.program_id(2)

    @pl.when(j == 0)
    def _():
        h_ref[...] = _rms_mod(x_ref[0], g_ref[...], sh_ref[0], sc_ref[0]).astype(BF16)
        acc_ref[...] = jnp.zeros_like(acc_ref)

    h = h_ref[...]
    act = _silu(_dot(h, wg_ref[...])) * _dot(h, wu_ref[...])
    acc_ref[...] += _dot(act.astype(BF16), wd_ref[...])

    @pl.when(j == pl.num_programs(2) - 1)
    def _():
        o_ref[0] = x_ref[0] + gate_ref[0] * acc_ref[...]


def _ffn(x, g, shift, scale, gate, wg, wu, wd):
    bn, t, d = x.shape
    ff = wg.shape[1]
    tm = min(512, t)
    tf = ff // 2
    seq = pl.BlockSpec((1, tm, d), lambda b, i, j: (b, i, 0))
    mod = pl.BlockSpec((1, 1, d), lambda b, i, j: (b, 0, 0))
    return pl.pallas_call(
        _ffn_kernel,
        grid=(bn, t // tm, ff // tf),
        in_specs=[seq, pl.BlockSpec((1, d), lambda b, i, j: (0, 0)), mod, mod, mod,
                  pl.BlockSpec((d, tf), lambda b, i, j: (0, j)),
                  pl.BlockSpec((d, tf), lambda b, i, j: (0, j)),
                  pl.BlockSpec((tf, d), lambda b, i, j: (j, 0))],
        out_specs=seq,
        out_shape=jax.ShapeDtypeStruct(x.shape, F32),
        scratch_shapes=[pltpu.VMEM((tm, d), BF16), pltpu.VMEM((tm, d), F32)],
        compiler_params=_cparams("parallel", "parallel", "arbitrary"),
        name="ffn_dense",
    )(x, g, shift, scale, gate, wg, wu, wd)


MOE_TILE = 512
_INFO_E, _INFO_W, _INFO_RANK = 0, 2, 4


def _route_kernel(x_ref, g_ref, sh_ref, sc_ref, r_ref, h_ref, info_ref, cnt_ref, carry_ref):
    @pl.when((pl.program_id(0) == 0) & (pl.program_id(1) == 0))
    def _():
        carry_ref[...] = jnp.zeros_like(carry_ref)

    h = _rms_mod(x_ref[0], g_ref[...], sh_ref[0], sc_ref[0])
    h_ref[0] = h
    logits = _dot3(h, r_ref[...])
    tm = logits.shape[0]
    lane = lax.broadcasted_iota(jnp.int32, logits.shape, 1)
    lg = jnp.where(lane < N_EXPERTS, logits, NEG_INF)
    m1 = jnp.max(lg, axis=-1, keepdims=True)
    i1 = jnp.min(jnp.where(lg == m1, lane, LANES), axis=-1, keepdims=True)
    lg2 = jnp.where(lane == i1, NEG_INF, lg)
    m2 = jnp.max(lg2, axis=-1, keepdims=True)
    i2 = jnp.min(jnp.where(lg2 == m2, lane, LANES), axis=-1, keepdims=True)
    e2 = jnp.exp(m2 - m1)
    w1 = 1.0 / (1.0 + e2)
    w2 = e2 * w1
    chosen = jnp.where((lane == i1) | (lane == i2), 1.0, 0.0)
    rr = lax.broadcasted_iota(jnp.int32, (tm, tm), 0)
    cc = lax.broadcasted_iota(jnp.int32, (tm, tm), 1)
    earlier = jnp.where(rr > cc, 1.0, 0.0).astype(BF16)
    before = _dot(earlier, chosen.astype(BF16)) + carry_ref[...]
    rank1 = jnp.sum(jnp.where(lane == i1, before, 0.0), axis=-1, keepdims=True)
    rank2 = jnp.sum(jnp.where(lane == i2, before, 0.0), axis=-1, keepdims=True)
    info = jnp.zeros_like(logits)
    for pos, val in ((_INFO_E, i1.astype(F32)), (_INFO_E + 1, i2.astype(F32)), (_INFO_W, w1), (_INFO_W + 1, w2),
                     (_INFO_RANK, rank1), (_INFO_RANK + 1, rank2)):
        info = jnp.where(lane == pos, val, info)
    info_ref[0] = info
    carry_ref[...] += jnp.sum(chosen, axis=0, keepdims=True)
    cnt_ref[...] = carry_ref[...]


def _row_copies(src_row, dst_row, sem, tm, wait):
    def body(r, carry):
        for c in range(2):
            cp = pltpu.make_async_copy(src_row(r, c), dst_row(r, c), sem.at[c])
            if wait:
                cp.wait()
            else:
                cp.start(priority=c)
        return carry
    lax.fori_loop(0, tm, body, 0, unroll=8)


def _scatter_kernel(dest_ref, h_ref, xs_in_ref, xs_ref, sem):
    del xs_in_ref
    tm = h_ref.shape[1]
    src = lambda r, c: h_ref.at[0, pl.ds(r, 1)]
    dst = lambda r, c: xs_ref.at[pl.ds(dest_ref[0, 0, c * tm + r], 1)]
    _row_copies(src, dst, sem, tm, wait=False)
    _row_copies(src, dst, sem, tm, wait=True)


def _group_ffn_kernel(te_ref, tb_ref, nv_ref, xs_ref, wg_ref, wu_ref, wd_ref, ys_ref, x16_ref, acc_ref):
    j = pl.program_id(0)
    f = pl.program_id(1)

    @pl.when(j < nv_ref[0])
    def _():
        @pl.when(f == 0)
        def _():
            x16_ref[...] = xs_ref[...].astype(BF16)
            acc_ref[...] = jnp.zeros_like(acc_ref)

        h = x16_ref[...]
        act = _silu(_dot(h, wg_ref[0])) * _dot(h, wu_ref[0])
        acc_ref[...] += _dot(act.astype(BF16), wd_ref[0])

        @pl.when(f == pl.num_programs(1) - 1)
        def _():
            ys_ref[...] = acc_ref[...]

    @pl.when(j >= nv_ref[0])
    def _():
        ys_ref[...] = jnp.zeros_like(ys_ref)


def _combine_kernel(dest_ref, x_ref, gate_ref, info_ref, ys_ref, o_ref, buf_ref, sem):
    tm = x_ref.shape[1]
    src = lambda r, c: ys_ref.at[pl.ds(dest_ref[0, 0, c * tm + r], 1)]
    dst = lambda r, c: buf_ref.at[c, pl.ds(r, 1)]
    _row_copies(src, dst, sem, tm, wait=False)
    _row_copies(src, dst, sem, tm, wait=True)
    info = info_ref[0]
    mix = info[:, _INFO_W:_INFO_W + 1] * buf_ref[0] + info[:, _INFO_W + 1:_INFO_W + 2] * buf_ref[1]
    o_ref[0] = x_ref[0] + gate_ref[0] * mix


def _moe(x, g, shift, scale, gate, router, wg, wu, wd):
    bn, t, d = x.shape
    ne, _, ff = wg.shape
    tm = MOE_TILE
    nt = t // tm
    n_tok = bn * t
    n_slots = 2 * n_tok // tm + ne
    seq = pl.BlockSpec((1, tm, d), lambda b, i: (b, i, 0))
    mod = pl.BlockSpec((1, 1, d), lambda b, i: (b, 0, 0))
    rec = pl.BlockSpec((1, tm, LANES), lambda b, i: (b, i, 0))
    h, info, counts = pl.pallas_call(
        _route_kernel,
        grid=(bn, nt),
        in_specs=[seq, pl.BlockSpec((1, d), lambda b, i: (0, 0)), mod, mod,
                  pl.BlockSpec(router.shape, lambda b, i: (0, 0))],
        out_specs=[seq, rec, pl.BlockSpec((1, LANES), lambda b, i: (0, 0))],
        out_shape=[jax.ShapeDtypeStruct(x.shape, F32), jax.ShapeDtypeStruct((bn, t, LANES), F32),
                   jax.ShapeDtypeStruct((1, LANES), F32)],
        scratch_shapes=[pltpu.VMEM((1, LANES), F32)],
        compiler_params=_cparams("arbitrary", "arbitrary"),
        name="moe_route",
    )(x, g, shift, scale, router)

    flat = info.reshape(n_tok, LANES)
    expert = flat[:, _INFO_E:_INFO_E + 2].astype(jnp.int32)
    rank = flat[:, _INFO_RANK:_INFO_RANK + 2].astype(jnp.int32)
    tiles_e = (counts[0, :ne].astype(jnp.int32) + tm - 1) // tm
    ends = jnp.cumsum(tiles_e)
    dest = (ends - tiles_e)[expert] * tm + rank
    dest = dest.reshape(bn * nt, tm, 2).transpose(0, 2, 1).reshape(bn * nt, 1, 2 * tm)
    n_valid = ends[-1]
    slot = jnp.minimum(jnp.arange(n_slots, dtype=jnp.int32), n_valid - 1)
    slot_expert = jnp.minimum(jnp.searchsorted(ends, slot, side="right"), ne - 1).astype(jnp.int32)

    dspec = pl.BlockSpec((1, 1, 2 * tm), lambda b, i: (b * nt + i, 0, 0), memory_space=pltpu.SMEM)
    anyspec = pl.BlockSpec(memory_space=pl.ANY)
    xs = pl.pallas_call(
        _scatter_kernel,
        grid=(bn, nt),
        in_specs=[dspec, seq, anyspec],
        out_specs=anyspec,
        out_shape=jax.ShapeDtypeStruct((n_slots * tm, d), F32),
        scratch_shapes=[pltpu.SemaphoreType.DMA((2,))],
        input_output_aliases={2: 0},
        compiler_params=_cparams("arbitrary", "arbitrary"),
        name="moe_scatter",
    )(dest, h, jnp.zeros((n_slots * tm, d), F32))

    nf = 2
    tf = ff // nf
    last = nf - 1
    fidx = lambda j, f, nv: jnp.where(j < nv[0], f, last)
    ys = pl.pallas_call(
        _group_ffn_kernel,
        grid_spec=pltpu.PrefetchScalarGridSpec(
            num_scalar_prefetch=3,
            grid=(n_slots, nf),
            in_specs=[pl.BlockSpec((tm, d), lambda j, f, te, tb, nv: (tb[j], 0)),
                      pl.BlockSpec((1, d, tf), lambda j, f, te, tb, nv: (te[j], 0, fidx(j, f, nv))),
                      pl.BlockSpec((1, d, tf), lambda j, f, te, tb, nv: (te[j], 0, fidx(j, f, nv))),
                      pl.BlockSpec((1, tf, d), lambda j, f, te, tb, nv: (te[j], fidx(j, f, nv), 0))],
            out_specs=pl.BlockSpec((tm, d), lambda j, f, te, tb, nv: (j, 0)),
            scratch_shapes=[pltpu.VMEM((tm, d), BF16), pltpu.VMEM((tm, d), F32)]),
        out_shape=jax.ShapeDtypeStruct((n_slots * tm, d), F32),
        compiler_params=_cparams("arbitrary", "arbitrary"),
        name="moe_group_ffn",
    )(slot_expert, slot, n_valid.reshape(1), xs, wg, wu, wd)

    return pl.pallas_call(
        _combine_kernel,
        grid=(bn, nt),
        in_specs=[dspec, seq, mod, rec, anyspec],
        out_specs=seq,
        out_shape=jax.ShapeDtypeStruct(x.shape, F32),
        scratch_shapes=[pltpu.VMEM((2, tm, d), F32), pltpu.SemaphoreType.DMA((2,))],
        compiler_params=_cparams("arbitrary", "arbitrary"),
        name="moe_combine",
    )(dest, x, gate, info, ys)


def _final_norm_kernel(x_ref, g_ref, o_ref):
    x = x_ref[0]
    o_ref[0] = x * lax.rsqrt(jnp.mean(x * x, axis=-1, keepdims=True) + RMS_EPS) * g_ref[...]


def _final_norm(x, g):
    bn, t, d = x.shape
    tm = min(1024, t)
    seq = pl.BlockSpec((1, tm, d), lambda b, i: (b, i, 0))
    return pl.pallas_call(
        _final_norm_kernel,
        grid=(bn, t // tm),
        in_specs=[seq, pl.BlockSpec((1, d), lambda b, i: (0, 0))],
        out_specs=seq,
        out_shape=jax.ShapeDtypeStruct(x.shape, F32),
        compiler_params=_cparams("parallel", "parallel"),
        name="final_norm",
    )(x, g)


def _head_sum_matrix():
    idx = np.arange(DN_WIDTH) // HEAD_DIM
    return jnp.asarray(idx[:, None] == idx[None, :], dtype=BF16)


def kernel(x, c, ctx, c_ctx, w_mod, b_mod, norm_mix_g, norm_ffn_g, w_in, dn_conv_w, dn_a_log, dn_dt_bias, dn_norm_g, hy_conv_w, hy_w1, hy_b1, hy_freq1, hy_w2, hy_b2, hy_freq2, hy_w3, hy_b3, hy_skip, sw_sink, w_out, ffn_w_gate, ffn_w_up, ffn_w_down, moe_router, moe_w_gate, moe_w_up, moe_w_down, final_norm_g):
    bn, seq_len, d = x.shape
    ctx_len = ctx.shape[1]
    depth = w_mod.shape[0]

    c_rows = jnp.concatenate([c, c_ctx[None, :], jnp.zeros((SUBLANES - bn - 1, d), F32)], axis=0)
    mods = _modulation(c_rows, w_mod, b_mod).reshape(depth, SUBLANES, N_MOD, d)

    rope_tabs = _rope_tables(seq_len)
    gsum = _head_sum_matrix()
    cm_c, sm_c = _dft_matrices(ctx_len)
    hy_params = (hy_w1, hy_b1, hy_freq1, hy_w2, hy_b2, hy_freq2, hy_w3, hy_b3)
    fft_x = _fft_consts(seq_len)
    taps_x = _hy_taps(seq_len, *hy_params)
    hre_x, him_x = _hy_spec_fft(fft_x, taps_x[4], taps_x[2])
    hre_c, him_c = _hy_spec_dense(cm_c, sm_c, *_hy_taps(ctx_len, *hy_params)[:4])
    router = jnp.pad(moe_router, ((0, 0), (0, 0), (0, LANES - N_EXPERTS)))
    zero_state = jnp.zeros((bn, N_DIR, DN_HEADS, HEAD_DIM, HEAD_DIM), F32)

    for layer in range(depth):
        last = layer == depth - 1
        mod_x = [mods[layer, :bn, m][:, None, :] for m in range(N_MOD)]
        mod_c = [jnp.broadcast_to(mods[layer, bn, m][None, None, :], (bn, 1, d)) for m in range(N_MOD)]
        g_mix = norm_mix_g[layer][None, :]
        g_ffn = norm_ffn_g[layer][None, :]
        w_in_l = _relayout_w_in(w_in[layer])
        w_out_l = w_out[layer].astype(BF16)
        gn = jnp.tile(dn_norm_g[layer], DN_HEADS)[None, :]

        qkv_x, z_x, hyp_x, swq_x, swk_x, swv_x, ab_x = _in_proj(x, g_mix, mod_x[0], mod_x[1], w_in_l, rope_tabs)
        qkv_c, z_c, hyp_c, swq_c, swk_c, swv_c, ab_c = _in_proj(ctx, g_mix, mod_c[0], mod_c[1], w_in_l, None)

        dn_in_c = _dn_prep(qkv_c, ab_c, dn_conv_w[layer], dn_a_log[layer], dn_dt_bias[layer], gsum)
        dn_in_x = _dn_prep(qkv_x, ab_x, dn_conv_w[layer], dn_a_log[layer], dn_dt_bias[layer], gsum)
        of_c, ob_c, state_c = _dn_scan(*dn_in_c, zero_state)
        of_x, ob_x, _ = _dn_scan(*dn_in_x, state_c)

        hy_x = _hyena_fft(hyp_x, hy_conv_w[layer], fft_x, hre_x[layer], him_x[layer], hy_skip[layer])
        sw_x = _attention(swq_x, swk_x, swv_x, swk_c, swv_c, sw_sink[layer], True)
        x = _out_proj(of_x, ob_x, z_x, gn, gsum, hy_x, sw_x, x, mod_x[2], w_out_l)

        if not last:
            hy_c = _hyena(hyp_c, hy_conv_w[layer], cm_c, sm_c, hre_c[layer], him_c[layer], hy_skip[layer])
            sw_c = _attention(swq_c, None, None, swk_c, swv_c, sw_sink[layer], False)
            ctx = _out_proj(of_c, ob_c, z_c, gn, gsum, hy_c, sw_c, ctx, mod_c[2], w_out_l)

        i = layer // 2
        streams = [(x, mod_x)] if last else [(x, mod_x), (ctx, mod_c)]
        outs = []
        for s, mod in streams:
            if layer % 2 == 0:
                outs.append(_ffn(s, g_ffn, mod[3], mod[4], mod[5], ffn_w_gate[i].astype(BF16),
                                 ffn_w_up[i].astype(BF16), ffn_w_down[i].astype(BF16)))
            else:
                shape = s.shape
                if mod is mod_c:
                    s = s.reshape(1, -1, d)
                    mod = [m[:1] for m in mod]
                outs.append(_moe(s, g_ffn, mod[3], mod[4], mod[5], router[i], moe_w_gate[i].astype(BF16),
                                 moe_w_up[i].astype(BF16), moe_w_down[i].astype(BF16)).reshape(shape))
        x = outs[0]
        if not last:
            ctx = outs[1]

    return _final_norm(x, final_norm_g[None, :])
```
